```python
import math
import jax, jax.numpy as jnp
from jax import lax
import numpy as np

D_MODEL = 2048
BATCH = 4
SEQ = 4096
DEPTH = 1
DEC_BATCH = 8
DEC_SEQ = 16
PAST_LEN = 2048

CHUNK = 64
N_MEM = 256
LRU_WIDTH = 1024
LRU_BLOCKS = 16
LRU_BLOCK = LRU_WIDTH // LRU_BLOCKS
CONV_W = 4
LRU_C = 8.0
DIFF_HEADS = 8
DIFF_HD = 64
DIFF_VD = 2 * DIFF_HD
ATTN_WIDTH = DIFF_HEADS * DIFF_VD
MIX_WIDTH = LRU_WIDTH + ATTN_WIDTH
IN_COLS = 2 * LRU_WIDTH + 3 * ATTN_WIDTH
X_HEADS = 4
X_HD = D_MODEL // X_HEADS
N_GROUPS = 4
EXP_PER_GROUP = 4
N_EXPERTS = N_GROUPS * EXP_PER_GROUP
EXPERT_FF = 512
TOP_K_INNER = 2
Q_BLOCK = 128
EPS = 1e-6

kernel_name = 'hybrid_lru_diffattn_hmoe_stream_step'

F32 = jnp.float32


def rms_norm(x, g):
    xf = x.astype(F32)
    y = xf * lax.rsqrt(jnp.mean(xf * xf, axis=-1, keepdims=True) + EPS)
    return (y * g.astype(F32)).astype(x.dtype)


def lambda_init(layer):
    return 0.8 - 0.6 * math.exp(-0.3 * layer)


def alibi_slopes():
    return 2.0 ** (-8.0 * jnp.arange(1, DIFF_HEADS + 1, dtype=F32) / DIFF_HEADS)


def causal_conv(xb, buf, w, b):
    t = xb.shape[1]
    xp = jnp.concatenate([buf.astype(xb.dtype), xb], axis=1)
    y = b.astype(F32) + sum(w[j].astype(F32) * xp[:, j:j + t].astype(F32) for j in range(CONV_W))
    return y, xp[:, -(CONV_W - 1):]


def rg_lru(xc, h0, w_a, b_a, w_x, b_x, lam):
    bsz, t, _ = xc.shape
    xblk = xc.reshape(bsz, t, LRU_BLOCKS, LRU_BLOCK)
    r = jax.nn.sigmoid(jnp.einsum('btni,nij->btnj', xblk, w_a.astype(F32)) + b_a.astype(F32)).reshape(bsz, t, LRU_WIDTH)
    i = jax.nn.sigmoid(jnp.einsum('btni,nij->btnj', xblk, w_x.astype(F32)) + b_x.astype(F32)).reshape(bsz, t, LRU_WIDTH)
    log_a = -LRU_C * r * jax.nn.softplus(-lam.astype(F32))
    a = jnp.exp(log_a)
    u = jnp.sqrt(-jnp.expm1(2.0 * log_a)) * (i * xc)
    u = u.at[:, 0].add(a[:, 0] * h0.astype(F32))

    def combine(left, right):
        a_l, b_l = left
        a_r, b_r = right
        return a_l * a_r, a_r * b_l + b_r

    _, h = lax.associative_scan(combine, (a, u), axis=1)
    return h, h[:, -1]


def diff_attention(q, k, v, q_pos, k_pos, lam, lam_init, subln_g):
    bsz, tq = q.shape[:2]
    tk = k.shape[1]
    qf = q.astype(F32).reshape(bsz, tq, DIFF_HEADS, 2, DIFF_HD)
    kf = k.astype(F32).reshape(bsz, tk, DIFF_HEADS, 2, DIFF_HD)
    s = jnp.einsum('bqhmd,bkhmd->bmhqk', qf, kf) * DIFF_HD ** -0.5
    dist = jnp.abs(q_pos[:, None] - k_pos[None, :]).astype(F32)
    bias = -alibi_slopes()[:, None, None] * dist
    allowed = (k_pos[None, :] // CHUNK) <= (q_pos[:, None] // CHUNK)
    s = jnp.where(allowed, s + bias, -jnp.inf)
    p = jax.nn.softmax(s, axis=-1)
    w = p[:, 0] - lam * p[:, 1]
    o = jnp.einsum('bhqk,bkhe->bqhe', w, v.astype(F32))
    o = o * lax.rsqrt(jnp.mean(o * o, axis=-1, keepdims=True) + EPS) * subln_g.astype(F32) * (1.0 - lam_init)
    return o.reshape(bsz, tq, ATTN_WIDTH)


def blocked_diff_attention(q, k, v, offset, k_pos, lam, lam_init, subln_g):
    bsz, t = q.shape[:2]
    nb = t // Q_BLOCK
    qb = q.reshape(bsz, nb, Q_BLOCK, q.shape[-1]).transpose(1, 0, 2, 3)
    pb = offset + jnp.arange(t, dtype=jnp.int32).reshape(nb, Q_BLOCK)
    ob = lax.map(lambda args: diff_attention(args[0], k, v, args[1], k_pos, lam, lam_init, subln_g), (qb, pb))
    return ob.transpose(1, 0, 2, 3).reshape(bsz, t, ATTN_WIDTH)


def parallel_mixer(h, conv_buf, h0, past_k, past_v, offset, blocked, p, lam_init):
    bsz, t, _ = h.shape
    z = h @ p['w_in']
    xb = z[..., :LRU_WIDTH]
    gate = z[..., LRU_WIDTH:2 * LRU_WIDTH]
    q = z[..., 2 * LRU_WIDTH:2 * LRU_WIDTH + ATTN_WIDTH]
    k = z[..., 2 * LRU_WIDTH + ATTN_WIDTH:2 * LRU_WIDTH + 2 * ATTN_WIDTH]
    v = z[..., 2 * LRU_WIDTH + 2 * ATTN_WIDTH:]
    xc, new_buf = causal_conv(xb, conv_buf, p['conv_w'], p['conv_b'])
    hs, h_last = rg_lru(xc, h0, p['lru_wa'], p['lru_ba'], p['lru_wx'], p['lru_bx'], p['lru_lambda'])
    lru_out = hs * jax.nn.gelu(gate.astype(F32))
    k_new = k.reshape(bsz, t, DIFF_HEADS, 2 * DIFF_HD)
    v_new = v.reshape(bsz, t, DIFF_HEADS, DIFF_VD)
    k_all = jnp.concatenate([past_k.astype(k_new.dtype), k_new], axis=1)
    v_all = jnp.concatenate([past_v.astype(v_new.dtype), v_new], axis=1)
    k_pos = jnp.arange(offset + t, dtype=jnp.int32)
    lam = (jnp.exp(jnp.sum(p['lam_q1'].astype(F32) * p['lam_k1'].astype(F32)))
           - jnp.exp(jnp.sum(p['lam_q2'].astype(F32) * p['lam_k2'].astype(F32))) + lam_init)
    if blocked:
        att = blocked_diff_attention(q, k_all, v_all, offset, k_pos, lam, lam_init, p['subln_g'])
    else:
        q_pos = offset + jnp.arange(t, dtype=jnp.int32)
        att = diff_attention(q, k_all, v_all, q_pos, k_pos, lam, lam_init, p['subln_g'])
    mixed = jnp.concatenate([lru_out.astype(h.dtype), att.astype(h.dtype)], axis=-1) @ p['w_out']
    return mixed, new_buf, h_last, k_new, v_new


def memory_kv(mem, g, wk, wv):
    bsz = mem.shape[0]
    m = rms_norm(mem, g)
    return (m @ wk).reshape(bsz, N_MEM, X_HEADS, X_HD), (m @ wv).reshape(bsz, N_MEM, X_HEADS, X_HD)


def cross_attention(h, mk, mv, wq, wo):
    bsz, t, _ = h.shape
    q = (h @ wq).astype(F32).reshape(bsz, t, X_HEADS, X_HD)
    s = jnp.einsum('bqhd,bkhd->bhqk', q, mk.astype(F32)) * X_HD ** -0.5
    pr = jax.nn.softmax(s, axis=-1)
    o = jnp.einsum('bhqk,bkhd->bqhd', pr, mv.astype(F32)).reshape(bsz, t, D_MODEL)
    return o.astype(h.dtype) @ wo


def hier_moe(h, w_group, w_router, w_gate, w_up, w_down):
    bsz, t, d = h.shape
    hf = h.reshape(bsz * t, d)
    g_prob = jax.nn.softmax((hf @ w_group).astype(F32), axis=-1)
    g_top, g_idx = lax.top_k(g_prob, 1)
    e_logits = (hf @ w_router).astype(F32).reshape(-1, N_GROUPS, EXP_PER_GROUP)
    e_in_group = jnp.einsum('ng,nge->ne', jax.nn.one_hot(g_idx[:, 0], N_GROUPS, dtype=F32), e_logits)
    e_top, e_idx = lax.top_k(e_in_group, TOP_K_INNER)
    gate = g_top * jax.nn.softmax(e_top, axis=-1)
    combine = jnp.einsum('nke,nk->ne', jax.nn.one_hot(g_idx * EXP_PER_GROUP + e_idx, N_EXPERTS, dtype=F32), gate)
    y = jnp.zeros((bsz * t, d), F32)
    for e in range(N_EXPERTS):
        act = jax.nn.silu(hf @ w_gate[e]) * (hf @ w_up[e])
        y = y + combine[:, e:e + 1] * (act @ w_down[e]).astype(F32)
    return y.reshape(bsz, t, d).astype(h.dtype)


def trunk_layer(x, mem_k, mem_v, conv_buf, h0, past_k, past_v, offset, blocked, p, lam_init):
    mixed, new_buf, h_last, k_new, v_new = parallel_mixer(rms_norm(x, p['norm_mix_g']), conv_buf, h0, past_k, past_v,
                                                          offset, blocked, p, lam_init)
    x = x + mixed
    x = x + cross_attention(rms_norm(x, p['norm_cross_g']), mem_k, mem_v, p['xq_w'], p['xo_w'])
    x = x + hier_moe(rms_norm(x, p['norm_ffn_g']), p['router_group_w'], p['router_expert_w'],
                     p['exp_gate'], p['exp_up'], p['exp_down'])
    return x, new_buf, h_last, k_new, v_new


def setup_inputs(seed: int = 0) -> dict:
    key = jax.random.key(seed)
    ks = list(jax.random.split(key, 40))

    def nrm(i, shape, scale):
        return jax.random.normal(ks[i], shape, F32) * scale

    def gain(i, shape):
        return 1.0 + nrm(i, shape, 0.01)

    u = jax.random.uniform(ks[39], (DEPTH, LRU_WIDTH), F32, minval=0.9, maxval=0.999)
    a0 = u ** (1.0 / LRU_C)
    lru_lambda = jnp.log(a0) - jnp.log1p(-a0)
    return {
        'x_prompt': nrm(0, (BATCH, SEQ, D_MODEL), 1.0),
        'x_sample': nrm(1, (DEC_BATCH, DEC_SEQ, D_MODEL), 1.0),
        'cache_diff_k': nrm(2, (DEPTH, DEC_BATCH, PAST_LEN, DIFF_HEADS, 2 * DIFF_HD), 1.0),
        'cache_diff_v': nrm(3, (DEPTH, DEC_BATCH, PAST_LEN, DIFF_HEADS, DIFF_VD), 1.0),
        'cache_mem_k': nrm(4, (DEPTH, DEC_BATCH, N_MEM, X_HEADS, X_HD), 1.0),
        'cache_mem_v': nrm(5, (DEPTH, DEC_BATCH, N_MEM, X_HEADS, X_HD), 1.0),
        'state_conv': nrm(6, (DEPTH, DEC_BATCH, CONV_W - 1, LRU_WIDTH), 1.0),
        'state_lru': nrm(7, (DEPTH, DEC_BATCH, LRU_WIDTH), 0.5),
        'mem_prompt': nrm(8, (BATCH, N_MEM, D_MODEL), 1.0),
        'norm_mix_g': gain(9, (DEPTH, D_MODEL)),
        'w_in': nrm(10, (DEPTH, D_MODEL, IN_COLS), D_MODEL ** -0.5),
        'conv_w': nrm(11, (DEPTH, CONV_W, LRU_WIDTH), CONV_W ** -0.5),
        'conv_b': nrm(12, (DEPTH, LRU_WIDTH), 0.01),
        'lru_wa': nrm(13, (DEPTH, LRU_BLOCKS, LRU_BLOCK, LRU_BLOCK), LRU_BLOCK ** -0.5),
        'lru_ba': nrm(14, (DEPTH, LRU_BLOCKS, LRU_BLOCK), 0.01),
        'lru_wx': nrm(15, (DEPTH, LRU_BLOCKS, LRU_BLOCK, LRU_BLOCK), LRU_BLOCK ** -0.5),
        'lru_bx': nrm(16, (DEPTH, LRU_BLOCKS, LRU_BLOCK), 0.01),
        'lru_lambda': lru_lambda,
        'lam_q1': nrm(17, (DEPTH, DIFF_HD), 0.1),
        'lam_k1': nrm(18, (DEPTH, DIFF_HD), 0.1),
        'lam_q2': nrm(19, (DEPTH, DIFF_HD), 0.1),
        'lam_k2': nrm(20, (DEPTH, DIFF_HD), 0.1),
        'subln_g': gain(21, (DEPTH, DIFF_VD)),
        'w_out': nrm(22, (DEPTH, MIX_WIDTH, D_MODEL), MIX_WIDTH ** -0.5),
        'norm_cross_g': gain(23, (DEPTH, D_MODEL)),
        'norm_mem_g': gain(24, (DEPTH, D_MODEL)),
        'xq_w': nrm(25, (DEPTH, D_MODEL, D_MODEL), D_MODEL ** -0.5),
        'xk_w': nrm(26, (DEPTH, D_MODEL, D_MODEL), D_MODEL ** -0.5),
        'xv_w': nrm(27, (DEPTH, D_MODEL, D_MODEL), D_MODEL ** -0.5),
        'xo_w': nrm(28, (DEPTH, D_MODEL, D_MODEL), D_MODEL ** -0.5),
        'norm_ffn_g': gain(29, (DEPTH, D_MODEL)),
        'router_group_w': nrm(30, (DEPTH, D_MODEL, N_GROUPS), D_MODEL ** -0.5),
        'router_expert_w': nrm(31, (DEPTH, D_MODEL, N_EXPERTS), D_MODEL ** -0.5),
        'exp_gate': nrm(32, (DEPTH, N_EXPERTS, D_MODEL, EXPERT_FF), D_MODEL ** -0.5),
        'exp_up': nrm(33, (DEPTH, N_EXPERTS, D_MODEL, EXPERT_FF), D_MODEL ** -0.5),
        'exp_down': nrm(34, (DEPTH, N_EXPERTS, EXPERT_FF, D_MODEL), EXPERT_FF ** -0.5),
        'final_norm_g': gain(35, (D_MODEL,)),
    }


def reference(x_prompt, x_sample, cache_diff_k, cache_diff_v, cache_mem_k, cache_mem_v, state_conv, state_lru,
              mem_prompt, norm_mix_g, w_in, conv_w, conv_b, lru_wa, lru_ba, lru_wx, lru_bx, lru_lambda,
              lam_q1, lam_k1, lam_q2, lam_k2, subln_g, w_out, norm_cross_g, norm_mem_g, xq_w, xk_w, xv_w, xo_w,
              norm_ffn_g, router_group_w, router_expert_w, exp_gate, exp_up, exp_down, final_norm_g):
    bp = x_prompt.shape[0]
    past = cache_diff_k.shape[2]
    xp, xs = x_prompt, x_sample
    kp_l, vp_l, mkp_l, mvp_l, cp_l, hp_l = [], [], [], [], [], []
    ks_l, vs_l, cs_l, hs_l = [], [], [], []
    for l in range(DEPTH):
        p = dict(norm_mix_g=norm_mix_g[l], w_in=w_in[l], conv_w=conv_w[l], conv_b=conv_b[l],
                 lru_wa=lru_wa[l], lru_ba=lru_ba[l], lru_wx=lru_wx[l], lru_bx=lru_bx[l], lru_lambda=lru_lambda[l],
                 lam_q1=lam_q1[l], lam_k1=lam_k1[l], lam_q2=lam_q2[l], lam_k2=lam_k2[l], subln_g=subln_g[l],
                 w_out=w_out[l], norm_cross_g=norm_cross_g[l], xq_w=xq_w[l], xo_w=xo_w[l],
                 norm_ffn_g=norm_ffn_g[l], router_group_w=router_group_w[l], router_expert_w=router_expert_w[l],
                 exp_gate=exp_gate[l], exp_up=exp_up[l], exp_down=exp_down[l])
        li = lambda_init(l)
        mk_p, mv_p = memory_kv(mem_prompt, norm_mem_g[l], xk_w[l], xv_w[l])
        empty_k = jnp.zeros((bp, 0, DIFF_HEADS, 2 * DIFF_HD), x_prompt.dtype)
        empty_v = jnp.zeros((bp, 0, DIFF_HEADS, DIFF_VD), x_prompt.dtype)
        zero_buf = jnp.zeros((bp, CONV_W - 1, LRU_WIDTH), x_prompt.dtype)
        zero_h = jnp.zeros((bp, LRU_WIDTH), F32)
        xp, cb_p, hl_p, k_p, v_p = trunk_layer(xp, mk_p, mv_p, zero_buf, zero_h, empty_k, empty_v, 0, True, p, li)
        xs, cb_s, hl_s, k_s, v_s = trunk_layer(xs, cache_mem_k[l], cache_mem_v[l], state_conv[l], state_lru[l],
                                               cache_diff_k[l], cache_diff_v[l], past, False, p, li)
        kp_l.append(k_p); vp_l.append(v_p); mkp_l.append(mk_p); mvp_l.append(mv_p)
        cp_l.append(cb_p); hp_l.append(hl_p)
        ks_l.append(k_s); vs_l.append(v_s); cs_l.append(cb_s); hs_l.append(hl_s.astype(state_lru.dtype))
    y_prompt = rms_norm(xp, final_norm_g)
    y_sample = rms_norm(xs, final_norm_g)
    new_diff_k_prompt = jnp.stack(kp_l)
    new_diff_v_prompt = jnp.stack(vp_l)
    new_mem_k_prompt = jnp.stack(mkp_l)
    new_mem_v_prompt = jnp.stack(mvp_l)
    new_conv_prompt = jnp.stack(cp_l)
    new_lru_prompt = jnp.stack(hp_l)
    new_diff_k_sample = jnp.stack(ks_l)
    new_diff_v_sample = jnp.stack(vs_l)
    new_conv_sample = jnp.stack(cs_l)
    new_lru_sample = jnp.stack(hs_l)
    return (y_prompt, y_sample, new_diff_k_prompt, new_diff_v_prompt, new_mem_k_prompt, new_mem_v_prompt,
            new_conv_prompt, new_lru_prompt, new_diff_k_sample, new_diff_v_sample, new_conv_sample, new_lru_sample)
```

```python
import functools
import math

import jax
import jax.numpy as jnp
from jax import lax
from jax.experimental import pallas as pl
from jax.experimental.pallas import tpu as pltpu

F32 = jnp.float32
BF16 = jnp.bfloat16
HIGHEST = lax.Precision.HIGHEST

CHUNK = 64
CONV_W = 4
LRU_C = 8.0
LRU_BLOCK = 64
DIFF_HEADS = 8
DIFF_HD = 64
DIFF_VD = 2 * DIFF_HD
X_HEADS = 4
N_GROUPS = 4
EXP_PER_GROUP = 4
N_EXPERTS = N_GROUPS * EXP_PER_GROUP
EPS = 1e-6

LANES = 128
SUBLANES = 8
MXU_DIM = 256
VMEM_LIMIT_BYTES = 56 * 1024 * 1024


def _params(semantics):
    return pltpu.CompilerParams(dimension_semantics=semantics, vmem_limit_bytes=VMEM_LIMIT_BYTES)


def _dot(a, b, precise):
    if precise:
        return jnp.dot(a.astype(F32), b.astype(F32), precision=HIGHEST, preferred_element_type=F32)
    return jnp.dot(a.astype(BF16), b.astype(BF16), preferred_element_type=F32)


def _dot_nt(a, b, precise):
    dims = (((1,), (1,)), ((), ()))
    if precise:
        return lax.dot_general(a.astype(F32), b.astype(F32), dims, precision=HIGHEST, preferred_element_type=F32)
    return lax.dot_general(a.astype(BF16), b.astype(BF16), dims, preferred_element_type=F32)


def _rms(x, g):
    return x * lax.rsqrt(jnp.mean(x * x, axis=-1, keepdims=True) + EPS) * g


def _sigmoid(x):
    return 1.0 / (1.0 + jnp.exp(-x))


def _gelu_tanh(x):
    c = math.sqrt(2.0 / math.pi)
    return 0.5 * x * (1.0 + jnp.tanh(c * (x + 0.044715 * (x * x * x))))


def _norm_linear_kernel(x_ref, g_ref, w_ref, *refs, n_out, precise):
    out_refs = refs[:n_out]
    h_ref = refs[n_out]
    j = pl.program_id(1)

    @pl.when(j == 0)
    def _():
        h_ref[...] = _rms(x_ref[...], g_ref[...]).astype(h_ref.dtype)

    if n_out == 1:
        out_refs[0][...] = _dot(h_ref[...], w_ref[...], precise).astype(out_refs[0].dtype)
    else:
        for c in range(n_out):
            @pl.when(j == c)
            def _(c=c):
                out_refs[c][...] = _dot(h_ref[...], w_ref[...], precise).astype(out_refs[c].dtype)


def norm_linear(x, g, w, *, tm, tn, out_dtypes, split, precise):
    t, k = x.shape
    n = w.shape[1]
    tm = min(tm, t)
    nj = n // tn
    n_out = nj if split else 1
    if split:
        out_shape = [jax.ShapeDtypeStruct((t, tn), out_dtypes[c]) for c in range(nj)]
        out_specs = [pl.BlockSpec((tm, tn), lambda i, j: (i, 0)) for _ in range(nj)]
    else:
        out_shape = [jax.ShapeDtypeStruct((t, n), out_dtypes[0])]
        out_specs = [pl.BlockSpec((tm, tn), lambda i, j: (i, j))]
    outs = pl.pallas_call(
        functools.partial(_norm_linear_kernel, n_out=n_out, precise=precise),
        grid=(t // tm, nj),
        in_specs=[
            pl.BlockSpec((tm, k), lambda i, j: (i, 0)),
            pl.BlockSpec((1, k), lambda i, j: (0, 0)),
            pl.BlockSpec((k, tn), lambda i, j: (0, j)),
        ],
        out_specs=out_specs,
        out_shape=out_shape,
        scratch_shapes=[pltpu.VMEM((tm, k), F32 if precise else BF16)],
        compiler_params=_params(("parallel", "arbitrary")),
        name="norm_linear",
    )(x, g.reshape(1, k), w)
    return outs


def _linear_res_kernel(*refs, n_in, precise):
    res_ref = refs[0]
    a_refs = refs[1:1 + n_in]
    w_refs = refs[1 + n_in:1 + 2 * n_in]
    out_ref = refs[1 + 2 * n_in]
    acc = res_ref[...]
    for a_ref, w_ref in zip(a_refs, w_refs):
        acc = acc + _dot(a_ref[...], w_ref[...], precise)
    out_ref[...] = acc


def linear_residual(res, a_list, w, *, tm, tn, precise):
    t, n = res.shape
    tm = min(tm, t)
    n_in = len(a_list)
    kc = a_list[0].shape[1]
    in_specs = [pl.BlockSpec((tm, tn), lambda i, j: (i, j))]
    in_specs += [pl.BlockSpec((tm, kc), lambda i, j: (i, 0)) for _ in range(n_in)]
    in_specs += [pl.BlockSpec((kc, tn), lambda i, j, c=c: (c, j)) for c in range(n_in)]
    return pl.pallas_call(
        functools.partial(_linear_res_kernel, n_in=n_in, precise=precise),
        grid=(t // tm, n // tn),
        in_specs=in_specs,
        out_specs=pl.BlockSpec((tm, tn), lambda i, j: (i, j)),
        out_shape=jax.ShapeDtypeStruct((t, n), F32),
        compiler_params=_params(("parallel", "arbitrary")),
        name="linear_residual",
    )(res, *a_list, *([w] * n_in))


def _lru_kernel(xb_ref, gate_ref, cbuf_ref, h0_ref, cw_ref, cb_ref, wa_ref, ba_ref, wx_ref, bx_ref, lam_ref,
                out_ref, hlast_ref, xpad, hcar, a_s, u_s, *, tc, width, precise):
    c = pl.program_id(1)
    nslab = width // MXU_DIM
    ngrp = tc // SUBLANES

    @pl.when(c == 0)
    def _():
        xpad[pl.ds(0, SUBLANES), :] = cbuf_ref[0]
        hcar[...] = h0_ref[0]

    xpad[pl.ds(SUBLANES, tc), :] = xb_ref[0]
    xc = cb_ref[...] + cw_ref[pl.ds(CONV_W - 1, 1), :] * xpad[pl.ds(SUBLANES, tc), :]
    for j in range(CONV_W - 1):
        xc = xc + cw_ref[pl.ds(j, 1), :] * xpad[pl.ds(SUBLANES - (CONV_W - 1) + j, tc), :]
    xpad[pl.ds(0, SUBLANES), :] = xpad[pl.ds(tc, SUBLANES), :]

    lam = lam_ref[...]
    softplus_neg = jnp.maximum(-lam, 0.0) + jnp.log1p(jnp.exp(-jnp.abs(lam)))
    c8 = -LRU_C * softplus_neg

    sub = lax.broadcasted_iota(jnp.int32, (ngrp, SUBLANES, MXU_DIM), 1)
    for s in range(nslab):
        cols = slice(s * MXU_DIM, (s + 1) * MXU_DIM)
        xs = xc[:, cols]
        r = _sigmoid(_dot(xs, wa_ref[s], precise) + ba_ref[:, cols])
        i = _sigmoid(_dot(xs, wx_ref[s], precise) + bx_ref[:, cols])
        a = jnp.exp(c8[:, cols] * r)
        u = jnp.sqrt(1.0 - a * a) * (i * xs)
        a3 = a.reshape(ngrp, SUBLANES, MXU_DIM)
        u3 = u.reshape(ngrp, SUBLANES, MXU_DIM)
        d = 1
        while d < SUBLANES:
            a_sh = pltpu.roll(a3, d, 1)
            u_sh = pltpu.roll(u3, d, 1)
            keep = sub >= d
            u3 = jnp.where(keep, u3 + a3 * u_sh, u3)
            a3 = jnp.where(keep, a3 * a_sh, a3)
            d *= 2
        a_s[:, cols] = a3.reshape(tc, MXU_DIM)
        u_s[:, cols] = u3.reshape(tc, MXU_DIM)

    def body(g, hin):
        rows = pl.ds(pl.multiple_of(g * SUBLANES, SUBLANES), SUBLANES)
        h = u_s[rows, :] + a_s[rows, :] * hin
        u_s[rows, :] = h
        return h[SUBLANES - 1:SUBLANES, :]

    hfin = lax.fori_loop(0, ngrp, body, hcar[...])
    hcar[...] = hfin
    out_ref[0] = (u_s[...] * _gelu_tanh(gate_ref[0])).astype(out_ref.dtype)
    hlast_ref[0] = hfin


def _block_diag(w, per):
    nb, k, _ = w.shape
    w4 = w.reshape(nb // per, per, k, k)
    eye = jnp.eye(per, dtype=w.dtype)
    return jnp.einsum("cipq,ij->cipjq", w4, eye).reshape(nb // per, per * k, per * k)


def lru_mixer(xb, gate, conv_buf, h0, conv_w, conv_b, wa, ba, wx, bx, lam, *, tc, out_dtype, precise):
    b, t, width = xb.shape
    tc = min(tc, t)
    per = MXU_DIM // LRU_BLOCK
    wdt = F32 if precise else BF16
    wa_bd = _block_diag(wa, per).astype(wdt)
    wx_bd = _block_diag(wx, per).astype(wdt)
    nslab = wa_bd.shape[0]
    cbuf8 = jnp.concatenate([jnp.zeros((b, SUBLANES - (CONV_W - 1), width), F32), conv_buf.astype(F32)], axis=1)
    row = lambda v: v.reshape(1, width).astype(F32)
    vec_spec = pl.BlockSpec((1, width), lambda bi, ci: (0, 0))
    seq_spec = pl.BlockSpec((1, tc, width), lambda bi, ci: (bi, ci, 0))
    wspec = pl.BlockSpec((nslab, MXU_DIM, MXU_DIM), lambda bi, ci: (0, 0, 0))
    out, hlast = pl.pallas_call(
        functools.partial(_lru_kernel, tc=tc, width=width, precise=precise),
        grid=(b, t // tc),
        in_specs=[
            seq_spec, seq_spec,
            pl.BlockSpec((1, SUBLANES, width), lambda bi, ci: (bi, 0, 0)),
            pl.BlockSpec((1, 1, width), lambda bi, ci: (bi, 0, 0)),
            pl.BlockSpec((CONV_W, width), lambda bi, ci: (0, 0)),
            vec_spec, wspec, vec_spec, wspec, vec_spec, vec_spec,
        ],
        out_specs=[seq_spec, pl.BlockSpec((1, 1, width), lambda bi, ci: (bi, 0, 0))],
        out_shape=[jax.ShapeDtypeStruct((b, t, width), out_dtype), jax.ShapeDtypeStruct((b, 1, width), F32)],
        scratch_shapes=[
            pltpu.VMEM((tc + SUBLANES, width), F32),
            pltpu.VMEM((1, width), F32),
            pltpu.VMEM((tc, width), F32),
            pltpu.VMEM((tc, width), F32),
        ],
        compiler_params=_params(("parallel", "arbitrary")),
        name="lru_mixer",
    )(xb, gate, cbuf8, h0.reshape(b, 1, width).astype(F32), conv_w.astype(F32), row(conv_b), wa_bd,
      row(ba), wx_bd, row(bx), row(lam))
    return out, hlast.reshape(b, width)


def _diff_lambda(lq1, lk1, lq2, lk2, lam_init):
    s1 = jnp.sum(lq1[...] * lk1[...], axis=-1, keepdims=True)
    s2 = jnp.sum(lq2[...] * lk2[...], axis=-1, keepdims=True)
    return jnp.exp(s1) - jnp.exp(s2) + lam_init


def _diff_epilogue(o1, l1, o2, l2, lam, subln, lam_init):
    o = o1 / l1 - lam * (o2 / l2)
    return o * lax.rsqrt(jnp.mean(o * o, axis=-1, keepdims=True) + EPS) * subln * (1.0 - lam_init)


def _attn_prompt_kernel(slopes_ref, q_ref, k_ref, v_ref, lq1, lk1, lq2, lk2, subln_ref, out_ref,
                        kb, vb, qb, m_s, l_s, acc_s, *, tq, lam_init):
    h = pl.program_id(1)
    qi = pl.program_id(2)
    slope = slopes_ref[h]

    @pl.when(qi == 0)
    def _():
        kb[...] = k_ref[0].astype(BF16)
        vb[...] = v_ref[0].astype(BF16)

    qs = q_ref[0] * (DIFF_HD ** -0.5)
    qlane = lax.broadcasted_iota(jnp.int32, qs.shape, 1)
    qb[0] = jnp.where(qlane < DIFF_HD, qs, 0.0).astype(BF16)
    qb[1] = jnp.where(qlane >= DIFF_HD, qs, 0.0).astype(BF16)
    m_s[...] = jnp.full(m_s.shape, -jnp.inf, F32)
    l_s[...] = jnp.zeros(l_s.shape, F32)
    acc_s[...] = jnp.zeros(acc_s.shape, F32)

    rpos = lax.broadcasted_iota(jnp.int32, (tq, tq), 0)
    cpos = lax.broadcasted_iota(jnp.int32, (tq, tq), 1)
    rel = rpos - cpos

    def block(ki, diagonal):
        start = pl.multiple_of(ki * tq, tq)
        kblk = kb[pl.ds(start, tq), :]
        vblk = vb[pl.ds(start, tq), :]
        dist = jnp.abs(rel + (qi - ki) * tq).astype(F32)
        bias = -slope * dist
        if diagonal:
            allowed = (cpos // CHUNK) <= (rpos // CHUNK)
        for m in range(2):
            s = _dot_nt(qb[m], kblk, False) + bias
            if diagonal:
                s = jnp.where(allowed, s, -jnp.inf)
            m_old = m_s[m]
            m_new = jnp.maximum(m_old, jnp.max(s, axis=-1, keepdims=True))
            p = jnp.exp(s - m_new)
            alpha = jnp.exp(m_old - m_new)
            l_s[m] = alpha * l_s[m] + jnp.sum(p, axis=-1, keepdims=True)
            acc_s[m] = alpha * acc_s[m] + jnp.dot(p.astype(BF16), vblk, preferred_element_type=F32)
            m_s[m] = m_new

    def body(ki, carry):
        block(ki, False)
        return carry

    lax.fori_loop(0, qi, body, 0)
    block(qi, True)

    lam = _diff_lambda(lq1, lk1, lq2, lk2, lam_init)
    o = _diff_epilogue(acc_s[0], l_s[0], acc_s[1], l_s[1], lam, subln_ref[...], lam_init)
    out_ref[0] = o.astype(out_ref.dtype)


def _alibi_slopes():
    return 2.0 ** (-8.0 * jnp.arange(1, DIFF_HEADS + 1, dtype=F32) / DIFF_HEADS)


def diff_attention_prompt(q, k, v, lam_params, subln_g, lam_init, *, tq):
    b, t, aw = q.shape
    tq = min(tq, t)
    hd2 = 2 * DIFF_HD
    lrow = lambda p: p.reshape(1, DIFF_HD).astype(F32)
    lspec = pl.BlockSpec((1, DIFF_HD), lambda bi, hi, qi: (0, 0))
    kv_spec = pl.BlockSpec((1, t, hd2), lambda bi, hi, qi: (bi, 0, hi))
    q_spec = pl.BlockSpec((1, tq, hd2), lambda bi, hi, qi: (bi, qi, hi))
    return pl.pallas_call(
        functools.partial(_attn_prompt_kernel, tq=tq, lam_init=lam_init),
        grid=(b, DIFF_HEADS, t // tq),
        in_specs=[
            pl.BlockSpec(memory_space=pltpu.SMEM),
            q_spec, kv_spec, kv_spec, lspec, lspec, lspec, lspec,
            pl.BlockSpec((1, DIFF_VD), lambda bi, hi, qi: (0, 0)),
        ],
        out_specs=q_spec,
        out_shape=jax.ShapeDtypeStruct((b, t, aw), BF16),
        scratch_shapes=[
            pltpu.VMEM((t, hd2), BF16),
            pltpu.VMEM((t, DIFF_VD), BF16),
            pltpu.VMEM((2, tq, hd2), BF16),
            pltpu.VMEM((2, tq, 1), F32),
            pltpu.VMEM((2, tq, 1), F32),
            pltpu.VMEM((2, tq, DIFF_VD), F32),
        ],
        compiler_params=_params(("parallel", "parallel", "arbitrary")),
        name="diff_attention_prompt",
    )(_alibi_slopes(), q, k, v, *[lrow(p) for p in lam_params], subln_g.reshape(1, DIFF_VD).astype(F32))


def _attn_sample_kernel(slopes_ref, q_ref, kp_ref, vp_ref, kn_ref, vn_ref, lq1, lk1, lq2, lk2, subln_ref, out_ref,
                        *, past, tq, lam_init):
    h = pl.program_id(1)
    slope = slopes_ref[h]
    q = q_ref[0] * (DIFF_HD ** -0.5)
    kp, vp, kn, vn = kp_ref[0], vp_ref[0], kn_ref[0], vn_ref[0]

    def scores_bias(nk, k_off):
        qpos = past + lax.broadcasted_iota(jnp.int32, (tq, nk), 0)
        kpos = k_off + lax.broadcasted_iota(jnp.int32, (tq, nk), 1)
        bias = -slope * jnp.abs(qpos - kpos).astype(F32)
        allowed = (kpos // CHUNK) <= (qpos // CHUNK)
        return bias, allowed

    bias_p, ok_p = scores_bias(past, 0)
    bias_n, ok_n = scores_bias(tq, past)
    qlane = lax.broadcasted_iota(jnp.int32, q.shape, 1)
    outs = []
    for m in range(2):
        qm = jnp.where((qlane >= DIFF_HD) == (m == 1), q, 0.0)
        sp = jnp.where(ok_p, _dot_nt(qm, kp, True) + bias_p, -jnp.inf)
        sn = jnp.where(ok_n, _dot_nt(qm, kn, True) + bias_n, -jnp.inf)
        mx = jnp.maximum(jnp.max(sp, axis=-1, keepdims=True), jnp.max(sn, axis=-1, keepdims=True))
        pp = jnp.exp(sp - mx)
        pn = jnp.exp(sn - mx)
        l = jnp.sum(pp, axis=-1, keepdims=True) + jnp.sum(pn, axis=-1, keepdims=True)
        o = _dot(pp, vp, True) + _dot(pn, vn, True)
        outs.append((o, l))
    lam = _diff_lambda(lq1, lk1, lq2, lk2, lam_init)
    out_ref[0] = _diff_epilogue(outs[0][0], outs[0][1], outs[1][0], outs[1][1], lam, subln_ref[...], lam_init)


def diff_attention_sample(q, k_new, v_new, past_k, past_v, lam_params, subln_g, lam_init):
    b, t, aw = q.shape
    past = past_k.shape[1]
    hd2 = 2 * DIFF_HD
    lrow = lambda p: p.reshape(1, DIFF_HD).astype(F32)
    lspec = pl.BlockSpec((1, DIFF_HD), lambda bi, hi: (0, 0))
    new_spec = pl.BlockSpec((1, t, hd2), lambda bi, hi: (bi, 0, hi))
    past_spec = pl.BlockSpec((1, past, hd2), lambda bi, hi: (bi, 0, hi))
    return pl.pallas_call(
        functools.partial(_attn_sample_kernel, past=past, tq=t, lam_init=lam_init),
        grid=(b, DIFF_HEADS),
        in_specs=[
            pl.BlockSpec(memory_space=pltpu.SMEM),
            new_spec, past_spec, past_spec, new_spec, new_spec, lspec, lspec, lspec, lspec,
            pl.BlockSpec((1, DIFF_VD), lambda bi, hi: (0, 0)),
        ],
        out_specs=new_spec,
        out_shape=jax.ShapeDtypeStruct((b, t, aw), F32),
        compiler_params=_params(("parallel", "parallel")),
        name="diff_attention_sample",
    )(_alibi_slopes(), q, past_k, past_v, k_new, v_new, *[lrow(p) for p in lam_params],
      subln_g.reshape(1, DIFF_VD).astype(F32))


def _cross_kernel(q_ref, mk_ref, mv_ref, out_ref, *, precise):
    d = q_ref.shape[-1]
    hd = d // X_HEADS
    for hh in range(X_HEADS):
        cols = slice(hh * hd, (hh + 1) * hd)
        s = _dot_nt(q_ref[0, :, cols], mk_ref[0, :, cols], precise) * (hd ** -0.5)
        p = jnp.exp(s - jnp.max(s, axis=-1, keepdims=True))
        l = jnp.sum(p, axis=-1, keepdims=True)
        out_ref[0, :, cols] = (_dot(p, mv_ref[0, :, cols], precise) / l).astype(out_ref.dtype)


def cross_attention_core(q, mk, mv, *, tq, out_dtype, precise):
    b, t, d = q.shape
    nm = mk.shape[1]
    tq = min(tq, t)
    q_spec = pl.BlockSpec((1, tq, d), lambda bi, qi: (bi, qi, 0))
    m_spec = pl.BlockSpec((1, nm, d), lambda bi, qi: (bi, 0, 0))
    return pl.pallas_call(
        functools.partial(_cross_kernel, precise=precise),
        grid=(b, t // tq),
        in_specs=[q_spec, m_spec, m_spec],
        out_specs=q_spec,
        out_shape=jax.ShapeDtypeStruct((b, t, d), out_dtype),
        compiler_params=_params(("parallel", "arbitrary")),
        name="cross_attention_core",
    )(q, mk, mv)


def _first_argmax(vals, lane, valid):
    masked = jnp.where(valid, vals, -jnp.inf)
    mx = jnp.max(masked, axis=-1, keepdims=True)
    idx = jnp.min(jnp.where(masked == mx, lane, LANES), axis=-1, keepdims=True)
    return mx, idx


def _route(logits):
    lane = lax.broadcasted_iota(jnp.int32, logits.shape, 1).astype(F32)
    is_group = lane < N_GROUPS
    gmax, gidx = _first_argmax(logits, lane, is_group)
    gsum = jnp.sum(jnp.where(is_group, jnp.exp(logits - gmax), 0.0), axis=-1, keepdims=True)
    g_top = 1.0 / gsum
    lo = N_GROUPS + gidx * EXP_PER_GROUP
    in_group = (lane >= lo) & (lane < lo + EXP_PER_GROUP)
    e1, i1 = _first_argmax(logits, lane, in_group)
    e2, i2 = _first_argmax(logits, lane, in_group & (lane != i1))
    w2 = jnp.exp(e2 - e1)
    gate1 = g_top / (1.0 + w2)
    gate2 = g_top * w2 / (1.0 + w2)
    combine = jnp.where(lane == i1, gate1, 0.0) + jnp.where(lane == i2, gate2, 0.0)
    return pltpu.roll(combine, LANES - N_GROUPS, 1)


def _moe_dense_kernel(x_ref, g_ref, wr_ref, wg_ref, wu_ref, wd_ref, fg_ref, out_ref, h_s, comb_s, acc_s):
    e = pl.program_id(1)

    @pl.when(e == 0)
    def _():
        h = _rms(x_ref[...], g_ref[...])
        h_s[...] = h.astype(BF16)
        logits = jnp.dot(h, wr_ref[...], precision=HIGHEST, preferred_element_type=F32)
        comb_s[...] = _route(logits)
        acc_s[...] = x_ref[...]

    hb = h_s[...]
    act = jnp.dot(hb, wg_ref[0], preferred_element_type=F32)
    act = act * _sigmoid(act) * jnp.dot(hb, wu_ref[0], preferred_element_type=F32)
    y = jnp.dot(act.astype(BF16), wd_ref[0], preferred_element_type=F32)
    lane = lax.broadcasted_iota(jnp.int32, comb_s.shape, 1)
    ce = jnp.sum(jnp.where(lane == e, comb_s[...], 0.0), axis=-1, keepdims=True)
    acc_s[...] += ce * y

    @pl.when(e == pl.num_programs(1) - 1)
    def _():
        out_ref[...] = _rms(acc_s[...], fg_ref[...])


def moe_dense_final(x, g, w_router_pad, wg, wu, wd, final_g, *, tm):
    t, d = x.shape
    tm = min(tm, t)
    ne, _, ff = wg.shape
    return pl.pallas_call(
        _moe_dense_kernel,
        grid=(t // tm, ne),
        in_specs=[
            pl.BlockSpec((tm, d), lambda i, e: (i, 0)),
            pl.BlockSpec((1, d), lambda i, e: (0, 0)),
            pl.BlockSpec((d, LANES), lambda i, e: (0, 0)),
            pl.BlockSpec((1, d, ff), lambda i, e: (e, 0, 0)),
            pl.BlockSpec((1, d, ff), lambda i, e: (e, 0, 0)),
            pl.BlockSpec((1, ff, d), lambda i, e: (e, 0, 0)),
            pl.BlockSpec((1, d), lambda i, e: (0, 0)),
        ],
        out_specs=pl.BlockSpec((tm, d), lambda i, e: (i, 0)),
        out_shape=jax.ShapeDtypeStruct((t, d), F32),
        scratch_shapes=[pltpu.VMEM((tm, d), BF16), pltpu.VMEM((tm, LANES), F32), pltpu.VMEM((tm, d), F32)],
        compiler_params=_params(("parallel", "arbitrary")),
        name="moe_dense_final",
    )(x, g.reshape(1, d), w_router_pad, wg, wu, wd, final_g.reshape(1, d))


def _trunk(x, mem_k, mem_v, conv_buf, h0, past_k, past_v, p, lam_init, precise):
    b, t, d = x.shape
    n = b * t
    xf = x.reshape(n, d)
    act_dt = F32 if precise else BF16
    wsel = (lambda name: p[name]) if precise else (lambda name: p[name + "_bf16"])
    tm = 128 if precise else 512

    xb, gate, q, k, v = norm_linear(xf, p["norm_mix_g"], wsel("w_in"), tm=tm, tn=1024,
                                    out_dtypes=[F32] * 5, split=True, precise=precise)
    width = xb.shape[1]
    seq = lambda a: a.reshape(b, t, a.shape[-1])
    lru_out, h_last = lru_mixer(seq(xb), seq(gate), conv_buf, h0, p["conv_w"], p["conv_b"], p["lru_wa"],
                                p["lru_ba"].reshape(-1), p["lru_wx"], p["lru_bx"].reshape(-1), p["lru_lambda"],
                                tc=256, out_dtype=act_dt, precise=precise)
    lam_params = (p["lam_q1"], p["lam_k1"], p["lam_q2"], p["lam_k2"])
    if precise:
        att = diff_attention_sample(seq(q), seq(k), seq(v), past_k, past_v, lam_params, p["subln_g"], lam_init)
    else:
        att = diff_attention_prompt(seq(q), seq(k), seq(v), lam_params, p["subln_g"], lam_init, tq=512)
    x1 = linear_residual(xf, [lru_out.reshape(n, width), att.reshape(n, -1)], wsel("w_out"),
                         tm=tm, tn=1024, precise=precise)

    (qx,) = norm_linear(x1, p["norm_cross_g"], wsel("xq_w"), tm=tm, tn=1024, out_dtypes=[act_dt],
                        split=False, precise=precise)
    o = cross_attention_core(seq(qx), mem_k, mem_v, tq=512, out_dtype=act_dt, precise=precise)
    x2 = linear_residual(x1, [o.reshape(n, d)], wsel("xo_w"), tm=tm, tn=1024, precise=precise)

    y = moe_dense_final(x2, p["norm_ffn_g"], p["router_pad"], p["exp_gate_bf16"], p["exp_up_bf16"],
                        p["exp_down_bf16"], p["final_norm_g"], tm=tm)
    new_conv = seq(xb)[:, t - (CONV_W - 1):, :]
    return y.reshape(b, t, d), new_conv, h_last, k, v


def kernel(x_prompt, x_sample, cache_diff_k, cache_diff_v, cache_mem_k, cache_mem_v, state_conv, state_lru, mem_prompt, norm_mix_g, w_in, conv_w, conv_b, lru_wa, lru_ba, lru_wx, lru_bx, lru_lambda, lam_q1, lam_k1, lam_q2, lam_k2, subln_g, w_out, norm_cross_g, norm_mem_g, xq_w, xk_w, xv_w, xo_w, norm_ffn_g, router_group_w, router_expert_w, exp_gate, exp_up, exp_down, final_norm_g):
    depth = w_in.shape[0]
    assert depth == 1, "single-layer step"
    bp, tp, d = x_prompt.shape
    bs, ts, _ = x_sample.shape
    past = cache_diff_k.shape[2]
    n_mem = mem_prompt.shape[1]
    aw = DIFF_HEADS * DIFF_VD
    l = 0
    lam_init = 0.8 - 0.6 * math.exp(-0.3 * l)

    router = jnp.concatenate([router_group_w[l], router_expert_w[l]], axis=1)
    router_pad = jnp.pad(router, ((0, 0), (0, LANES - router.shape[1])))
    p = dict(norm_mix_g=norm_mix_g[l], w_in=w_in[l], conv_w=conv_w[l], conv_b=conv_b[l], lru_wa=lru_wa[l],
             lru_ba=lru_ba[l], lru_wx=lru_wx[l], lru_bx=lru_bx[l], lru_lambda=lru_lambda[l], lam_q1=lam_q1[l],
             lam_k1=lam_k1[l], lam_q2=lam_q2[l], lam_k2=lam_k2[l], subln_g=subln_g[l], w_out=w_out[l],
             norm_cross_g=norm_cross_g[l], xq_w=xq_w[l], xo_w=xo_w[l], norm_ffn_g=norm_ffn_g[l],
             router_pad=router_pad, final_norm_g=final_norm_g)
    for name in ("w_in", "w_out", "xq_w", "xo_w"):
        p[name + "_bf16"] = p[name].astype(BF16)
    p["exp_gate_bf16"] = exp_gate[l].astype(BF16)
    p["exp_up_bf16"] = exp_up[l].astype(BF16)
    p["exp_down_bf16"] = exp_down[l].astype(BF16)

    memf = mem_prompt.reshape(bp * n_mem, d)
    (mk_p,) = norm_linear(memf, norm_mem_g[l], xk_w[l].astype(BF16), tm=512, tn=1024, out_dtypes=[F32],
                          split=False, precise=False)
    (mv_p,) = norm_linear(memf, norm_mem_g[l], xv_w[l].astype(BF16), tm=512, tn=1024, out_dtypes=[F32],
                          split=False, precise=False)
    mk_p = mk_p.reshape(bp, n_mem, d)
    mv_p = mv_p.reshape(bp, n_mem, d)

    zero_buf = jnp.zeros((bp, CONV_W - 1, lru_lambda.shape[1]), F32)
    zero_h = jnp.zeros((bp, lru_lambda.shape[1]), F32)
    y_p, cb_p, hl_p, k_p, v_p = _trunk(x_prompt, mk_p, mv_p, zero_buf, zero_h, None, None, p, lam_init, False)
    y_s, cb_s, hl_s, k_s, v_s = _trunk(x_sample, cache_mem_k[l].reshape(bs, n_mem, d),
                                       cache_mem_v[l].reshape(bs, n_mem, d), state_conv[l], state_lru[l],
                                       cache_diff_k[l].reshape(bs, past, aw), cache_diff_v[l].reshape(bs, past, aw),
                                       p, lam_init, True)

    hd2 = 2 * DIFF_HD
    return (y_p, y_s,
            k_p.reshape(1, bp, tp, DIFF_HEADS, hd2), v_p.reshape(1, bp, tp, DIFF_HEADS, DIFF_VD),
            mk_p.reshape(1, bp, n_mem, X_HEADS, d // X_HEADS), mv_p.reshape(1, bp, n_mem, X_HEADS, d // X_HEADS),
            cb_p[None], hl_p[None],
            k_s.reshape(1, bs, ts, DIFF_HEADS, hd2), v_s.reshape(1, bs, ts, DIFF_HEADS, DIFF_VD),
            cb_s[None], hl_s[None].astype(state_lru.dtype))
```

```python
import functools
import math

import jax
import jax.numpy as jnp
from jax import lax
from jax.experimental import pallas as pl
from jax.experimental.pallas import tpu as pltpu

F32 = jnp.float32
BF16 = jnp.bfloat16
HIGHEST = lax.Precision.HIGHEST

CHUNK = 64
CONV_W = 4
LRU_C = 8.0
LRU_BLOCK = 64
DIFF_HEADS = 8
DIFF_HD = 64
DIFF_VD = 2 * DIFF_HD
X_HEADS = 4
N_GROUPS = 4
EXP_PER_GROUP = 4
N_EXPERTS = N_GROUPS * EXP_PER_GROUP
EPS = 1e-6

LANES = 128
SUBLANES = 8
MXU_DIM = 256
VMEM_LIMIT_BYTES = 56 * 1024 * 1024


def _params(semantics):
    return pltpu.CompilerParams(dimension_semantics=semantics, vmem_limit_bytes=VMEM_LIMIT_BYTES)


def _dot(a, b, precise=False):
    if precise:
        return jnp.dot(a.astype(F32), b.astype(F32), precision=HIGHEST, preferred_element_type=F32)
    return jnp.dot(a.astype(BF16), b.astype(BF16), preferred_element_type=F32)


def _dot_nt(a, b, precise=False):
    dims = (((1,), (1,)), ((), ()))
    if precise:
        return lax.dot_general(a.astype(F32), b.astype(F32), dims, precision=HIGHEST, preferred_element_type=F32)
    return lax.dot_general(a.astype(BF16), b.astype(BF16), dims, preferred_element_type=F32)


def _rms(x, g):
    return x * lax.rsqrt(jnp.mean(x * x, axis=-1, keepdims=True) + EPS) * g


def _sigmoid(x):
    return 1.0 / (1.0 + jnp.exp(-x))


def _gelu_tanh(x):
    c = math.sqrt(2.0 / math.pi)
    return 0.5 * x * (1.0 + jnp.tanh(c * (x + 0.044715 * (x * x * x))))


def _norm_linear_kernel(x_ref, g_ref, w_ref, *out_refs, tn):
    h = _rms(x_ref[...], g_ref[...]).astype(BF16)
    col = 0
    for o_ref in out_refs:
        for c in range(o_ref.shape[1] // tn):
            o_ref[:, c * tn:(c + 1) * tn] = jnp.dot(
                h, w_ref[:, col:col + tn], preferred_element_type=F32).astype(o_ref.dtype)
            col += tn


def norm_linear(x, g, w, *, tm, tn, out_widths, out_dtypes):
    t, k = x.shape
    n = w.shape[1]
    tm = min(tm, t)
    assert sum(out_widths) == n and all(wd % tn == 0 for wd in out_widths)
    return pl.pallas_call(
        functools.partial(_norm_linear_kernel, tn=tn),
        grid=(t // tm,),
        in_specs=[
            pl.BlockSpec((tm, k), lambda i: (i, 0)),
            pl.BlockSpec((1, k), lambda i: (0, 0)),
            pl.BlockSpec((k, n), lambda i: (0, 0)),
        ],
        out_specs=[pl.BlockSpec((tm, wd), lambda i: (i, 0)) for wd in out_widths],
        out_shape=[jax.ShapeDtypeStruct((t, wd), dt) for wd, dt in zip(out_widths, out_dtypes)],
        compiler_params=_params(("parallel",)),
        name="norm_linear",
    )(x, g.reshape(1, k), w)


def _linear_res_kernel(*refs, n_in, tn):
    res_ref = refs[0]
    a_refs = refs[1:1 + n_in]
    w_ref = refs[1 + n_in]
    out_ref = refs[2 + n_in]
    kc = a_refs[0].shape[1]
    for c in range(out_ref.shape[1] // tn):
        cols = slice(c * tn, (c + 1) * tn)
        acc = res_ref[:, cols]
        for r, a_ref in enumerate(a_refs):
            acc = acc + _dot(a_ref[...], w_ref[r * kc:(r + 1) * kc, cols])
        out_ref[:, cols] = acc


def linear_residual(res, a_list, w, *, tm, tn):
    t, n = res.shape
    tm = min(tm, t)
    n_in = len(a_list)
    kc = a_list[0].shape[1]
    in_specs = [pl.BlockSpec((tm, n), lambda i: (i, 0))]
    in_specs += [pl.BlockSpec((tm, kc), lambda i: (i, 0)) for _ in range(n_in)]
    in_specs += [pl.BlockSpec(w.shape, lambda i: (0, 0))]
    return pl.pallas_call(
        functools.partial(_linear_res_kernel, n_in=n_in, tn=tn),
        grid=(t // tm,),
        in_specs=in_specs,
        out_specs=pl.BlockSpec((tm, n), lambda i: (i, 0)),
        out_shape=jax.ShapeDtypeStruct((t, n), F32),
        compiler_params=_params(("parallel",)),
        name="linear_residual",
    )(res, *a_list, w)


def _norm_linear_f32_kernel(x_ref, g_ref, w_ref, *refs):
    out_refs, h_ref = refs[:-1], refs[-1]
    j = pl.program_id(1)

    @pl.when(j == 0)
    def _():
        h_ref[...] = _rms(x_ref[...], g_ref[...])

    if len(out_refs) == 1:
        out_refs[0][...] = _dot(h_ref[...], w_ref[...], True)
    else:
        for c, o_ref in enumerate(out_refs):
            @pl.when(j == c)
            def _(o_ref=o_ref):
                o_ref[...] = _dot(h_ref[...], w_ref[...], True)


def norm_linear_f32(x, g, w, *, tn, split):
    t, k = x.shape
    n = w.shape[1]
    nj = n // tn
    if split:
        out_shape = [jax.ShapeDtypeStruct((t, tn), F32) for _ in range(nj)]
        out_specs = [pl.BlockSpec((t, tn), lambda i, j: (i, 0)) for _ in range(nj)]
    else:
        out_shape = [jax.ShapeDtypeStruct((t, n), F32)]
        out_specs = [pl.BlockSpec((t, tn), lambda i, j: (i, j))]
    return pl.pallas_call(
        _norm_linear_f32_kernel,
        grid=(1, nj),
        in_specs=[
            pl.BlockSpec((t, k), lambda i, j: (i, 0)),
            pl.BlockSpec((1, k), lambda i, j: (0, 0)),
            pl.BlockSpec((k, tn), lambda i, j: (0, j)),
        ],
        out_specs=out_specs,
        out_shape=out_shape,
        scratch_shapes=[pltpu.VMEM((t, k), F32)],
        compiler_params=_params(("parallel", "arbitrary")),
        name="norm_linear_f32",
    )(x, g.reshape(1, k), w)


def _linear_res_f32_kernel(*refs, n_in):
    res_ref = refs[0]
    a_refs = refs[1:1 + n_in]
    w_refs = refs[1 + n_in:1 + 2 * n_in]
    out_ref = refs[1 + 2 * n_in]
    acc = res_ref[...]
    for a_ref, w_ref in zip(a_refs, w_refs):
        acc = acc + _dot(a_ref[...], w_ref[...], True)
    out_ref[...] = acc


def linear_residual_f32(res, a_list, w, *, tn):
    t, n = res.shape
    n_in = len(a_list)
    kc = a_list[0].shape[1]
    in_specs = [pl.BlockSpec((t, tn), lambda i, j: (i, j))]
    in_specs += [pl.BlockSpec((t, kc), lambda i, j: (i, 0)) for _ in range(n_in)]
    in_specs += [pl.BlockSpec((kc, tn), lambda i, j, c=c: (c, j)) for c in range(n_in)]
    return pl.pallas_call(
        functools.partial(_linear_res_f32_kernel, n_in=n_in),
        grid=(1, n // tn),
        in_specs=in_specs,
        out_specs=pl.BlockSpec((t, tn), lambda i, j: (i, j)),
        out_shape=jax.ShapeDtypeStruct((t, n), F32),
        compiler_params=_params(("parallel", "arbitrary")),
        name="linear_residual_f32",
    )(res, *a_list, *([w] * n_in))


def _lru_kernel(xb_ref, gate_ref, cbuf_ref, h0_ref, cw_ref, cb_ref, wa_ref, ba_ref, wx_ref, bx_ref, lam_ref,
                out_ref, hlast_ref, xpad, hcar, a_s, u_s, *, tc, width, precise):
    c = pl.program_id(1)
    nslab = width // MXU_DIM
    ngrp = tc // SUBLANES

    @pl.when(c == 0)
    def _():
        xpad[pl.ds(0, SUBLANES), :] = cbuf_ref[0]
        hcar[...] = h0_ref[0]

    xpad[pl.ds(SUBLANES, tc), :] = xb_ref[0]
    xc = cb_ref[...] + cw_ref[pl.ds(CONV_W - 1, 1), :] * xpad[pl.ds(SUBLANES, tc), :]
    for j in range(CONV_W - 1):
        xc = xc + cw_ref[pl.ds(j, 1), :] * xpad[pl.ds(SUBLANES - (CONV_W - 1) + j, tc), :]
    xpad[pl.ds(0, SUBLANES), :] = xpad[pl.ds(tc, SUBLANES), :]

    lam = lam_ref[...]
    softplus_neg = jnp.maximum(-lam, 0.0) + jnp.log1p(jnp.exp(-jnp.abs(lam)))
    c8 = -LRU_C * softplus_neg

    sub = lax.broadcasted_iota(jnp.int32, (ngrp, SUBLANES, MXU_DIM), 1)
    for s in range(nslab):
        cols = slice(s * MXU_DIM, (s + 1) * MXU_DIM)
        xs = xc[:, cols]
        r = _sigmoid(_dot(xs, wa_ref[s], precise) + ba_ref[:, cols])
        i = _sigmoid(_dot(xs, wx_ref[s], precise) + bx_ref[:, cols])
        a = jnp.exp(c8[:, cols] * r)
        u = jnp.sqrt(1.0 - a * a) * (i * xs)
        a3 = a.reshape(ngrp, SUBLANES, MXU_DIM)
        u3 = u.reshape(ngrp, SUBLANES, MXU_DIM)
        d = 1
        while d < SUBLANES:
            a_sh = pltpu.roll(a3, d, 1)
            u_sh = pltpu.roll(u3, d, 1)
            keep = sub >= d
            u3 = jnp.where(keep, u3 + a3 * u_sh, u3)
            a3 = jnp.where(keep, a3 * a_sh, a3)
            d *= 2
        a_s[:, cols] = a3.reshape(tc, MXU_DIM)
        u_s[:, cols] = u3.reshape(tc, MXU_DIM)

    def body(g, hin):
        rows = pl.ds(pl.multiple_of(g * SUBLANES, SUBLANES), SUBLANES)
        h = u_s[rows, :] + a_s[rows, :] * hin
        u_s[rows, :] = h
        return h[SUBLANES - 1:SUBLANES, :]

    hfin = lax.fori_loop(0, ngrp, body, hcar[...])
    hcar[...] = hfin
    out_ref[0] = (u_s[...] * _gelu_tanh(gate_ref[0])).astype(out_ref.dtype)
    hlast_ref[0] = hfin


def _block_diag(w, per):
    nb, k, _ = w.shape
    w4 = w.reshape(nb // per, per, k, k)
    eye = jnp.eye(per, dtype=w.dtype)
    return jnp.einsum("cipq,ij->cipjq", w4, eye).reshape(nb // per, per * k, per * k)


def lru_mixer(xb, gate, conv_buf, h0, conv_w, conv_b, wa, ba, wx, bx, lam, *, tc, precise):
    b, t, width = xb.shape
    tc = min(tc, t)
    per = MXU_DIM // LRU_BLOCK
    act_dt = F32 if precise else BF16
    wa_bd = _block_diag(wa, per).astype(act_dt)
    wx_bd = _block_diag(wx, per).astype(act_dt)
    nslab = wa_bd.shape[0]
    cbuf8 = jnp.concatenate([jnp.zeros((b, SUBLANES - (CONV_W - 1), width), F32), conv_buf.astype(F32)], axis=1)
    row = lambda v: v.reshape(1, width).astype(F32)
    vec_spec = pl.BlockSpec((1, width), lambda bi, ci: (0, 0))
    seq_spec = pl.BlockSpec((1, tc, width), lambda bi, ci: (bi, ci, 0))
    wspec = pl.BlockSpec((nslab, MXU_DIM, MXU_DIM), lambda bi, ci: (0, 0, 0))
    out, hlast = pl.pallas_call(
        functools.partial(_lru_kernel, tc=tc, width=width, precise=precise),
        grid=(b, t // tc),
        in_specs=[
            seq_spec, seq_spec,
            pl.BlockSpec((1, SUBLANES, width), lambda bi, ci: (bi, 0, 0)),
            pl.BlockSpec((1, 1, width), lambda bi, ci: (bi, 0, 0)),
            pl.BlockSpec((CONV_W, width), lambda bi, ci: (0, 0)),
            vec_spec, wspec, vec_spec, wspec, vec_spec, vec_spec,
        ],
        out_specs=[seq_spec, pl.BlockSpec((1, 1, width), lambda bi, ci: (bi, 0, 0))],
        out_shape=[jax.ShapeDtypeStruct((b, t, width), act_dt), jax.ShapeDtypeStruct((b, 1, width), F32)],
        scratch_shapes=[
            pltpu.VMEM((tc + SUBLANES, width), F32),
            pltpu.VMEM((1, width), F32),
            pltpu.VMEM((tc, width), F32),
            pltpu.VMEM((tc, width), F32),
        ],
        compiler_params=_params(("parallel", "arbitrary")),
        name="lru_mixer",
    )(xb, gate, cbuf8, h0.reshape(b, 1, width).astype(F32), conv_w.astype(F32), row(conv_b), wa_bd,
      row(ba), wx_bd, row(bx), row(lam))
    return out, hlast.reshape(b, width)


def _diff_lambda(lq1, lk1, lq2, lk2, lam_init):
    s1 = jnp.sum(lq1[...] * lk1[...], axis=-1, keepdims=True)
    s2 = jnp.sum(lq2[...] * lk2[...], axis=-1, keepdims=True)
    return jnp.exp(s1) - jnp.exp(s2) + lam_init


_LOG2E_PARTS = (1.4453125, -0.00262451171875, 7.063150405883789e-06)
LOG2E = sum(_LOG2E_PARTS)
N_BIAS_COLS = 2 * len(_LOG2E_PARTS)


def _attn_prompt_kernel(slopes_ref, q_ref, k_ref, v_ref, lq1, lk1, lq2, lk2, subln_ref, out_ref,
                        kaug, vt, qt, m_s, l_s, acc_s, *, tq, lam_init):
    h = pl.program_id(1)
    qi = pl.program_id(2)
    slope = slopes_ref[h]
    nblk, tk, _ = kaug.shape
    nparts = len(_LOG2E_PARTS)

    @pl.when(qi == 0)
    def _():
        pos = lax.broadcasted_iota(jnp.int32, (tk, LANES), 0)
        lane = lax.broadcasted_iota(jnp.int32, (tk, LANES), 1)
        within = (pos % CHUNK).astype(F32) * slope
        for j in range(nblk):
            rows = slice(j * tk, (j + 1) * tk)
            kaug[j, :, 0:LANES] = k_ref[0, rows, :].astype(BF16)
            coarse = ((pos + j * tk) // CHUNK * CHUNK).astype(F32) * slope
            cols = jnp.where(lane < nparts, coarse, jnp.where(lane < N_BIAS_COLS, within, 0.0))
            kaug[j, :, LANES:2 * LANES] = cols.astype(BF16)
            vt[j] = v_ref[0, rows, :].T.astype(BF16)
        r = lax.broadcasted_iota(jnp.int32, (LANES, tq), 0)
        part = jnp.where(r % nparts == 0, _LOG2E_PARTS[0],
                         jnp.where(r % nparts == 1, _LOG2E_PARTS[1], _LOG2E_PARTS[2]))
        const_rows = jnp.where(r < N_BIAS_COLS, part, 0.0).astype(BF16)
        qt[0, LANES:2 * LANES, :] = const_rows
        qt[1, LANES:2 * LANES, :] = const_rows

    qs = q_ref[0] * (LOG2E * DIFF_HD ** -0.5)
    qlane = lax.broadcasted_iota(jnp.int32, qs.shape, 1)
    qt[0, 0:LANES, :] = jnp.where(qlane < DIFF_HD, qs, 0.0).T.astype(BF16)
    qt[1, 0:LANES, :] = jnp.where(qlane >= DIFF_HD, qs, 0.0).T.astype(BF16)
    m_s[...] = jnp.full(m_s.shape, -jnp.inf, F32)
    l_s[...] = jnp.zeros(l_s.shape, F32)
    acc_s[...] = jnp.zeros(acc_s.shape, F32)

    def block(ki, diagonal):
        kblk = kaug[ki]
        vblk = vt[ki]
        if diagonal:
            kpos = lax.broadcasted_iota(jnp.int32, (tk, tq), 0)
            qpos = lax.broadcasted_iota(jnp.int32, (tk, tq), 1)
            ahead = (kpos - qpos).astype(F32)
            fix = jnp.where(kpos > qpos, (-2.0 * LOG2E) * slope * ahead, 0.0)
            allowed = (kpos // CHUNK) <= (qpos // CHUNK)
        for m in range(2):
            s = jnp.dot(kblk, qt[m], preferred_element_type=F32)
            if diagonal:
                s = jnp.where(allowed, s + fix, -jnp.inf)
            m_old = m_s[m]
            m_new = jnp.maximum(m_old, jnp.max(s, axis=0, keepdims=True))
            p = jnp.exp2(s - m_new)
            alpha = jnp.exp2(m_old - m_new)
            l_s[m] = alpha * l_s[m] + jnp.sum(p, axis=0, keepdims=True)
            acc_s[m] = alpha * acc_s[m] + jnp.dot(vblk, p.astype(BF16), preferred_element_type=F32)
            m_s[m] = m_new

    def body(ki, carry):
        block(ki, False)
        return carry

    lax.fori_loop(0, qi, body, 0)
    block(qi, True)

    lam = _diff_lambda(lq1, lk1, lq2, lk2, lam_init)
    o = acc_s[0] / l_s[0] - lam * (acc_s[1] / l_s[1])
    o = o * lax.rsqrt(jnp.mean(o * o, axis=0, keepdims=True) + EPS) * subln_ref[...] * (1.0 - lam_init)
    out_ref[0] = o.T.astype(out_ref.dtype)


def _alibi_slopes():
    return 2.0 ** (-8.0 * jnp.arange(1, DIFF_HEADS + 1, dtype=F32) / DIFF_HEADS)


def diff_attention_prompt(q, k, v, lam_params, subln_g, lam_init, *, tq):
    b, t, aw = q.shape
    tq = min(tq, t)
    hd2 = 2 * DIFF_HD
    assert hd2 == LANES and DIFF_VD == LANES and tq % CHUNK == 0
    assert t // CHUNK <= 256, "chunk index must stay exact in bf16"
    lrow = lambda p: p.reshape(1, DIFF_HD).astype(F32)
    lspec = pl.BlockSpec((1, DIFF_HD), lambda bi, hi, qi: (0, 0))
    kv_spec = pl.BlockSpec((1, t, hd2), lambda bi, hi, qi: (bi, 0, hi))
    q_spec = pl.BlockSpec((1, tq, hd2), lambda bi, hi, qi: (bi, qi, hi))
    return pl.pallas_call(
        functools.partial(_attn_prompt_kernel, tq=tq, lam_init=lam_init),
        grid=(b, DIFF_HEADS, t // tq),
        in_specs=[
            pl.BlockSpec(memory_space=pltpu.SMEM),
            q_spec, kv_spec, kv_spec, lspec, lspec, lspec, lspec,
            pl.BlockSpec((DIFF_VD, 1), lambda bi, hi, qi: (0, 0)),
        ],
        out_specs=q_spec,
        out_shape=jax.ShapeDtypeStruct((b, t, aw), BF16),
        scratch_shapes=[
            pltpu.VMEM((t // tq, tq, 2 * LANES), BF16),
            pltpu.VMEM((t // tq, DIFF_VD, tq), BF16),
            pltpu.VMEM((2, 2 * LANES, tq), BF16),
            pltpu.VMEM((2, 1, tq), F32),
            pltpu.VMEM((2, 1, tq), F32),
            pltpu.VMEM((2, DIFF_VD, tq), F32),
        ],
        compiler_params=_params(("parallel", "parallel", "arbitrary")),
        name="diff_attention_prompt",
    )(_alibi_slopes(), q, k, v, *[lrow(p) for p in lam_params], subln_g.reshape(DIFF_VD, 1).astype(F32))


def _attn_sample_kernel(slopes_ref, q_ref, kp_ref, vp_ref, kn_ref, vn_ref, lq1, lk1, lq2, lk2, subln_ref, out_ref,
                        *, past, tq, lam_init):
    h = pl.program_id(1)
    slope = slopes_ref[h]
    q = q_ref[0]
    kp, vp, kn, vn = kp_ref[0], vp_ref[0], kn_ref[0], vn_ref[0]

    def bias_mask(nk, k_off):
        qpos = past + lax.broadcasted_iota(jnp.int32, (tq, nk), 0)
        kpos = k_off + lax.broadcasted_iota(jnp.int32, (tq, nk), 1)
        bias = -slope * jnp.abs(qpos - kpos).astype(F32)
        allowed = (kpos // CHUNK) <= (qpos // CHUNK)
        return bias, allowed

    bias_p, ok_p = bias_mask(past, 0)
    bias_n, ok_n = bias_mask(tq, past)
    qlane = lax.broadcasted_iota(jnp.int32, q.shape, 1)
    probs = []
    for m in range(2):
        qm = jnp.where((qlane >= DIFF_HD) == (m == 1), q, 0.0)
        sp = jnp.where(ok_p, _dot_nt(qm, kp, True) * (DIFF_HD ** -0.5) + bias_p, -jnp.inf)
        sn = jnp.where(ok_n, _dot_nt(qm, kn, True) * (DIFF_HD ** -0.5) + bias_n, -jnp.inf)
        mx = jnp.maximum(jnp.max(sp, axis=-1, keepdims=True), jnp.max(sn, axis=-1, keepdims=True))
        pp = jnp.exp(sp - mx)
        pn = jnp.exp(sn - mx)
        l = jnp.sum(pp, axis=-1, keepdims=True) + jnp.sum(pn, axis=-1, keepdims=True)
        probs.append((pp / l, pn / l))
    lam = _diff_lambda(lq1, lk1, lq2, lk2, lam_init)
    o = _dot(probs[0][0] - lam * probs[1][0], vp, True) + _dot(probs[0][1] - lam * probs[1][1], vn, True)
    out_ref[0] = (o * lax.rsqrt(jnp.mean(o * o, axis=-1, keepdims=True) + EPS) * subln_ref[...]
                  * (1.0 - lam_init)).astype(out_ref.dtype)


def diff_attention_sample(q, k_new, v_new, past_k, past_v, lam_params, subln_g, lam_init):
    b, t, aw = q.shape
    past = past_k.shape[1]
    hd2 = 2 * DIFF_HD
    lrow = lambda p: p.reshape(1, DIFF_HD).astype(F32)
    lspec = pl.BlockSpec((1, DIFF_HD), lambda bi, hi: (0, 0))
    new_spec = pl.BlockSpec((1, t, hd2), lambda bi, hi: (bi, 0, hi))
    past_spec = pl.BlockSpec((1, past, hd2), lambda bi, hi: (bi, 0, hi))
    return pl.pallas_call(
        functools.partial(_attn_sample_kernel, past=past, tq=t, lam_init=lam_init),
        grid=(b, DIFF_HEADS),
        in_specs=[
            pl.BlockSpec(memory_space=pltpu.SMEM),
            new_spec, past_spec, past_spec, new_spec, new_spec, lspec, lspec, lspec, lspec,
            pl.BlockSpec((1, DIFF_VD), lambda bi, hi: (0, 0)),
        ],
        out_specs=new_spec,
        out_shape=jax.ShapeDtypeStruct((b, t, aw), F32),
        compiler_params=_params(("parallel", "parallel")),
        name="diff_attention_sample",
    )(_alibi_slopes(), q, past_k, past_v, k_new, v_new, *[lrow(p) for p in lam_params],
      subln_g.reshape(1, DIFF_VD).astype(F32))


def _cross_kernel(q_ref, mk_ref, mv_ref, out_ref, *, precise):
    d = q_ref.shape[-1]
    hd = d // X_HEADS
    for hh in range(X_HEADS):
        cols = slice(hh * hd, (hh + 1) * hd)
        s = _dot_nt(q_ref[0, :, cols], mk_ref[0, :, cols], precise) * (hd ** -0.5)
        p = jnp.exp(s - jnp.max(s, axis=-1, keepdims=True))
        p = p / jnp.sum(p, axis=-1, keepdims=True)
        out_ref[0, :, cols] = _dot(p, mv_ref[0, :, cols], precise).astype(out_ref.dtype)


def cross_attention_core(q, mk, mv, *, tq, precise):
    b, t, d = q.shape
    nm = mk.shape[1]
    tq = min(tq, t)
    q_spec = pl.BlockSpec((1, tq, d), lambda bi, qi: (bi, qi, 0))
    m_spec = pl.BlockSpec((1, nm, d), lambda bi, qi: (bi, 0, 0))
    return pl.pallas_call(
        functools.partial(_cross_kernel, precise=precise),
        grid=(b, t // tq),
        in_specs=[q_spec, m_spec, m_spec],
        out_specs=q_spec,
        out_shape=jax.ShapeDtypeStruct((b, t, d), F32 if precise else BF16),
        compiler_params=_params(("parallel", "arbitrary")),
        name="cross_attention_core",
    )(q, mk, mv)


def _first_argmax(vals, lane, valid):
    masked = jnp.where(valid, vals, -jnp.inf)
    mx = jnp.max(masked, axis=-1, keepdims=True)
    idx = jnp.min(jnp.where(masked == mx, lane, LANES), axis=-1, keepdims=True)
    return mx, idx


def _route(logits):
    lane = lax.broadcasted_iota(jnp.int32, logits.shape, 1).astype(F32)
    is_group = lane < N_GROUPS
    gmax, gidx = _first_argmax(logits, lane, is_group)
    gsum = jnp.sum(jnp.where(is_group, jnp.exp(logits - gmax), 0.0), axis=-1, keepdims=True)
    g_top = 1.0 / gsum
    lo = N_GROUPS + gidx * EXP_PER_GROUP
    in_group = (lane >= lo) & (lane < lo + EXP_PER_GROUP)
    e1, i1 = _first_argmax(logits, lane, in_group)
    e2, i2 = _first_argmax(logits, lane, in_group & (lane != i1))
    w2 = jnp.exp(e2 - e1)
    gate1 = g_top / (1.0 + w2)
    gate2 = g_top * w2 / (1.0 + w2)
    combine = jnp.where(lane == i1, gate1, 0.0) + jnp.where(lane == i2, gate2, 0.0)
    return pltpu.roll(combine, LANES - N_GROUPS, 1)


def _moe_dense_kernel(x_ref, g_ref, wr_ref, wg_ref, wu_ref, wd_ref, fg_ref, out_ref, h_s, comb_s, acc_s,
                      *, precise_router):
    e = pl.program_id(1)

    @pl.when(e == 0)
    def _():
        h = _rms(x_ref[...], g_ref[...])
        h_s[...] = h.astype(BF16)
        comb_s[...] = _route(_dot(h, wr_ref[...], precise_router))
        acc_s[...] = x_ref[...]

    hb = h_s[...]
    act = jnp.dot(hb, wg_ref[0], preferred_element_type=F32)
    act = act * _sigmoid(act) * jnp.dot(hb, wu_ref[0], preferred_element_type=F32)
    y = jnp.dot(act.astype(BF16), wd_ref[0], preferred_element_type=F32)
    lane = lax.broadcasted_iota(jnp.int32, comb_s.shape, 1)
    ce = jnp.sum(jnp.where(lane == e, comb_s[...], 0.0), axis=-1, keepdims=True)
    acc_s[...] += ce * y

    @pl.when(e == pl.num_programs(1) - 1)
    def _():
        out_ref[...] = _rms(acc_s[...], fg_ref[...])


def moe_dense_final(x, g, w_router_pad, wg, wu, wd, final_g, *, tm, precise_router):
    t, d = x.shape
    tm = min(tm, t)
    ne, _, ff = wg.shape
    return pl.pallas_call(
        functools.partial(_moe_dense_kernel, precise_router=precise_router),
        grid=(t // tm, ne),
        in_specs=[
            pl.BlockSpec((tm, d), lambda i, e: (i, 0)),
            pl.BlockSpec((1, d), lambda i, e: (0, 0)),
            pl.BlockSpec((d, LANES), lambda i, e: (0, 0)),
            pl.BlockSpec((1, d, ff), lambda i, e: (e, 0, 0)),
            pl.BlockSpec((1, d, ff), lambda i, e: (e, 0, 0)),
            pl.BlockSpec((1, ff, d), lambda i, e: (e, 0, 0)),
            pl.BlockSpec((1, d), lambda i, e: (0, 0)),
        ],
        out_specs=pl.BlockSpec((tm, d), lambda i, e: (i, 0)),
        out_shape=jax.ShapeDtypeStruct((t, d), F32),
        scratch_shapes=[pltpu.VMEM((tm, d), BF16), pltpu.VMEM((tm, LANES), F32), pltpu.VMEM((tm, d), F32)],
        compiler_params=_params(("parallel", "arbitrary")),
        name="moe_dense_final",
    )(x, g.reshape(1, d), w_router_pad, wg, wu, wd, final_g.reshape(1, d))


def _trunk(x, mem_k, mem_v, conv_buf, h0, past_k, past_v, p, lam_init):
    b, t, d = x.shape
    n = b * t
    xf = x.reshape(n, d)
    aw = DIFF_HEADS * DIFF_VD
    lru_w = p["lru_lambda"].shape[0]
    precise = past_k is not None
    tm = n if precise else 512
    tn = 1024
    assert t >= CONV_W - 1
    seq = lambda a: a.reshape(b, t, a.shape[-1])
    lam_params = (p["lam_q1"], p["lam_k1"], p["lam_q2"], p["lam_k2"])
    lru_args = (p["conv_w"], p["conv_b"], p["lru_wa"], p["lru_ba"].reshape(-1), p["lru_wx"],
                p["lru_bx"].reshape(-1), p["lru_lambda"])

    if precise:
        xb, gate, q, k, v = norm_linear_f32(xf, p["norm_mix_g"], p["w_in_f32"], tn=tn, split=True)
    else:
        xb, gate, q, k, v = norm_linear(xf, p["norm_mix_g"], p["w_in"], tm=tm, tn=tn,
                                        out_widths=[lru_w, lru_w, aw, aw, aw], out_dtypes=[F32] * 5)
    lru_out, h_last = lru_mixer(seq(xb), seq(gate), conv_buf, h0, *lru_args, tc=256, precise=precise)
    if precise:
        att = diff_attention_sample(seq(q), seq(k), seq(v), past_k, past_v, lam_params, p["subln_g"], lam_init)
    else:
        att = diff_attention_prompt(seq(q), seq(k), seq(v), lam_params, p["subln_g"], lam_init, tq=512)
    mix_in = [lru_out.reshape(n, lru_w), att.reshape(n, aw)]
    if precise:
        x1 = linear_residual_f32(xf, mix_in, p["w_out_f32"], tn=tn)
        (qx,) = norm_linear_f32(x1, p["norm_cross_g"], p["xq_w_f32"], tn=tn, split=False)
    else:
        x1 = linear_residual(xf, mix_in, p["w_out"], tm=tm, tn=tn)
        (qx,) = norm_linear(x1, p["norm_cross_g"], p["xq_w"], tm=tm, tn=tn, out_widths=[d], out_dtypes=[BF16])
    o = cross_attention_core(seq(qx), mem_k, mem_v, tq=512, precise=precise)
    if precise:
        x2 = linear_residual_f32(x1, [o.reshape(n, d)], p["xo_w_f32"], tn=tn)
    else:
        x2 = linear_residual(x1, [o.reshape(n, d)], p["xo_w"], tm=tm, tn=tn)

    y = moe_dense_final(x2, p["norm_ffn_g"], p["router_pad_f32"] if precise else p["router_pad"], p["exp_gate"],
                        p["exp_up"], p["exp_down"], p["final_norm_g"], tm=tm, precise_router=precise)
    new_conv = seq(xb)[:, t - (CONV_W - 1):, :]
    return y.reshape(b, t, d), new_conv, h_last, k, v


def kernel(x_prompt, x_sample, cache_diff_k, cache_diff_v, cache_mem_k, cache_mem_v, state_conv, state_lru, mem_prompt, norm_mix_g, w_in, conv_w, conv_b, lru_wa, lru_ba, lru_wx, lru_bx, lru_lambda, lam_q1, lam_k1, lam_q2, lam_k2, subln_g, w_out, norm_cross_g, norm_mem_g, xq_w, xk_w, xv_w, xo_w, norm_ffn_g, router_group_w, router_expert_w, exp_gate, exp_up, exp_down, final_norm_g):
    depth = w_in.shape[0]
    assert depth == 1, "single-layer step"
    bp, tp, d = x_prompt.shape
    bs, ts, _ = x_sample.shape
    past = cache_diff_k.shape[2]
    n_mem = mem_prompt.shape[1]
    aw = DIFF_HEADS * DIFF_VD
    l = 0
    lam_init = 0.8 - 0.6 * math.exp(-0.3 * l)

    router = jnp.concatenate([router_group_w[l], router_expert_w[l]], axis=1)
    router_pad_f32 = jnp.pad(router, ((0, 0), (0, LANES - router.shape[1])))
    router_pad = router_pad_f32.astype(BF16)
    p = dict(router_pad_f32=router_pad_f32, w_in_f32=w_in[l], w_out_f32=w_out[l], xq_w_f32=xq_w[l],
             xo_w_f32=xo_w[l], **dict(norm_mix_g=norm_mix_g[l], conv_w=conv_w[l], conv_b=conv_b[l], lru_wa=lru_wa[l],
             lru_ba=lru_ba[l], lru_wx=lru_wx[l], lru_bx=lru_bx[l], lru_lambda=lru_lambda[l], lam_q1=lam_q1[l],
             lam_k1=lam_k1[l], lam_q2=lam_q2[l], lam_k2=lam_k2[l], subln_g=subln_g[l],
             norm_cross_g=norm_cross_g[l], norm_ffn_g=norm_ffn_g[l], router_pad=router_pad,
             final_norm_g=final_norm_g))
    for name, w in (("w_in", w_in), ("w_out", w_out), ("xq_w", xq_w), ("xo_w", xo_w), ("exp_gate", exp_gate),
                    ("exp_up", exp_up), ("exp_down", exp_down)):
        p[name] = w[l].astype(BF16)

    memf = mem_prompt.reshape(bp * n_mem, d)
    w_mem = jnp.concatenate([xk_w[l].astype(BF16), xv_w[l].astype(BF16)], axis=1)
    mk_p, mv_p = norm_linear(memf, norm_mem_g[l], w_mem, tm=512, tn=1024, out_widths=[d, d], out_dtypes=[F32, F32])
    mk_p = mk_p.reshape(bp, n_mem, d)
    mv_p = mv_p.reshape(bp, n_mem, d)

    zero_buf = jnp.zeros((bp, CONV_W - 1, lru_lambda.shape[1]), F32)
    zero_h = jnp.zeros((bp, lru_lambda.shape[1]), F32)
    y_p, cb_p, hl_p, k_p, v_p = _trunk(x_prompt, mk_p, mv_p, zero_buf, zero_h, None, None, p, lam_init)
    y_s, cb_s, hl_s, k_s, v_s = _trunk(x_sample, cache_mem_k[l].reshape(bs, n_mem, d),
                                       cache_mem_v[l].reshape(bs, n_mem, d), state_conv[l], state_lru[l],
                                       cache_diff_k[l].reshape(bs, past, aw), cache_diff_v[l].reshape(bs, past, aw),
                                       p, lam_init)

    hd2 = 2 * DIFF_HD
    return (y_p, y_s,
            k_p.reshape(1, bp, tp, DIFF_HEADS, hd2), v_p.reshape(1, bp, tp, DIFF_HEADS, DIFF_VD),
            mk_p.reshape(1, bp, n_mem, X_HEADS, d // X_HEADS), mv_p.reshape(1, bp, n_mem, X_HEADS, d // X_HEADS),
            cb_p[None], hl_p[None],
            k_s.reshape(1, bs, ts, DIFF_HEADS, hd2), v_s.reshape(1, bs, ts, DIFF_HEADS, DIFF_VD),
            cb_s[None], hl_s[None].astype(state_lru.dtype))
```

```python
import functools
import math

import jax
import jax.numpy as jnp
from jax import lax
from jax.experimental import pallas as pl
from jax.experimental.pallas import tpu as pltpu

F32 = jnp.float32
BF16 = jnp.bfloat16
HIGHEST = lax.Precision.HIGHEST

CHUNK = 64
CONV_W = 4
LRU_C = 8.0
LRU_BLOCK = 64
DIFF_HEADS = 8
DIFF_HD = 64
DIFF_VD = 2 * DIFF_HD
X_HEADS = 4
N_GROUPS = 4
EXP_PER_GROUP = 4
N_EXPERTS = N_GROUPS * EXP_PER_GROUP
EPS = 1e-6

LANES = 128
SUBLANES = 8
MXU_DIM = 256
VMEM_LIMIT_BYTES = 56 * 1024 * 1024


def _params(semantics):
    return pltpu.CompilerParams(dimension_semantics=semantics, vmem_limit_bytes=VMEM_LIMIT_BYTES)


def _dot(a, b, precise=False):
    if precise:
        return jnp.dot(a.astype(F32), b.astype(F32), precision=HIGHEST, preferred_element_type=F32)
    return jnp.dot(a.astype(BF16), b.astype(BF16), preferred_element_type=F32)


def _dot_nt(a, b, precise=False):
    dims = (((1,), (1,)), ((), ()))
    if precise:
        return lax.dot_general(a.astype(F32), b.astype(F32), dims, precision=HIGHEST, preferred_element_type=F32)
    return lax.dot_general(a.astype(BF16), b.astype(BF16), dims, preferred_element_type=F32)


def _rms(x, g):
    return x * lax.rsqrt(jnp.mean(x * x, axis=-1, keepdims=True) + EPS) * g


def _sigmoid(x):
    return 1.0 / (1.0 + jnp.exp(-x))


def _gelu_tanh(x):
    c = math.sqrt(2.0 / math.pi)
    return 0.5 * x * (1.0 + jnp.tanh(c * (x + 0.044715 * (x * x * x))))


def _norm_linear_kernel(x_ref, g_ref, w_ref, *out_refs, tn):
    h = _rms(x_ref[...], g_ref[...]).astype(BF16)
    col = 0
    for o_ref in out_refs:
        for c in range(o_ref.shape[1] // tn):
            o_ref[:, c * tn:(c + 1) * tn] = jnp.dot(
                h, w_ref[:, col:col + tn], preferred_element_type=F32).astype(o_ref.dtype)
            col += tn


def norm_linear(x, g, w, *, tm, tn, out_widths, out_dtypes):
    t, k = x.shape
    n = w.shape[1]
    tm = min(tm, t)
    assert sum(out_widths) == n and all(wd % tn == 0 for wd in out_widths)
    return pl.pallas_call(
        functools.partial(_norm_linear_kernel, tn=tn),
        grid=(t // tm,),
        in_specs=[
            pl.BlockSpec((tm, k), lambda i: (i, 0)),
            pl.BlockSpec((1, k), lambda i: (0, 0)),
            pl.BlockSpec((k, n), lambda i: (0, 0)),
        ],
        out_specs=[pl.BlockSpec((tm, wd), lambda i: (i, 0)) for wd in out_widths],
        out_shape=[jax.ShapeDtypeStruct((t, wd), dt) for wd, dt in zip(out_widths, out_dtypes)],
        compiler_params=_params(("parallel",)),
        name="norm_linear",
    )(x, g.reshape(1, k), w)


def _linear_res_kernel(*refs, n_in, tn):
    res_ref = refs[0]
    a_refs = refs[1:1 + n_in]
    w_ref = refs[1 + n_in]
    out_ref = refs[2 + n_in]
    kc = a_refs[0].shape[1]
    for c in range(out_ref.shape[1] // tn):
        cols = slice(c * tn, (c + 1) * tn)
        acc = res_ref[:, cols]
        for r, a_ref in enumerate(a_refs):
            acc = acc + _dot(a_ref[...], w_ref[r * kc:(r + 1) * kc, cols])
        out_ref[:, cols] = acc


def linear_residual(res, a_list, w, *, tm, tn):
    t, n = res.shape
    tm = min(tm, t)
    n_in = len(a_list)
    kc = a_list[0].shape[1]
    in_specs = [pl.BlockSpec((tm, n), lambda i: (i, 0))]
    in_specs += [pl.BlockSpec((tm, kc), lambda i: (i, 0)) for _ in range(n_in)]
    in_specs += [pl.BlockSpec(w.shape, lambda i: (0, 0))]
    return pl.pallas_call(
        functools.partial(_linear_res_kernel, n_in=n_in, tn=tn),
        grid=(t // tm,),
        in_specs=in_specs,
        out_specs=pl.BlockSpec((tm, n), lambda i: (i, 0)),
        out_shape=jax.ShapeDtypeStruct((t, n), F32),
        compiler_params=_params(("parallel",)),
        name="linear_residual",
    )(res, *a_list, w)


def _norm_linear_f32_kernel(x_ref, g_ref, w_ref, *refs):
    out_refs, h_ref = refs[:-1], refs[-1]
    j = pl.program_id(1)

    @pl.when(j == 0)
    def _():
        h_ref[...] = _rms(x_ref[...], g_ref[...])

    if len(out_refs) == 1:
        out_refs[0][...] = _dot(h_ref[...], w_ref[...], True)
    else:
        for c, o_ref in enumerate(out_refs):
            @pl.when(j == c)
            def _(o_ref=o_ref):
                o_ref[...] = _dot(h_ref[...], w_ref[...], True)


def norm_linear_f32(x, g, w, *, tn, split):
    t, k = x.shape
    n = w.shape[1]
    nj = n // tn
    if split:
        out_shape = [jax.ShapeDtypeStruct((t, tn), F32) for _ in range(nj)]
        out_specs = [pl.BlockSpec((t, tn), lambda i, j: (i, 0)) for _ in range(nj)]
    else:
        out_shape = [jax.ShapeDtypeStruct((t, n), F32)]
        out_specs = [pl.BlockSpec((t, tn), lambda i, j: (i, j))]
    return pl.pallas_call(
        _norm_linear_f32_kernel,
        grid=(1, nj),
        in_specs=[
            pl.BlockSpec((t, k), lambda i, j: (i, 0)),
            pl.BlockSpec((1, k), lambda i, j: (0, 0)),
            pl.BlockSpec((k, tn), lambda i, j: (0, j)),
        ],
        out_specs=out_specs,
        out_shape=out_shape,
        scratch_shapes=[pltpu.VMEM((t, k), F32)],
        compiler_params=_params(("parallel", "arbitrary")),
        name="norm_linear_f32",
    )(x, g.reshape(1, k), w)


def _linear_res_f32_kernel(*refs, n_in):
    res_ref = refs[0]
    a_refs = refs[1:1 + n_in]
    w_refs = refs[1 + n_in:1 + 2 * n_in]
    out_ref = refs[1 + 2 * n_in]
    acc = res_ref[...]
    for a_ref, w_ref in zip(a_refs, w_refs):
        acc = acc + _dot(a_ref[...], w_ref[...], True)
    out_ref[...] = acc


def linear_residual_f32(res, a_list, w, *, tn):
    t, n = res.shape
    n_in = len(a_list)
    kc = a_list[0].shape[1]
    in_specs = [pl.BlockSpec((t, tn), lambda i, j: (i, j))]
    in_specs += [pl.BlockSpec((t, kc), lambda i, j: (i, 0)) for _ in range(n_in)]
    in_specs += [pl.BlockSpec((kc, tn), lambda i, j, c=c: (c, j)) for c in range(n_in)]
    return pl.pallas_call(
        functools.partial(_linear_res_f32_kernel, n_in=n_in),
        grid=(1, n // tn),
        in_specs=in_specs,
        out_specs=pl.BlockSpec((t, tn), lambda i, j: (i, j)),
        out_shape=jax.ShapeDtypeStruct((t, n), F32),
        compiler_params=_params(("parallel", "arbitrary")),
        name="linear_residual_f32",
    )(res, *a_list, *([w] * n_in))


def _lru_kernel(xb_ref, gate_ref, cbuf_ref, h0_ref, cw_ref, cb_ref, wa_ref, ba_ref, wx_ref, bx_ref, lam_ref,
                out_ref, hlast_ref, xpad, hcar, a_s, u_s, *, tc, width, precise):
    c = pl.program_id(1)
    nslab = width // MXU_DIM
    ngrp = tc // SUBLANES

    @pl.when(c == 0)
    def _():
        xpad[pl.ds(0, SUBLANES), :] = cbuf_ref[0]
        hcar[...] = h0_ref[0]

    xpad[pl.ds(SUBLANES, tc), :] = xb_ref[0]
    xc = cb_ref[...] + cw_ref[pl.ds(CONV_W - 1, 1), :] * xpad[pl.ds(SUBLANES, tc), :]
    for j in range(CONV_W - 1):
        xc = xc + cw_ref[pl.ds(j, 1), :] * xpad[pl.ds(SUBLANES - (CONV_W - 1) + j, tc), :]
    xpad[pl.ds(0, SUBLANES), :] = xpad[pl.ds(tc, SUBLANES), :]

    lam = lam_ref[...]
    softplus_neg = jnp.maximum(-lam, 0.0) + jnp.log1p(jnp.exp(-jnp.abs(lam)))
    c8 = -LRU_C * softplus_neg

    sub = lax.broadcasted_iota(jnp.int32, (ngrp, SUBLANES, MXU_DIM), 1)
    for s in range(nslab):
        cols = slice(s * MXU_DIM, (s + 1) * MXU_DIM)
        xs = xc[:, cols]
        r = _sigmoid(_dot(xs, wa_ref[s], precise) + ba_ref[:, cols])
        i = _sigmoid(_dot(xs, wx_ref[s], precise) + bx_ref[:, cols])
        a = jnp.exp(c8[:, cols] * r)
        u = jnp.sqrt(1.0 - a * a) * (i * xs)
        a3 = a.reshape(ngrp, SUBLANES, MXU_DIM)
        u3 = u.reshape(ngrp, SUBLANES, MXU_DIM)
        d = 1
        while d < SUBLANES:
            a_sh = pltpu.roll(a3, d, 1)
            u_sh = pltpu.roll(u3, d, 1)
            keep = sub >= d
            u3 = jnp.where(keep, u3 + a3 * u_sh, u3)
            a3 = jnp.where(keep, a3 * a_sh, a3)
            d *= 2
        a_s[:, cols] = a3.reshape(tc, MXU_DIM)
        u_s[:, cols] = u3.reshape(tc, MXU_DIM)

    def body(g, hin):
        rows = pl.ds(pl.multiple_of(g * SUBLANES, SUBLANES), SUBLANES)
        h = u_s[rows, :] + a_s[rows, :] * hin
        u_s[rows, :] = h
        return h[SUBLANES - 1:SUBLANES, :]

    hfin = lax.fori_loop(0, ngrp, body, hcar[...])
    hcar[...] = hfin
    out_ref[0] = (u_s[...] * _gelu_tanh(gate_ref[0])).astype(out_ref.dtype)
    hlast_ref[0] = hfin


def _block_diag(w, per):
    nb, k, _ = w.shape
    w4 = w.reshape(nb // per, per, k, k)
    eye = jnp.eye(per, dtype=w.dtype)
    return jnp.einsum("cipq,ij->cipjq", w4, eye).reshape(nb // per, per * k, per * k)


def lru_mixer(xb, gate, conv_buf, h0, conv_w, conv_b, wa, ba, wx, bx, lam, *, tc, precise):
    b, t, width = xb.shape
    tc = min(tc, t)
    per = MXU_DIM // LRU_BLOCK
    act_dt = F32 if precise else BF16
    wa_bd = _block_diag(wa, per).astype(act_dt)
    wx_bd = _block_diag(wx, per).astype(act_dt)
    nslab = wa_bd.shape[0]
    cbuf8 = jnp.concatenate([jnp.zeros((b, SUBLANES - (CONV_W - 1), width), F32), conv_buf.astype(F32)], axis=1)
    row = lambda v: v.reshape(1, width).astype(F32)
    vec_spec = pl.BlockSpec((1, width), lambda bi, ci: (0, 0))
    seq_spec = pl.BlockSpec((1, tc, width), lambda bi, ci: (bi, ci, 0))
    wspec = pl.BlockSpec((nslab, MXU_DIM, MXU_DIM), lambda bi, ci: (0, 0, 0))
    out, hlast = pl.pallas_call(
        functools.partial(_lru_kernel, tc=tc, width=width, precise=precise),
        grid=(b, t // tc),
        in_specs=[
            seq_spec, seq_spec,
            pl.BlockSpec((1, SUBLANES, width), lambda bi, ci: (bi, 0, 0)),
            pl.BlockSpec((1, 1, width), lambda bi, ci: (bi, 0, 0)),
            pl.BlockSpec((CONV_W, width), lambda bi, ci: (0, 0)),
            vec_spec, wspec, vec_spec, wspec, vec_spec, vec_spec,
        ],
        out_specs=[seq_spec, pl.BlockSpec((1, 1, width), lambda bi, ci: (bi, 0, 0))],
        out_shape=[jax.ShapeDtypeStruct((b, t, width), act_dt), jax.ShapeDtypeStruct((b, 1, width), F32)],
        scratch_shapes=[
            pltpu.VMEM((tc + SUBLANES, width), F32),
            pltpu.VMEM((1, width), F32),
            pltpu.VMEM((tc, width), F32),
            pltpu.VMEM((tc, width), F32),
        ],
        compiler_params=_params(("parallel", "arbitrary")),
        name="lru_mixer",
    )(xb, gate, cbuf8, h0.reshape(b, 1, width).astype(F32), conv_w.astype(F32), row(conv_b), wa_bd,
      row(ba), wx_bd, row(bx), row(lam))
    return out, hlast.reshape(b, width)


def _diff_lambda(lq1, lk1, lq2, lk2, lam_init):
    s1 = jnp.sum(lq1[...] * lk1[...], axis=-1, keepdims=True)
    s2 = jnp.sum(lq2[...] * lk2[...], axis=-1, keepdims=True)
    return jnp.exp(s1) - jnp.exp(s2) + lam_init


_LOG2E_PARTS = (1.4453125, -0.00262451171875, 7.063150405883789e-06)
LOG2E = sum(_LOG2E_PARTS)
N_BIAS_COLS = 2 * len(_LOG2E_PARTS)


def _attn_prompt_kernel(slopes_ref, q_ref, k_ref, v_ref, lq1, lk1, lq2, lk2, subln_ref, out_ref,
                        kaug, vt, qt, m_s, l_s, acc_s, *, tq, lam_init):
    h = pl.program_id(1)
    qi = pl.program_id(2)
    slope = slopes_ref[h]
    nblk, tk, _ = kaug.shape
    nparts = len(_LOG2E_PARTS)

    @pl.when(qi == 0)
    def _():
        pos = lax.broadcasted_iota(jnp.int32, (tk, LANES), 0)
        lane = lax.broadcasted_iota(jnp.int32, (tk, LANES), 1)
        within = (pos % CHUNK).astype(F32) * slope
        for j in range(nblk):
            rows = slice(j * tk, (j + 1) * tk)
            kaug[j, :, 0:LANES] = k_ref[0, rows, :].astype(BF16)
            coarse = ((pos + j * tk) // CHUNK * CHUNK).astype(F32) * slope
            cols = jnp.where(lane < nparts, coarse, jnp.where(lane < N_BIAS_COLS, within, 0.0))
            kaug[j, :, LANES:2 * LANES] = cols.astype(BF16)
            vt[j] = v_ref[0, rows, :].T.astype(BF16)
        r = lax.broadcasted_iota(jnp.int32, (LANES, tq), 0)
        part = jnp.where(r % nparts == 0, _LOG2E_PARTS[0],
                         jnp.where(r % nparts == 1, _LOG2E_PARTS[1], _LOG2E_PARTS[2]))
        const_rows = jnp.where(r < N_BIAS_COLS, part, 0.0).astype(BF16)
        qt[0, LANES:2 * LANES, :] = const_rows
        qt[1, LANES:2 * LANES, :] = const_rows

    qs = q_ref[0] * (LOG2E * DIFF_HD ** -0.5)
    qlane = lax.broadcasted_iota(jnp.int32, qs.shape, 1)
    qt[0, 0:LANES, :] = jnp.where(qlane < DIFF_HD, qs, 0.0).T.astype(BF16)
    qt[1, 0:LANES, :] = jnp.where(qlane >= DIFF_HD, qs, 0.0).T.astype(BF16)
    m_s[...] = jnp.full(m_s.shape, -jnp.inf, F32)
    l_s[...] = jnp.zeros(l_s.shape, F32)
    acc_s[...] = jnp.zeros(acc_s.shape, F32)

    def block(ki, diagonal):
        kblk = kaug[ki]
        vblk = vt[ki]
        if diagonal:
            kpos = lax.broadcasted_iota(jnp.int32, (tk, tq), 0)
            qpos = lax.broadcasted_iota(jnp.int32, (tk, tq), 1)
            ahead = (kpos - qpos).astype(F32)
            fix = jnp.where(kpos > qpos, (-2.0 * LOG2E) * slope * ahead, 0.0)
            allowed = (kpos // CHUNK) <= (qpos // CHUNK)
        for m in range(2):
            s = jnp.dot(kblk, qt[m], preferred_element_type=F32)
            if diagonal:
                s = jnp.where(allowed, s + fix, -jnp.inf)
            m_old = m_s[m]
            m_new = jnp.maximum(m_old, jnp.max(s, axis=0, keepdims=True))
            p = jnp.exp2(s - m_new)
            alpha = jnp.exp2(m_old - m_new)
            l_s[m] = alpha * l_s[m] + jnp.sum(p, axis=0, keepdims=True)
            acc_s[m] = alpha * acc_s[m] + jnp.dot(vblk, p.astype(BF16), preferred_element_type=F32)
            m_s[m] = m_new

    def body(ki, carry):
        block(ki, False)
        return carry

    lax.fori_loop(0, qi, body, 0)
    block(qi, True)

    lam = _diff_lambda(lq1, lk1, lq2, lk2, lam_init)
    o = acc_s[0] / l_s[0] - lam * (acc_s[1] / l_s[1])
    o = o * lax.rsqrt(jnp.mean(o * o, axis=0, keepdims=True) + EPS) * subln_ref[...] * (1.0 - lam_init)
    out_ref[0] = o.T.astype(out_ref.dtype)


def _alibi_slopes():
    return 2.0 ** (-8.0 * jnp.arange(1, DIFF_HEADS + 1, dtype=F32) / DIFF_HEADS)


def diff_attention_prompt(q, k, v, lam_params, subln_g, lam_init, *, tq):
    b, t, aw = q.shape
    tq = min(tq, t)
    hd2 = 2 * DIFF_HD
    assert hd2 == LANES and DIFF_VD == LANES and tq % CHUNK == 0
    assert t // CHUNK <= 256, "chunk index must stay exact in bf16"
    lrow = lambda p: p.reshape(1, DIFF_HD).astype(F32)
    lspec = pl.BlockSpec((1, DIFF_HD), lambda bi, hi, qi: (0, 0))
    kv_spec = pl.BlockSpec((1, t, hd2), lambda bi, hi, qi: (bi, 0, hi))
    q_spec = pl.BlockSpec((1, tq, hd2), lambda bi, hi, qi: (bi, qi, hi))
    return pl.pallas_call(
        functools.partial(_attn_prompt_kernel, tq=tq, lam_init=lam_init),
        grid=(b, DIFF_HEADS, t // tq),
        in_specs=[
            pl.BlockSpec(memory_space=pltpu.SMEM),
            q_spec, kv_spec, kv_spec, lspec, lspec, lspec, lspec,
            pl.BlockSpec((DIFF_VD, 1), lambda bi, hi, qi: (0, 0)),
        ],
        out_specs=q_spec,
        out_shape=jax.ShapeDtypeStruct((b, t, aw), BF16),
        scratch_shapes=[
            pltpu.VMEM((t // tq, tq, 2 * LANES), BF16),
            pltpu.VMEM((t // tq, DIFF_VD, tq), BF16),
            pltpu.VMEM((2, 2 * LANES, tq), BF16),
            pltpu.VMEM((2, 1, tq), F32),
            pltpu.VMEM((2, 1, tq), F32),
            pltpu.VMEM((2, DIFF_VD, tq), F32),
        ],
        compiler_params=_params(("parallel", "parallel", "arbitrary")),
        name="diff_attention_prompt",
    )(_alibi_slopes(), q, k, v, *[lrow(p) for p in lam_params], subln_g.reshape(DIFF_VD, 1).astype(F32))


def _attn_sample_kernel(slopes_ref, q_ref, kp_ref, vp_ref, kn_ref, vn_ref, lq1, lk1, lq2, lk2, subln_ref, out_ref,
                        *, past, tq, lam_init):
    h = pl.program_id(1)
    slope = slopes_ref[h]
    q = q_ref[0]
    kp, vp, kn, vn = kp_ref[0], vp_ref[0], kn_ref[0], vn_ref[0]

    def bias_mask(nk, k_off):
        qpos = past + lax.broadcasted_iota(jnp.int32, (tq, nk), 0)
        kpos = k_off + lax.broadcasted_iota(jnp.int32, (tq, nk), 1)
        bias = -slope * jnp.abs(qpos - kpos).astype(F32)
        allowed = (kpos // CHUNK) <= (qpos // CHUNK)
        return bias, allowed

    bias_p, ok_p = bias_mask(past, 0)
    bias_n, ok_n = bias_mask(tq, past)
    qlane = lax.broadcasted_iota(jnp.int32, q.shape, 1)
    probs = []
    for m in range(2):
        qm = jnp.where((qlane >= DIFF_HD) == (m == 1), q, 0.0)
        sp = jnp.where(ok_p, _dot_nt(qm, kp, True) * (DIFF_HD ** -0.5) + bias_p, -jnp.inf)
        sn = jnp.where(ok_n, _dot_nt(qm, kn, True) * (DIFF_HD ** -0.5) + bias_n, -jnp.inf)
        mx = jnp.maximum(jnp.max(sp, axis=-1, keepdims=True), jnp.max(sn, axis=-1, keepdims=True))
        pp = jnp.exp(sp - mx)
        pn = jnp.exp(sn - mx)
        l = jnp.sum(pp, axis=-1, keepdims=True) + jnp.sum(pn, axis=-1, keepdims=True)
        probs.append((pp / l, pn / l))
    lam = _diff_lambda(lq1, lk1, lq2, lk2, lam_init)
    o = _dot(probs[0][0] - lam * probs[1][0], vp, True) + _dot(probs[0][1] - lam * probs[1][1], vn, True)
    out_ref[0] = (o * lax.rsqrt(jnp.mean(o * o, axis=-1, keepdims=True) + EPS) * subln_ref[...]
                  * (1.0 - lam_init)).astype(out_ref.dtype)


def diff_attention_sample(q, k_new, v_new, past_k, past_v, lam_params, subln_g, lam_init):
    b, t, aw = q.shape
    past = past_k.shape[1]
    hd2 = 2 * DIFF_HD
    lrow = lambda p: p.reshape(1, DIFF_HD).astype(F32)
    lspec = pl.BlockSpec((1, DIFF_HD), lambda bi, hi: (0, 0))
    new_spec = pl.BlockSpec((1, t, hd2), lambda bi, hi: (bi, 0, hi))
    past_spec = pl.BlockSpec((1, past, hd2), lambda bi, hi: (bi, 0, hi))
    return pl.pallas_call(
        functools.partial(_attn_sample_kernel, past=past, tq=t, lam_init=lam_init),
        grid=(b, DIFF_HEADS),
        in_specs=[
            pl.BlockSpec(memory_space=pltpu.SMEM),
            new_spec, past_spec, past_spec, new_spec, new_spec, lspec, lspec, lspec, lspec,
            pl.BlockSpec((1, DIFF_VD), lambda bi, hi: (0, 0)),
        ],
        out_specs=new_spec,
        out_shape=jax.ShapeDtypeStruct((b, t, aw), F32),
        compiler_params=_params(("parallel", "parallel")),
        name="diff_attention_sample",
    )(_alibi_slopes(), q, past_k, past_v, k_new, v_new, *[lrow(p) for p in lam_params],
      subln_g.reshape(1, DIFF_VD).astype(F32))


def _cross_kernel(q_ref, mk_ref, mv_ref, out_ref, *, precise):
    d = q_ref.shape[-1]
    hd = d // X_HEADS
    for hh in range(X_HEADS):
        cols = slice(hh * hd, (hh + 1) * hd)
        s = _dot_nt(q_ref[0, :, cols], mk_ref[0, :, cols], precise) * (hd ** -0.5)
        p = jnp.exp(s - jnp.max(s, axis=-1, keepdims=True))
        p = p / jnp.sum(p, axis=-1, keepdims=True)
        out_ref[0, :, cols] = _dot(p, mv_ref[0, :, cols], precise).astype(out_ref.dtype)


def cross_attention_core(q, mk, mv, *, tq, precise):
    b, t, d = q.shape
    nm = mk.shape[1]
    tq = min(tq, t)
    q_spec = pl.BlockSpec((1, tq, d), lambda bi, qi: (bi, qi, 0))
    m_spec = pl.BlockSpec((1, nm, d), lambda bi, qi: (bi, 0, 0))
    return pl.pallas_call(
        functools.partial(_cross_kernel, precise=precise),
        grid=(b, t // tq),
        in_specs=[q_spec, m_spec, m_spec],
        out_specs=q_spec,
        out_shape=jax.ShapeDtypeStruct((b, t, d), F32 if precise else BF16),
        compiler_params=_params(("parallel", "arbitrary")),
        name="cross_attention_core",
    )(q, mk, mv)


def _first_argmax(vals, lane, valid):
    masked = jnp.where(valid, vals, -jnp.inf)
    mx = jnp.max(masked, axis=-1, keepdims=True)
    idx = jnp.min(jnp.where(masked == mx, lane, LANES), axis=-1, keepdims=True)
    return mx, idx


def _route(logits):
    lane = lax.broadcasted_iota(jnp.int32, logits.shape, 1).astype(F32)
    is_group = lane < N_GROUPS
    gmax, gidx = _first_argmax(logits, lane, is_group)
    gsum = jnp.sum(jnp.where(is_group, jnp.exp(logits - gmax), 0.0), axis=-1, keepdims=True)
    g_top = 1.0 / gsum
    lo = N_GROUPS + gidx * EXP_PER_GROUP
    in_group = (lane >= lo) & (lane < lo + EXP_PER_GROUP)
    e1, i1 = _first_argmax(logits, lane, in_group)
    e2, i2 = _first_argmax(logits, lane, in_group & (lane != i1))
    w2 = jnp.exp(e2 - e1)
    gate1 = g_top / (1.0 + w2)
    gate2 = g_top * w2 / (1.0 + w2)
    return i1 - N_GROUPS, i2 - N_GROUPS, gate1, gate2


def _combine_weights(logits):
    lane = lax.broadcasted_iota(jnp.int32, logits.shape, 1).astype(F32)
    x1, x2, gate1, gate2 = _route(logits)
    return jnp.where(lane == x1, gate1, 0.0) + jnp.where(lane == x2, gate2, 0.0)


def _moe_dense_kernel(x_ref, g_ref, wr_ref, wg_ref, wu_ref, wd_ref, fg_ref, out_ref, h_s, comb_s, acc_s,
                      *, precise_router):
    e = pl.program_id(1)

    @pl.when(e == 0)
    def _():
        h = _rms(x_ref[...], g_ref[...])
        h_s[...] = h.astype(BF16)
        comb_s[...] = _combine_weights(_dot(h, wr_ref[...], precise_router))
        acc_s[...] = x_ref[...]

    hb = h_s[...]
    act = jnp.dot(hb, wg_ref[0], preferred_element_type=F32)
    act = act * _sigmoid(act) * jnp.dot(hb, wu_ref[0], preferred_element_type=F32)
    y = jnp.dot(act.astype(BF16), wd_ref[0], preferred_element_type=F32)
    lane = lax.broadcasted_iota(jnp.int32, comb_s.shape, 1)
    ce = jnp.sum(jnp.where(lane == e, comb_s[...], 0.0), axis=-1, keepdims=True)
    acc_s[...] += ce * y

    @pl.when(e == pl.num_programs(1) - 1)
    def _():
        out_ref[...] = _rms(acc_s[...], fg_ref[...])


def moe_dense_final(x, g, w_router_pad, wg, wu, wd, final_g, *, tm, precise_router):
    t, d = x.shape
    tm = min(tm, t)
    ne, _, ff = wg.shape
    return pl.pallas_call(
        functools.partial(_moe_dense_kernel, precise_router=precise_router),
        grid=(t // tm, ne),
        in_specs=[
            pl.BlockSpec((tm, d), lambda i, e: (i, 0)),
            pl.BlockSpec((1, d), lambda i, e: (0, 0)),
            pl.BlockSpec((d, LANES), lambda i, e: (0, 0)),
            pl.BlockSpec((1, d, ff), lambda i, e: (e, 0, 0)),
            pl.BlockSpec((1, d, ff), lambda i, e: (e, 0, 0)),
            pl.BlockSpec((1, ff, d), lambda i, e: (e, 0, 0)),
            pl.BlockSpec((1, d), lambda i, e: (0, 0)),
        ],
        out_specs=pl.BlockSpec((tm, d), lambda i, e: (i, 0)),
        out_shape=jax.ShapeDtypeStruct((t, d), F32),
        scratch_shapes=[pltpu.VMEM((tm, d), BF16), pltpu.VMEM((tm, LANES), F32), pltpu.VMEM((tm, d), F32)],
        compiler_params=_params(("parallel", "arbitrary")),
        name="moe_dense_final",
    )(x, g.reshape(1, d), w_router_pad, wg, wu, wd, final_g.reshape(1, d))


ROW_UNIT = 16
ROUTE_TILE = MXU_DIM
EXPERT_TILE = 512
UNIT_BITS = (16, 8, 4, 2, 1)


def _sorted_cap(tr):
    rows = 2 * tr + N_EXPERTS * (ROW_UNIT - 1)
    return -(-rows // MXU_DIM) * MXU_DIM


def _chunk_dma(units, make_copy, wait):
    off = jnp.int32(0)
    for bit in UNIT_BITS:
        take = (units & bit) != 0

        @pl.when(take)
        def _(off=off, bit=bit):
            cp = make_copy(off, bit)
            if wait:
                cp.wait()
            else:
                cp.start()

        off = off + jnp.where(take, bit, 0)


def _rows(unit_start, units):
    return pl.ds(pl.multiple_of(unit_start * ROW_UNIT, ROW_UNIT), units * ROW_UNIT)


def _moe_route_kernel(x_ref, g_ref, wr_ref, xs_ref, info_ref, tab_ref, tot_ref, xc, run, sem, *, tr, cap, seg_units):
    i = pl.program_id(0)
    nt = pl.num_programs(0)

    @pl.when(i == 0)
    def _():
        for e in range(N_EXPERTS):
            run[e] = 0

    hb = _rms(x_ref[...], g_ref[...]).astype(BF16)
    e1, e2, g1, g2 = _route(jnp.dot(hb, wr_ref[...], preferred_element_type=F32))
    lane = lax.broadcasted_iota(jnp.int32, (tr, LANES), 1).astype(F32)
    a1 = lane == e1
    a2 = lane == e2
    assigned = jnp.where(a1 | a2, 1.0, 0.0)
    earlier = lax.broadcasted_iota(jnp.int32, (tr, tr), 1) < lax.broadcasted_iota(jnp.int32, (tr, tr), 0)
    rank = jnp.dot(jnp.where(earlier, 1.0, 0.0).astype(BF16), assigned.astype(BF16), preferred_element_type=F32)
    count = jnp.sum(assigned, axis=0, keepdims=True)
    units = jnp.floor((count + (ROW_UNIT - 1)) * (1.0 / ROW_UNIT))
    before = lax.broadcasted_iota(jnp.int32, (LANES, LANES), 0) < lax.broadcasted_iota(jnp.int32, (LANES, LANES), 1)
    units8 = jnp.broadcast_to(units, (SUBLANES, LANES)).astype(BF16)
    base = ROW_UNIT * jnp.dot(units8, jnp.where(before, 1.0, 0.0).astype(BF16), preferred_element_type=F32)[0:1]
    slot = base + rank
    slot1 = jnp.sum(jnp.where(a1, slot, 0.0), axis=1, keepdims=True)
    slot2 = jnp.sum(jnp.where(a2, slot, 0.0), axis=1, keepdims=True)
    info_ref[...] = jnp.where(lane == 0, slot1, jnp.where(lane == 1, slot2,
                              jnp.where(lane == 2, g1, jnp.where(lane == 3, g2, 0.0))))
    pos = lax.broadcasted_iota(jnp.int32, (tr, cap), 1).astype(F32)
    onehot_t = jnp.where((pos == slot1) | (pos == slot2), 1.0, 0.0).astype(BF16)
    xc[...] = lax.dot_general(onehot_t, hb, (((0,), (0,)), ((), ())), preferred_element_type=F32).astype(BF16)

    def copies(wait):
        src = jnp.int32(0)
        for e in range(N_EXPERTS):
            ne = tab_ref[i * 2 * N_EXPERTS + N_EXPERTS + e]
            dst = e * seg_units + tab_ref[i * 2 * N_EXPERTS + e]
            _chunk_dma(ne, lambda off, bit, src=src, dst=dst: pltpu.make_async_copy(
                xc.at[_rows(src + off, bit)], xs_ref.at[_rows(dst + off, bit)], sem), wait)
            src = src + ne

    for e in range(N_EXPERTS):
        ne = units[0, e].astype(jnp.int32)
        tab_ref[i * 2 * N_EXPERTS + e] = run[e]
        tab_ref[i * 2 * N_EXPERTS + N_EXPERTS + e] = ne
        run[e] = run[e] + ne
    copies(False)
    copies(True)

    @pl.when(i == nt - 1)
    def _():
        fill = EXPERT_TILE // ROW_UNIT
        xc[pl.ds(0, EXPERT_TILE), :] = jnp.zeros((EXPERT_TILE, xc.shape[1]), BF16)
        tails = [pltpu.make_async_copy(xc.at[_rows(0, fill)], xs_ref.at[_rows(e * seg_units + run[e], fill)], sem)
                 for e in range(N_EXPERTS)]
        for cp in tails:
            cp.start()
        for cp in tails:
            cp.wait()
        for e in range(N_EXPERTS):
            tot_ref[e] = run[e]


def _moe_expert_kernel(eo_ref, rb_ref, valid_ref, xs_ref, wg_ref, wu_ref, wd_ref, ys_ref):
    @pl.when(valid_ref[pl.program_id(0)] == 1)
    def _():
        x = xs_ref[...]
        act = jnp.dot(x, wg_ref[0], preferred_element_type=F32)
        act = act * _sigmoid(act) * jnp.dot(x, wu_ref[0], preferred_element_type=F32)
        ys_ref[...] = jnp.dot(act.astype(BF16), wd_ref[0], preferred_element_type=F32).astype(ys_ref.dtype)


def _moe_combine_kernel(tab_ref, x_ref, info_ref, ys_ref, fg_ref, out_ref, yc, sem, *, tr, cap, seg_units):
    i = pl.program_id(0)

    @pl.when(i == 0)
    def _():
        yc[...] = jnp.zeros(yc.shape, yc.dtype)

    def copies(wait):
        dst = jnp.int32(0)
        for e in range(N_EXPERTS):
            ne = tab_ref[i * 2 * N_EXPERTS + N_EXPERTS + e]
            src = e * seg_units + tab_ref[i * 2 * N_EXPERTS + e]
            _chunk_dma(ne, lambda off, bit, src=src, dst=dst: pltpu.make_async_copy(
                ys_ref.at[_rows(src + off, bit)], yc.at[_rows(dst + off, bit)], sem), wait)
            dst = dst + ne

    copies(False)
    copies(True)
    info = info_ref[...]
    pos = lax.broadcasted_iota(jnp.int32, (tr, cap), 1).astype(F32)
    rows = yc[...]
    y1 = jnp.dot(jnp.where(pos == info[:, 0:1], 1.0, 0.0).astype(BF16), rows, preferred_element_type=F32)
    y2 = jnp.dot(jnp.where(pos == info[:, 1:2], 1.0, 0.0).astype(BF16), rows, preferred_element_type=F32)
    out_ref[...] = _rms(x_ref[...] + info[:, 2:3] * y1 + info[:, 3:4] * y2, fg_ref[...])


def moe_sparse_final(x, g, w_router_pad, wg, wu, wd, final_g):
    t, d = x.shape
    tr, te = ROUTE_TILE, EXPERT_TILE
    assert t % tr == 0
    ntiles = t // tr
    cap = _sorted_cap(tr)
    ne, _, ff = wg.shape
    seg_rows = -(-(t + (ROW_UNIT - 1) * ntiles + te) // te) * te
    seg_units = seg_rows // ROW_UNIT
    smem = pl.BlockSpec(memory_space=pltpu.SMEM)

    xs, info, tab, tot = pl.pallas_call(
        functools.partial(_moe_route_kernel, tr=tr, cap=cap, seg_units=seg_units),
        grid=(ntiles,),
        in_specs=[
            pl.BlockSpec((tr, d), lambda i: (i, 0)),
            pl.BlockSpec((1, d), lambda i: (0, 0)),
            pl.BlockSpec((d, LANES), lambda i: (0, 0)),
        ],
        out_specs=[pl.BlockSpec(memory_space=pl.ANY), pl.BlockSpec((tr, LANES), lambda i: (i, 0)), smem, smem],
        out_shape=[
            jax.ShapeDtypeStruct((ne * seg_rows, d), BF16),
            jax.ShapeDtypeStruct((t, LANES), F32),
            jax.ShapeDtypeStruct((ntiles * 2 * ne,), jnp.int32),
            jax.ShapeDtypeStruct((ne,), jnp.int32),
        ],
        scratch_shapes=[pltpu.VMEM((cap, d), BF16), pltpu.SMEM((ne,), jnp.int32), pltpu.SemaphoreType.DMA],
        compiler_params=_params(("arbitrary",)),
        name="moe_route",
    )(x, g.reshape(1, d), w_router_pad)

    tiles_per_e = (tot * ROW_UNIT + te - 1) // te
    ends = jnp.cumsum(tiles_per_e)
    n_items = ends[-1]
    max_items = (2 * t + ne * (ROW_UNIT - 1) * ntiles) // te + ne
    w = jnp.arange(max_items, dtype=jnp.int32)
    wc = jnp.minimum(w, n_items - 1)
    eo = jnp.searchsorted(ends, wc, side="right").astype(jnp.int32)
    rb = (eo * (seg_rows // te) + wc - (ends - tiles_per_e)[eo]).astype(jnp.int32)
    valid = (w < n_items).astype(jnp.int32)

    ys = pl.pallas_call(
        _moe_expert_kernel,
        grid_spec=pltpu.PrefetchScalarGridSpec(
            num_scalar_prefetch=3,
            grid=(max_items,),
            in_specs=[
                pl.BlockSpec((te, d), lambda w, eo, rb, va: (rb[w], 0)),
                pl.BlockSpec((1, d, ff), lambda w, eo, rb, va: (eo[w], 0, 0)),
                pl.BlockSpec((1, d, ff), lambda w, eo, rb, va: (eo[w], 0, 0)),
                pl.BlockSpec((1, ff, d), lambda w, eo, rb, va: (eo[w], 0, 0)),
            ],
            out_specs=pl.BlockSpec((te, d), lambda w, eo, rb, va: (rb[w], 0)),
        ),
        out_shape=jax.ShapeDtypeStruct((ne * seg_rows, d), BF16),
        compiler_params=_params(("arbitrary",)),
        name="moe_experts",
    )(eo, rb, valid, xs, wg, wu, wd)

    return pl.pallas_call(
        functools.partial(_moe_combine_kernel, tr=tr, cap=cap, seg_units=seg_units),
        grid_spec=pltpu.PrefetchScalarGridSpec(
            num_scalar_prefetch=1,
            grid=(ntiles,),
            in_specs=[
                pl.BlockSpec((tr, d), lambda i, tab: (i, 0)),
                pl.BlockSpec((tr, LANES), lambda i, tab: (i, 0)),
                pl.BlockSpec(memory_space=pl.ANY),
                pl.BlockSpec((1, d), lambda i, tab: (0, 0)),
            ],
            out_specs=pl.BlockSpec((tr, d), lambda i, tab: (i, 0)),
            scratch_shapes=[pltpu.VMEM((cap, d), BF16), pltpu.SemaphoreType.DMA],
        ),
        out_shape=jax.ShapeDtypeStruct((t, d), F32),
        compiler_params=_params(("arbitrary",)),
        name="moe_combine",
    )(tab, x, info, ys, final_g.reshape(1, d))


def _trunk(x, mem_k, mem_v, conv_buf, h0, past_k, past_v, p, lam_init):
    b, t, d = x.shape
    n = b * t
    xf = x.reshape(n, d)
    aw = DIFF_HEADS * DIFF_VD
    lru_w = p["lru_lambda"].shape[0]
    precise = past_k is not None
    tm = n if precise else 512
    tn = 1024
    assert t >= CONV_W - 1
    seq = lambda a: a.reshape(b, t, a.shape[-1])
    lam_params = (p["lam_q1"], p["lam_k1"], p["lam_q2"], p["lam_k2"])
    lru_args = (p["conv_w"], p["conv_b"], p["lru_wa"], p["lru_ba"].reshape(-1), p["lru_wx"],
                p["lru_bx"].reshape(-1), p["lru_lambda"])

    if precise:
        xb, gate, q, k, v = norm_linear_f32(xf, p["norm_mix_g"], p["w_in_f32"], tn=tn, split=True)
    else:
        xb, gate, q, k, v = norm_linear(xf, p["norm_mix_g"], p["w_in"], tm=tm, tn=tn,
                                        out_widths=[lru_w, lru_w, aw, aw, aw], out_dtypes=[F32] * 5)
    lru_out, h_last = lru_mixer(seq(xb), seq(gate), conv_buf, h0, *lru_args, tc=256, precise=precise)
    if precise:
        att = diff_attention_sample(seq(q), seq(k), seq(v), past_k, past_v, lam_params, p["subln_g"], lam_init)
    else:
        att = diff_attention_prompt(seq(q), seq(k), seq(v), lam_params, p["subln_g"], lam_init, tq=512)
    mix_in = [lru_out.reshape(n, lru_w), att.reshape(n, aw)]
    if precise:
        x1 = linear_residual_f32(xf, mix_in, p["w_out_f32"], tn=tn)
        (qx,) = norm_linear_f32(x1, p["norm_cross_g"], p["xq_w_f32"], tn=tn, split=False)
    else:
        x1 = linear_residual(xf, mix_in, p["w_out"], tm=tm, tn=tn)
        (qx,) = norm_linear(x1, p["norm_cross_g"], p["xq_w"], tm=tm, tn=tn, out_widths=[d], out_dtypes=[BF16])
    o = cross_attention_core(seq(qx), mem_k, mem_v, tq=512, precise=precise)
    if precise:
        x2 = linear_residual_f32(x1, [o.reshape(n, d)], p["xo_w_f32"], tn=tn)
    else:
        x2 = linear_residual(x1, [o.reshape(n, d)], p["xo_w"], tm=tm, tn=tn)

    experts = (p["exp_gate"], p["exp_up"], p["exp_down"])
    if precise:
        y = moe_dense_final(x2, p["norm_ffn_g"], p["router_pad_f32"], *experts, p["final_norm_g"], tm=tm,
                            precise_router=True)
    else:
        y = moe_sparse_final(x2, p["norm_ffn_g"], p["router_pad"], *experts, p["final_norm_g"])
    new_conv = seq(xb)[:, t - (CONV_W - 1):, :]
    return y.reshape(b, t, d), new_conv, h_last, k, v


def kernel(x_prompt, x_sample, cache_diff_k, cache_diff_v, cache_mem_k, cache_mem_v, state_conv, state_lru, mem_prompt, norm_mix_g, w_in, conv_w, conv_b, lru_wa, lru_ba, lru_wx, lru_bx, lru_lambda, lam_q1, lam_k1, lam_q2, lam_k2, subln_g, w_out, norm_cross_g, norm_mem_g, xq_w, xk_w, xv_w, xo_w, norm_ffn_g, router_group_w, router_expert_w, exp_gate, exp_up, exp_down, final_norm_g):
    depth = w_in.shape[0]
    assert depth == 1, "single-layer step"
    bp, tp, d = x_prompt.shape
    bs, ts, _ = x_sample.shape
    past = cache_diff_k.shape[2]
    n_mem = mem_prompt.shape[1]
    aw = DIFF_HEADS * DIFF_VD
    l = 0
    lam_init = 0.8 - 0.6 * math.exp(-0.3 * l)

    router = jnp.concatenate([router_group_w[l], router_expert_w[l]], axis=1)
    router_pad_f32 = jnp.pad(router, ((0, 0), (0, LANES - router.shape[1])))
    router_pad = router_pad_f32.astype(BF16)
    p = dict(router_pad_f32=router_pad_f32, w_in_f32=w_in[l], w_out_f32=w_out[l], xq_w_f32=xq_w[l],
             xo_w_f32=xo_w[l], **dict(norm_mix_g=norm_mix_g[l], conv_w=conv_w[l], conv_b=conv_b[l], lru_wa=lru_wa[l],
             lru_ba=lru_ba[l], lru_wx=lru_wx[l], lru_bx=lru_bx[l], lru_lambda=lru_lambda[l], lam_q1=lam_q1[l],
             lam_k1=lam_k1[l], lam_q2=lam_q2[l], lam_k2=lam_k2[l], subln_g=subln_g[l],
             norm_cross_g=norm_cross_g[l], norm_ffn_g=norm_ffn_g[l], router_pad=router_pad,
             final_norm_g=final_norm_g))
    for name, w in (("w_in", w_in), ("w_out", w_out), ("xq_w", xq_w), ("xo_w", xo_w), ("exp_gate", exp_gate),
                    ("exp_up", exp_up), ("exp_down", exp_down)):
        p[name] = w[l].astype(BF16)

    memf = mem_prompt.reshape(bp * n_mem, d)
    w_mem = jnp.concatenate([xk_w[l].astype(BF16), xv_w[l].astype(BF16)], axis=1)
    mk_p, mv_p = norm_linear(memf, norm_mem_g[l], w_mem, tm=512, tn=1024, out_widths=[d, d], out_dtypes=[F32, F32])
    mk_p = mk_p.reshape(bp, n_mem, d)
    mv_p = mv_p.reshape(bp, n_mem, d)

    zero_buf = jnp.zeros((bp, CONV_W - 1, lru_lambda.shape[1]), F32)
    zero_h = jnp.zeros((bp, lru_lambda.shape[1]), F32)
    y_p, cb_p, hl_p, k_p, v_p = _trunk(x_prompt, mk_p, mv_p, zero_buf, zero_h, None, None, p, lam_init)
    y_s, cb_s, hl_s, k_s, v_s = _trunk(x_sample, cache_mem_k[l].reshape(bs, n_mem, d),
                                       cache_mem_v[l].reshape(bs, n_mem, d), state_conv[l], state_lru[l],
                                       cache_diff_k[l].reshape(bs, past, aw), cache_diff_v[l].reshape(bs, past, aw),
                                       p, lam_init)

    hd2 = 2 * DIFF_HD
    return (y_p, y_s,
            k_p.reshape(1, bp, tp, DIFF_HEADS, hd2), v_p.reshape(1, bp, tp, DIFF_HEADS, DIFF_VD),
            mk_p.reshape(1, bp, n_mem, X_HEADS, d // X_HEADS), mv_p.reshape(1, bp, n_mem, X_HEADS, d // X_HEADS),
            cb_p[None], hl_p[None],
            k_s.reshape(1, bs, ts, DIFF_HEADS, hd2), v_s.reshape(1, bs, ts, DIFF_HEADS, DIFF_VD),
            cb_s[None], hl_s[None].astype(state_lru.dtype))
```

```python
import functools
import math

import jax
import jax.numpy as jnp
from jax import lax
from jax.experimental import pallas as pl
from jax.experimental.pallas import tpu as pltpu

F32 = jnp.float32
BF16 = jnp.bfloat16
HIGHEST = lax.Precision.HIGHEST

CHUNK = 64
CONV_W = 4
LRU_C = 8.0
LRU_BLOCK = 64
DIFF_HEADS = 8
DIFF_HD = 64
DIFF_VD = 2 * DIFF_HD
X_HEADS = 4
N_GROUPS = 4
EXP_PER_GROUP = 4
N_EXPERTS = N_GROUPS * EXP_PER_GROUP
EPS = 1e-6

LANES = 128
SUBLANES = 8
MXU_DIM = 256
VMEM_LIMIT_BYTES = 56 * 1024 * 1024


def _params(semantics):
    return pltpu.CompilerParams(dimension_semantics=semantics, vmem_limit_bytes=VMEM_LIMIT_BYTES)


def _dot(a, b, precise=False):
    if precise:
        return jnp.dot(a.astype(F32), b.astype(F32), precision=HIGHEST, preferred_element_type=F32)
    return jnp.dot(a.astype(BF16), b.astype(BF16), preferred_element_type=F32)


def _dot_nt(a, b, precise=False):
    dims = (((1,), (1,)), ((), ()))
    if precise:
        return lax.dot_general(a.astype(F32), b.astype(F32), dims, precision=HIGHEST, preferred_element_type=F32)
    return lax.dot_general(a.astype(BF16), b.astype(BF16), dims, preferred_element_type=F32)


def _rms(x, g):
    return x * lax.rsqrt(jnp.mean(x * x, axis=-1, keepdims=True) + EPS) * g


def _sigmoid(x):
    return 1.0 / (1.0 + jnp.exp(-x))


def _gelu_tanh(x):
    c = math.sqrt(2.0 / math.pi)
    return 0.5 * x * (1.0 + jnp.tanh(c * (x + 0.044715 * (x * x * x))))


def _norm_linear_kernel(x_ref, g_ref, w_ref, *out_refs, tn):
    h = _rms(x_ref[...], g_ref[...]).astype(BF16)
    col = 0
    for o_ref in out_refs:
        for c in range(o_ref.shape[1] // tn):
            o_ref[:, c * tn:(c + 1) * tn] = jnp.dot(
                h, w_ref[:, col:col + tn], preferred_element_type=F32).astype(o_ref.dtype)
            col += tn


def norm_linear(x, g, w, *, tm, tn, out_widths, out_dtypes):
    t, k = x.shape
    n = w.shape[1]
    tm = min(tm, t)
    assert sum(out_widths) == n and all(wd % tn == 0 for wd in out_widths)
    return pl.pallas_call(
        functools.partial(_norm_linear_kernel, tn=tn),
        grid=(t // tm,),
        in_specs=[
            pl.BlockSpec((tm, k), lambda i: (i, 0)),
            pl.BlockSpec((1, k), lambda i: (0, 0)),
            pl.BlockSpec((k, n), lambda i: (0, 0)),
        ],
        out_specs=[pl.BlockSpec((tm, wd), lambda i: (i, 0)) for wd in out_widths],
        out_shape=[jax.ShapeDtypeStruct((t, wd), dt) for wd, dt in zip(out_widths, out_dtypes)],
        compiler_params=_params(("parallel",)),
        name="norm_linear",
    )(x, g.reshape(1, k), w)


def _linear_res_kernel(*refs, n_in, tn):
    res_ref = refs[0]
    a_refs = refs[1:1 + n_in]
    w_ref = refs[1 + n_in]
    out_ref = refs[2 + n_in]
    kc = a_refs[0].shape[1]
    for c in range(out_ref.shape[1] // tn):
        cols = slice(c * tn, (c + 1) * tn)
        acc = res_ref[:, cols]
        for r, a_ref in enumerate(a_refs):
            acc = acc + _dot(a_ref[...], w_ref[r * kc:(r + 1) * kc, cols])
        out_ref[:, cols] = acc


def linear_residual(res, a_list, w, *, tm, tn):
    t, n = res.shape
    tm = min(tm, t)
    n_in = len(a_list)
    kc = a_list[0].shape[1]
    in_specs = [pl.BlockSpec((tm, n), lambda i: (i, 0))]
    in_specs += [pl.BlockSpec((tm, kc), lambda i: (i, 0)) for _ in range(n_in)]
    in_specs += [pl.BlockSpec(w.shape, lambda i: (0, 0))]
    return pl.pallas_call(
        functools.partial(_linear_res_kernel, n_in=n_in, tn=tn),
        grid=(t // tm,),
        in_specs=in_specs,
        out_specs=pl.BlockSpec((tm, n), lambda i: (i, 0)),
        out_shape=jax.ShapeDtypeStruct((t, n), F32),
        compiler_params=_params(("parallel",)),
        name="linear_residual",
    )(res, *a_list, w)


def _norm_linear_f32_kernel(x_ref, g_ref, w_ref, *refs):
    out_refs, h_ref = refs[:-1], refs[-1]
    j = pl.program_id(1)

    @pl.when(j == 0)
    def _():
        h_ref[...] = _rms(x_ref[...], g_ref[...])

    if len(out_refs) == 1:
        out_refs[0][...] = _dot(h_ref[...], w_ref[...], True)
    else:
        for c, o_ref in enumerate(out_refs):
            @pl.when(j == c)
            def _(o_ref=o_ref):
                o_ref[...] = _dot(h_ref[...], w_ref[...], True)


def norm_linear_f32(x, g, w, *, tn, split):
    t, k = x.shape
    n = w.shape[1]
    nj = n // tn
    if split:
        out_shape = [jax.ShapeDtypeStruct((t, tn), F32) for _ in range(nj)]
        out_specs = [pl.BlockSpec((t, tn), lambda i, j: (i, 0)) for _ in range(nj)]
    else:
        out_shape = [jax.ShapeDtypeStruct((t, n), F32)]
        out_specs = [pl.BlockSpec((t, tn), lambda i, j: (i, j))]
    return pl.pallas_call(
        _norm_linear_f32_kernel,
        grid=(1, nj),
        in_specs=[
            pl.BlockSpec((t, k), lambda i, j: (i, 0)),
            pl.BlockSpec((1, k), lambda i, j: (0, 0)),
            pl.BlockSpec((k, tn), lambda i, j: (0, j)),
        ],
        out_specs=out_specs,
        out_shape=out_shape,
        scratch_shapes=[pltpu.VMEM((t, k), F32)],
        compiler_params=_params(("parallel", "arbitrary")),
        name="norm_linear_f32",
    )(x, g.reshape(1, k), w)


def _linear_res_f32_kernel(*refs, n_in):
    res_ref = refs[0]
    a_refs = refs[1:1 + n_in]
    w_refs = refs[1 + n_in:1 + 2 * n_in]
    out_ref = refs[1 + 2 * n_in]
    acc = res_ref[...]
    for a_ref, w_ref in zip(a_refs, w_refs):
        acc = acc + _dot(a_ref[...], w_ref[...], True)
    out_ref[...] = acc


def linear_residual_f32(res, a_list, w, *, tn):
    t, n = res.shape
    n_in = len(a_list)
    kc = a_list[0].shape[1]
    in_specs = [pl.BlockSpec((t, tn), lambda i, j: (i, j))]
    in_specs += [pl.BlockSpec((t, kc), lambda i, j: (i, 0)) for _ in range(n_in)]
    in_specs += [pl.BlockSpec((kc, tn), lambda i, j, c=c: (c, j)) for c in range(n_in)]
    return pl.pallas_call(
        functools.partial(_linear_res_f32_kernel, n_in=n_in),
        grid=(1, n // tn),
        in_specs=in_specs,
        out_specs=pl.BlockSpec((t, tn), lambda i, j: (i, j)),
        out_shape=jax.ShapeDtypeStruct((t, n), F32),
        compiler_params=_params(("parallel", "arbitrary")),
        name="linear_residual_f32",
    )(res, *a_list, *([w] * n_in))


def _lru_kernel(xb_ref, gate_ref, cbuf_ref, h0_ref, cw_ref, cb_ref, wa_ref, ba_ref, wx_ref, bx_ref, lam_ref,
                out_ref, hlast_ref, xpad, hcar, a_s, u_s, *, tc, width, precise):
    c = pl.program_id(1)
    nslab = width // MXU_DIM
    ngrp = tc // SUBLANES

    @pl.when(c == 0)
    def _():
        xpad[pl.ds(0, SUBLANES), :] = cbuf_ref[0]
        hcar[...] = h0_ref[0]

    xpad[pl.ds(SUBLANES, tc), :] = xb_ref[0]
    xc = cb_ref[...] + cw_ref[pl.ds(CONV_W - 1, 1), :] * xpad[pl.ds(SUBLANES, tc), :]
    for j in range(CONV_W - 1):
        xc = xc + cw_ref[pl.ds(j, 1), :] * xpad[pl.ds(SUBLANES - (CONV_W - 1) + j, tc), :]
    xpad[pl.ds(0, SUBLANES), :] = xpad[pl.ds(tc, SUBLANES), :]

    lam = lam_ref[...]
    softplus_neg = jnp.maximum(-lam, 0.0) + jnp.log1p(jnp.exp(-jnp.abs(lam)))
    c8 = -LRU_C * softplus_neg

    sub = lax.broadcasted_iota(jnp.int32, (ngrp, SUBLANES, MXU_DIM), 1)
    for s in range(nslab):
        cols = slice(s * MXU_DIM, (s + 1) * MXU_DIM)
        xs = xc[:, cols]
        r = _sigmoid(_dot(xs, wa_ref[s], precise) + ba_ref[:, cols])
        i = _sigmoid(_dot(xs, wx_ref[s], precise) + bx_ref[:, cols])
        a = jnp.exp(c8[:, cols] * r)
        u = jnp.sqrt(1.0 - a * a) * (i * xs)
        a3 = a.reshape(ngrp, SUBLANES, MXU_DIM)
        u3 = u.reshape(ngrp, SUBLANES, MXU_DIM)
        d = 1
        while d < SUBLANES:
            a_sh = pltpu.roll(a3, d, 1)
            u_sh = pltpu.roll(u3, d, 1)
            keep = sub >= d
            u3 = jnp.where(keep, u3 + a3 * u_sh, u3)
            a3 = jnp.where(keep, a3 * a_sh, a3)
            d *= 2
        a_s[:, cols] = a3.reshape(tc, MXU_DIM)
        u_s[:, cols] = u3.reshape(tc, MXU_DIM)

    def body(g, hin):
        rows = pl.ds(pl.multiple_of(g * SUBLANES, SUBLANES), SUBLANES)
        h = u_s[rows, :] + a_s[rows, :] * hin
        u_s[rows, :] = h
        return h[SUBLANES - 1:SUBLANES, :]

    hfin = lax.fori_loop(0, ngrp, body, hcar[...])
    hcar[...] = hfin
    out_ref[0] = (u_s[...] * _gelu_tanh(gate_ref[0])).astype(out_ref.dtype)
    hlast_ref[0] = hfin


def _block_diag(w, per):
    nb, k, _ = w.shape
    w4 = w.reshape(nb // per, per, k, k)
    eye = jnp.eye(per, dtype=w.dtype)
    return jnp.einsum("cipq,ij->cipjq", w4, eye).reshape(nb // per, per * k, per * k)


def lru_mixer(xb, gate, conv_buf, h0, conv_w, conv_b, wa, ba, wx, bx, lam, *, tc, precise):
    b, t, width = xb.shape
    tc = min(tc, t)
    per = MXU_DIM // LRU_BLOCK
    act_dt = F32 if precise else BF16
    wa_bd = _block_diag(wa, per).astype(act_dt)
    wx_bd = _block_diag(wx, per).astype(act_dt)
    nslab = wa_bd.shape[0]
    cbuf8 = jnp.concatenate([jnp.zeros((b, SUBLANES - (CONV_W - 1), width), F32), conv_buf.astype(F32)], axis=1)
    row = lambda v: v.reshape(1, width).astype(F32)
    vec_spec = pl.BlockSpec((1, width), lambda bi, ci: (0, 0))
    seq_spec = pl.BlockSpec((1, tc, width), lambda bi, ci: (bi, ci, 0))
    wspec = pl.BlockSpec((nslab, MXU_DIM, MXU_DIM), lambda bi, ci: (0, 0, 0))
    out, hlast = pl.pallas_call(
        functools.partial(_lru_kernel, tc=tc, width=width, precise=precise),
        grid=(b, t // tc),
        in_specs=[
            seq_spec, seq_spec,
            pl.BlockSpec((1, SUBLANES, width), lambda bi, ci: (bi, 0, 0)),
            pl.BlockSpec((1, 1, width), lambda bi, ci: (bi, 0, 0)),
            pl.BlockSpec((CONV_W, width), lambda bi, ci: (0, 0)),
            vec_spec, wspec, vec_spec, wspec, vec_spec, vec_spec,
        ],
        out_specs=[seq_spec, pl.BlockSpec((1, 1, width), lambda bi, ci: (bi, 0, 0))],
        out_shape=[jax.ShapeDtypeStruct((b, t, width), act_dt), jax.ShapeDtypeStruct((b, 1, width), F32)],
        scratch_shapes=[
            pltpu.VMEM((tc + SUBLANES, width), F32),
            pltpu.VMEM((1, width), F32),
            pltpu.VMEM((tc, width), F32),
            pltpu.VMEM((tc, width), F32),
        ],
        compiler_params=_params(("parallel", "arbitrary")),
        name="lru_mixer",
    )(xb, gate, cbuf8, h0.reshape(b, 1, width).astype(F32), conv_w.astype(F32), row(conv_b), wa_bd,
      row(ba), wx_bd, row(bx), row(lam))
    return out, hlast.reshape(b, width)


def _diff_lambda(lq1, lk1, lq2, lk2, lam_init):
    s1 = jnp.sum(lq1[...] * lk1[...], axis=-1, keepdims=True)
    s2 = jnp.sum(lq2[...] * lk2[...], axis=-1, keepdims=True)
    return jnp.exp(s1) - jnp.exp(s2) + lam_init


_LOG2E_PARTS = (1.4453125, -0.00262451171875, 7.063150405883789e-06)
LOG2E = sum(_LOG2E_PARTS)
N_BIAS_COLS = 2 * len(_LOG2E_PARTS)


def _attn_prompt_kernel(slopes_ref, q_ref, k_ref, v_ref, lq1, lk1, lq2, lk2, subln_ref, out_ref,
                        kaug, vt, qt, m_s, l_s, acc_s, *, tq, lam_init):
    h = pl.program_id(1)
    qi = pl.program_id(2)
    slope = slopes_ref[h]
    nblk, tk, _ = kaug.shape
    nparts = len(_LOG2E_PARTS)

    @pl.when(qi == 0)
    def _():
        pos = lax.broadcasted_iota(jnp.int32, (tk, LANES), 0)
        lane = lax.broadcasted_iota(jnp.int32, (tk, LANES), 1)
        within = (pos % CHUNK).astype(F32) * slope
        for j in range(nblk):
            rows = slice(j * tk, (j + 1) * tk)
            kaug[j, :, 0:LANES] = k_ref[0, rows, :].astype(BF16)
            coarse = ((pos + j * tk) // CHUNK * CHUNK).astype(F32) * slope
            cols = jnp.where(lane < nparts, coarse, jnp.where(lane < N_BIAS_COLS, within, 0.0))
            kaug[j, :, LANES:2 * LANES] = cols.astype(BF16)
            vt[j] = v_ref[0, rows, :].T.astype(BF16)
        r = lax.broadcasted_iota(jnp.int32, (LANES, tq), 0)
        part = jnp.where(r % nparts == 0, _LOG2E_PARTS[0],
                         jnp.where(r % nparts == 1, _LOG2E_PARTS[1], _LOG2E_PARTS[2]))
        const_rows = jnp.where(r < N_BIAS_COLS, part, 0.0).astype(BF16)
        qt[0, LANES:2 * LANES, :] = const_rows
        qt[1, LANES:2 * LANES, :] = const_rows

    qs = q_ref[0] * (LOG2E * DIFF_HD ** -0.5)
    qlane = lax.broadcasted_iota(jnp.int32, qs.shape, 1)
    qt[0, 0:LANES, :] = jnp.where(qlane < DIFF_HD, qs, 0.0).T.astype(BF16)
    qt[1, 0:LANES, :] = jnp.where(qlane >= DIFF_HD, qs, 0.0).T.astype(BF16)
    m_s[...] = jnp.full(m_s.shape, -jnp.inf, F32)
    l_s[...] = jnp.zeros(l_s.shape, F32)
    acc_s[...] = jnp.zeros(acc_s.shape, F32)

    def blocks(kis, diagonal):
        if diagonal:
            kpos = lax.broadcasted_iota(jnp.int32, (tk, tq), 0)
            qpos = lax.broadcasted_iota(jnp.int32, (tk, tq), 1)
            ahead = (kpos - qpos).astype(F32)
            fix = jnp.where(kpos > qpos, (-2.0 * LOG2E) * slope * ahead, 0.0)
            allowed = (kpos // CHUNK) <= (qpos // CHUNK)
        scores = [[jnp.dot(kaug[ki], qt[m], preferred_element_type=F32) for m in range(2)] for ki in kis]
        for j, ki in enumerate(kis):
            vblk = vt[ki]
            for m in range(2):
                s = scores[j][m]
                if diagonal and j == len(kis) - 1:
                    s = jnp.where(allowed, s + fix, -jnp.inf)
                m_old = m_s[m]
                m_new = jnp.maximum(m_old, jnp.max(s, axis=0, keepdims=True))
                p = jnp.exp2(s - m_new)
                alpha = jnp.exp2(m_old - m_new)
                l_s[m] = alpha * l_s[m] + jnp.sum(p, axis=0, keepdims=True)
                acc_s[m] = alpha * acc_s[m] + jnp.dot(vblk, p.astype(BF16), preferred_element_type=F32)
                m_s[m] = m_new

    def pair_body(j, carry):
        blocks([2 * j, 2 * j + 1], False)
        return carry

    lax.fori_loop(0, qi // 2, pair_body, 0)

    @pl.when(qi % 2 == 1)
    def _():
        blocks([qi - 1, qi], True)

    @pl.when(qi % 2 == 0)
    def _():
        blocks([qi], True)

    lam = _diff_lambda(lq1, lk1, lq2, lk2, lam_init)
    o = acc_s[0] / l_s[0] - lam * (acc_s[1] / l_s[1])
    o = o * lax.rsqrt(jnp.mean(o * o, axis=0, keepdims=True) + EPS) * subln_ref[...] * (1.0 - lam_init)
    out_ref[0] = o.T.astype(out_ref.dtype)


def _alibi_slopes():
    return 2.0 ** (-8.0 * jnp.arange(1, DIFF_HEADS + 1, dtype=F32) / DIFF_HEADS)


def diff_attention_prompt(q, k, v, lam_params, subln_g, lam_init, *, tq):
    b, t, aw = q.shape
    tq = min(tq, t)
    hd2 = 2 * DIFF_HD
    assert hd2 == LANES and DIFF_VD == LANES and tq % CHUNK == 0
    assert t // CHUNK <= 256, "chunk index must stay exact in bf16"
    lrow = lambda p: p.reshape(1, DIFF_HD).astype(F32)
    lspec = pl.BlockSpec((1, DIFF_HD), lambda bi, hi, qi: (0, 0))
    kv_spec = pl.BlockSpec((1, t, hd2), lambda bi, hi, qi: (bi, 0, hi))
    q_spec = pl.BlockSpec((1, tq, hd2), lambda bi, hi, qi: (bi, qi, hi))
    return pl.pallas_call(
        functools.partial(_attn_prompt_kernel, tq=tq, lam_init=lam_init),
        grid=(b, DIFF_HEADS, t // tq),
        in_specs=[
            pl.BlockSpec(memory_space=pltpu.SMEM),
            q_spec, kv_spec, kv_spec, lspec, lspec, lspec, lspec,
            pl.BlockSpec((DIFF_VD, 1), lambda bi, hi, qi: (0, 0)),
        ],
        out_specs=q_spec,
        out_shape=jax.ShapeDtypeStruct((b, t, aw), BF16),
        scratch_shapes=[
            pltpu.VMEM((t // tq, tq, 2 * LANES), BF16),
            pltpu.VMEM((t // tq, DIFF_VD, tq), BF16),
            pltpu.VMEM((2, 2 * LANES, tq), BF16),
            pltpu.VMEM((2, 1, tq), F32),
            pltpu.VMEM((2, 1, tq), F32),
            pltpu.VMEM((2, DIFF_VD, tq), F32),
        ],
        compiler_params=_params(("parallel", "parallel", "arbitrary")),
        name="diff_attention_prompt",
    )(_alibi_slopes(), q, k, v, *[lrow(p) for p in lam_params], subln_g.reshape(DIFF_VD, 1).astype(F32))


def _attn_sample_kernel(q_ref, kp_ref, vp_ref, kn_ref, vn_ref, lq1, lk1, lq2, lk2, subln_ref, out_ref,
                        *, past, tq, lam_init):
    hd2 = 2 * DIFF_HD

    def bias_mask(nk, k_off):
        qpos = past + lax.broadcasted_iota(jnp.int32, (tq, nk), 0)
        kpos = k_off + lax.broadcasted_iota(jnp.int32, (tq, nk), 1)
        dist = jnp.abs(qpos - kpos).astype(F32)
        allowed = (kpos // CHUNK) <= (qpos // CHUNK)
        return dist, allowed

    dist_p, ok_p = bias_mask(past, 0)
    dist_n, ok_n = bias_mask(tq, past)
    lam = _diff_lambda(lq1, lk1, lq2, lk2, lam_init)
    qlane = lax.broadcasted_iota(jnp.int32, (tq, hd2), 1)
    for h in range(DIFF_HEADS):
        slope = 2.0 ** (-8.0 * (h + 1) / DIFF_HEADS)
        cols = slice(h * hd2, (h + 1) * hd2)
        q = q_ref[0, :, cols]
        kp, vp = kp_ref[0, :, h, :], vp_ref[0, :, h, :]
        kn, vn = kn_ref[0, :, cols], vn_ref[0, :, cols]
        probs = []
        for m in range(2):
            qm = jnp.where((qlane >= DIFF_HD) == (m == 1), q, 0.0)
            sp = jnp.where(ok_p, _dot_nt(qm, kp, True) * (DIFF_HD ** -0.5) - slope * dist_p, -jnp.inf)
            sn = jnp.where(ok_n, _dot_nt(qm, kn, True) * (DIFF_HD ** -0.5) - slope * dist_n, -jnp.inf)
            mx = jnp.maximum(jnp.max(sp, axis=-1, keepdims=True), jnp.max(sn, axis=-1, keepdims=True))
            pp = jnp.exp(sp - mx)
            pn = jnp.exp(sn - mx)
            l = jnp.sum(pp, axis=-1, keepdims=True) + jnp.sum(pn, axis=-1, keepdims=True)
            probs.append((pp / l, pn / l))
        o = _dot(probs[0][0] - lam * probs[1][0], vp, True) + _dot(probs[0][1] - lam * probs[1][1], vn, True)
        out_ref[0, :, cols] = (o * lax.rsqrt(jnp.mean(o * o, axis=-1, keepdims=True) + EPS) * subln_ref[...]
                               * (1.0 - lam_init)).astype(out_ref.dtype)


def diff_attention_sample(q, k_new, v_new, past_k, past_v, lam_params, subln_g, lam_init):
    b, t, aw = q.shape
    past = past_k.shape[1]
    lrow = lambda p: p.reshape(1, DIFF_HD).astype(F32)
    lspec = pl.BlockSpec((1, DIFF_HD), lambda bi: (0, 0))
    new_spec = pl.BlockSpec((1, t, aw), lambda bi: (bi, 0, 0))
    past_spec = pl.BlockSpec((1, past, DIFF_HEADS, DIFF_VD), lambda bi: (bi, 0, 0, 0))
    return pl.pallas_call(
        functools.partial(_attn_sample_kernel, past=past, tq=t, lam_init=lam_init),
        grid=(b,),
        in_specs=[
            new_spec, past_spec, past_spec, new_spec, new_spec, lspec, lspec, lspec, lspec,
            pl.BlockSpec((1, DIFF_VD), lambda bi: (0, 0)),
        ],
        out_specs=new_spec,
        out_shape=jax.ShapeDtypeStruct((b, t, aw), F32),
        compiler_params=_params(("parallel",)),
        name="diff_attention_sample",
    )(q, past_k, past_v, k_new, v_new, *[lrow(p) for p in lam_params], subln_g.reshape(1, DIFF_VD).astype(F32))


def _cross_kernel(q_ref, mk_ref, mv_ref, out_ref, *, precise):
    d = q_ref.shape[-1]
    hd = d // X_HEADS
    per_head = len(mk_ref.shape) == 4
    for hh in range(X_HEADS):
        cols = slice(hh * hd, (hh + 1) * hd)
        mk = mk_ref[0, :, hh, :] if per_head else mk_ref[0, :, cols]
        mv = mv_ref[0, :, hh, :] if per_head else mv_ref[0, :, cols]
        s = _dot_nt(q_ref[0, :, cols], mk, precise) * (hd ** -0.5)
        p = jnp.exp(s - jnp.max(s, axis=-1, keepdims=True))
        p = p / jnp.sum(p, axis=-1, keepdims=True)
        out_ref[0, :, cols] = _dot(p, mv, precise).astype(out_ref.dtype)


def cross_attention_core(q, mk, mv, *, tq, precise):
    b, t, d = q.shape
    nm = mk.shape[1]
    tq = min(tq, t)
    q_spec = pl.BlockSpec((1, tq, d), lambda bi, qi: (bi, qi, 0))
    m_spec = pl.BlockSpec((1,) + mk.shape[1:], lambda bi, qi: (bi,) + (0,) * (mk.ndim - 1))
    return pl.pallas_call(
        functools.partial(_cross_kernel, precise=precise),
        grid=(b, t // tq),
        in_specs=[q_spec, m_spec, m_spec],
        out_specs=q_spec,
        out_shape=jax.ShapeDtypeStruct((b, t, d), F32 if precise else BF16),
        compiler_params=_params(("parallel", "arbitrary")),
        name="cross_attention_core",
    )(q, mk, mv)


def _first_argmax(vals, lane, valid):
    masked = jnp.where(valid, vals, -jnp.inf)
    mx = jnp.max(masked, axis=-1, keepdims=True)
    idx = jnp.min(jnp.where(masked == mx, lane, LANES), axis=-1, keepdims=True)
    return mx, idx


def _route(logits):
    lane = lax.broadcasted_iota(jnp.int32, logits.shape, 1).astype(F32)
    is_group = lane < N_GROUPS
    gmax, gidx = _first_argmax(logits, lane, is_group)
    gsum = jnp.sum(jnp.where(is_group, jnp.exp(logits - gmax), 0.0), axis=-1, keepdims=True)
    g_top = 1.0 / gsum
    lo = N_GROUPS + gidx * EXP_PER_GROUP
    in_group = (lane >= lo) & (lane < lo + EXP_PER_GROUP)
    e1, i1 = _first_argmax(logits, lane, in_group)
    e2, i2 = _first_argmax(logits, lane, in_group & (lane != i1))
    w2 = jnp.exp(e2 - e1)
    gate1 = g_top / (1.0 + w2)
    gate2 = g_top * w2 / (1.0 + w2)
    return i1 - N_GROUPS, i2 - N_GROUPS, gate1, gate2


def _combine_weights(logits):
    lane = lax.broadcasted_iota(jnp.int32, logits.shape, 1).astype(F32)
    x1, x2, gate1, gate2 = _route(logits)
    return jnp.where(lane == x1, gate1, 0.0) + jnp.where(lane == x2, gate2, 0.0)


def _moe_dense_kernel(x_ref, g_ref, wr_ref, wg_ref, wu_ref, wd_ref, fg_ref, out_ref, h_s, comb_s, acc_s,
                      *, precise_router):
    e = pl.program_id(1)

    @pl.when(e == 0)
    def _():
        h = _rms(x_ref[...], g_ref[...])
        h_s[...] = h.astype(BF16)
        comb_s[...] = _combine_weights(_dot(h, wr_ref[...], precise_router))
        acc_s[...] = x_ref[...]

    hb = h_s[...]
    act = jnp.dot(hb, wg_ref[0], preferred_element_type=F32)
    act = act * _sigmoid(act) * jnp.dot(hb, wu_ref[0], preferred_element_type=F32)
    y = jnp.dot(act.astype(BF16), wd_ref[0], preferred_element_type=F32)
    lane = lax.broadcasted_iota(jnp.int32, comb_s.shape, 1)
    ce = jnp.sum(jnp.where(lane == e, comb_s[...], 0.0), axis=-1, keepdims=True)
    acc_s[...] += ce * y

    @pl.when(e == pl.num_programs(1) - 1)
    def _():
        out_ref[...] = _rms(acc_s[...], fg_ref[...])


def moe_dense_final(x, g, w_router_pad, wg, wu, wd, final_g, *, tm, precise_router):
    t, d = x.shape
    tm = min(tm, t)
    ne, _, ff = wg.shape
    return pl.pallas_call(
        functools.partial(_moe_dense_kernel, precise_router=precise_router),
        grid=(t // tm, ne),
        in_specs=[
            pl.BlockSpec((tm, d), lambda i, e: (i, 0)),
            pl.BlockSpec((1, d), lambda i, e: (0, 0)),
            pl.BlockSpec((d, LANES), lambda i, e: (0, 0)),
            pl.BlockSpec((1, d, ff), lambda i, e: (e, 0, 0)),
            pl.BlockSpec((1, d, ff), lambda i, e: (e, 0, 0)),
            pl.BlockSpec((1, ff, d), lambda i, e: (e, 0, 0)),
            pl.BlockSpec((1, d), lambda i, e: (0, 0)),
        ],
        out_specs=pl.BlockSpec((tm, d), lambda i, e: (i, 0)),
        out_shape=jax.ShapeDtypeStruct((t, d), F32),
        scratch_shapes=[pltpu.VMEM((tm, d), BF16), pltpu.VMEM((tm, LANES), F32), pltpu.VMEM((tm, d), F32)],
        compiler_params=_params(("parallel", "arbitrary")),
        name="moe_dense_final",
    )(x, g.reshape(1, d), w_router_pad, wg, wu, wd, final_g.reshape(1, d))


ROW_UNIT = 16
ROUTE_TILE = MXU_DIM
EXPERT_TILE = 512
UNIT_BITS = (16, 8, 4, 2, 1)


def _sorted_cap(tr):
    rows = 2 * tr + N_EXPERTS * (ROW_UNIT - 1)
    return -(-rows // MXU_DIM) * MXU_DIM


def _chunk_dma(units, make_copy, wait):
    off = jnp.int32(0)
    for bit in UNIT_BITS:
        take = (units & bit) != 0

        @pl.when(take)
        def _(off=off, bit=bit):
            cp = make_copy(off, bit)
            if wait:
                cp.wait()
            else:
                cp.start()

        off = off + jnp.where(take, bit, 0)


def _rows(unit_start, units):
    return pl.ds(pl.multiple_of(unit_start * ROW_UNIT, ROW_UNIT), units * ROW_UNIT)


def _moe_route_kernel(x_ref, g_ref, wr_ref, xs_ref, info_ref, tab_ref, tot_ref, xc, run, sem, *, tr, cap, seg_units):
    i = pl.program_id(0)
    nt = pl.num_programs(0)

    @pl.when(i == 0)
    def _():
        for e in range(N_EXPERTS):
            run[e] = 0

    hb = _rms(x_ref[...], g_ref[...]).astype(BF16)
    e1, e2, g1, g2 = _route(jnp.dot(hb, wr_ref[...], preferred_element_type=F32))
    lane = lax.broadcasted_iota(jnp.int32, (tr, LANES), 1).astype(F32)
    a1 = lane == e1
    a2 = lane == e2
    assigned = jnp.where(a1 | a2, 1.0, 0.0)
    earlier = lax.broadcasted_iota(jnp.int32, (tr, tr), 1) < lax.broadcasted_iota(jnp.int32, (tr, tr), 0)
    rank = jnp.dot(jnp.where(earlier, 1.0, 0.0).astype(BF16), assigned.astype(BF16), preferred_element_type=F32)
    count = jnp.sum(assigned, axis=0, keepdims=True)
    units = jnp.floor((count + (ROW_UNIT - 1)) * (1.0 / ROW_UNIT))
    before = lax.broadcasted_iota(jnp.int32, (LANES, LANES), 0) < lax.broadcasted_iota(jnp.int32, (LANES, LANES), 1)
    units8 = jnp.broadcast_to(units, (SUBLANES, LANES)).astype(BF16)
    base = ROW_UNIT * jnp.dot(units8, jnp.where(before, 1.0, 0.0).astype(BF16), preferred_element_type=F32)[0:1]
    slot = base + rank
    slot1 = jnp.sum(jnp.where(a1, slot, 0.0), axis=1, keepdims=True)
    slot2 = jnp.sum(jnp.where(a2, slot, 0.0), axis=1, keepdims=True)
    info_ref[...] = jnp.where(lane == 0, slot1, jnp.where(lane == 1, slot2,
                              jnp.where(lane == 2, g1, jnp.where(lane == 3, g2, 0.0))))
    pos = lax.broadcasted_iota(jnp.int32, (tr, cap), 1).astype(F32)
    onehot_t = jnp.where((pos == slot1) | (pos == slot2), 1.0, 0.0).astype(BF16)
    xc[i % 2] = lax.dot_general(onehot_t, hb, (((0,), (0,)), ((), ())), preferred_element_type=F32).astype(BF16)

    def copies(tile, wait):
        buf = tile % 2
        src = jnp.int32(0)
        for e in range(N_EXPERTS):
            ne = tab_ref[tile * 2 * N_EXPERTS + N_EXPERTS + e]
            dst = e * seg_units + tab_ref[tile * 2 * N_EXPERTS + e]
            _chunk_dma(ne, lambda off, bit, src=src, dst=dst: pltpu.make_async_copy(
                xc.at[buf].at[_rows(src + off, bit)], xs_ref.at[_rows(dst + off, bit)], sem.at[buf]), wait)
            src = src + ne

    for e in range(N_EXPERTS):
        ne = units[0, e].astype(jnp.int32)
        tab_ref[i * 2 * N_EXPERTS + e] = run[e]
        tab_ref[i * 2 * N_EXPERTS + N_EXPERTS + e] = ne
        run[e] = run[e] + ne
    copies(i, False)

    @pl.when(i > 0)
    def _():
        copies(i - 1, True)

    @pl.when(i == nt - 1)
    def _():
        copies(i, True)
        fill = EXPERT_TILE // ROW_UNIT
        xc[0, pl.ds(0, EXPERT_TILE), :] = jnp.zeros((EXPERT_TILE, xc.shape[2]), BF16)
        tails = [pltpu.make_async_copy(xc.at[0].at[_rows(0, fill)],
                                       xs_ref.at[_rows(e * seg_units + run[e], fill)], sem.at[0])
                 for e in range(N_EXPERTS)]
        for cp in tails:
            cp.start()
        for cp in tails:
            cp.wait()
        for e in range(N_EXPERTS):
            tot_ref[e] = run[e]


def _moe_expert_kernel(eo_ref, rb_ref, valid_ref, xs_ref, wg_ref, wu_ref, wd_ref, ys_ref):
    @pl.when(valid_ref[pl.program_id(0)] == 1)
    def _():
        x = xs_ref[...]
        act = jnp.dot(x, wg_ref[0], preferred_element_type=F32)
        act = act * _sigmoid(act) * jnp.dot(x, wu_ref[0], preferred_element_type=F32)
        ys_ref[...] = jnp.dot(act.astype(BF16), wd_ref[0], preferred_element_type=F32).astype(ys_ref.dtype)


def _moe_combine_kernel(tab_ref, x_ref, info_ref, ys_ref, fg_ref, out_ref, yc, sem, *, tr, cap, seg_units):
    i = pl.program_id(0)

    nt = pl.num_programs(0)

    def copies(tile, wait):
        buf = tile % 2
        dst = jnp.int32(0)
        for e in range(N_EXPERTS):
            ne = tab_ref[tile * 2 * N_EXPERTS + N_EXPERTS + e]
            src = e * seg_units + tab_ref[tile * 2 * N_EXPERTS + e]
            _chunk_dma(ne, lambda off, bit, src=src, dst=dst: pltpu.make_async_copy(
                ys_ref.at[_rows(src + off, bit)], yc.at[buf].at[_rows(dst + off, bit)], sem.at[buf]), wait)
            dst = dst + ne

    @pl.when(i == 0)
    def _():
        yc[...] = jnp.zeros(yc.shape, yc.dtype)
        copies(i, False)

    @pl.when(i + 1 < nt)
    def _():
        copies(i + 1, False)

    copies(i, True)
    info = info_ref[...]
    pos = lax.broadcasted_iota(jnp.int32, (tr, cap), 1).astype(F32)
    rows = yc[i % 2]
    y1 = jnp.dot(jnp.where(pos == info[:, 0:1], 1.0, 0.0).astype(BF16), rows, preferred_element_type=F32)
    y2 = jnp.dot(jnp.where(pos == info[:, 1:2], 1.0, 0.0).astype(BF16), rows, preferred_element_type=F32)
    out_ref[...] = _rms(x_ref[...] + info[:, 2:3] * y1 + info[:, 3:4] * y2, fg_ref[...])


def moe_sparse_final(x, g, w_router_pad, wg, wu, wd, final_g):
    t, d = x.shape
    tr, te = ROUTE_TILE, EXPERT_TILE
    assert t % tr == 0
    ntiles = t // tr
    cap = _sorted_cap(tr)
    ne, _, ff = wg.shape
    seg_rows = -(-(t + (ROW_UNIT - 1) * ntiles + te) // te) * te
    seg_units = seg_rows // ROW_UNIT
    smem = pl.BlockSpec(memory_space=pltpu.SMEM)

    xs, info, tab, tot = pl.pallas_call(
        functools.partial(_moe_route_kernel, tr=tr, cap=cap, seg_units=seg_units),
        grid=(ntiles,),
        in_specs=[
            pl.BlockSpec((tr, d), lambda i: (i, 0)),
            pl.BlockSpec((1, d), lambda i: (0, 0)),
            pl.BlockSpec((d, LANES), lambda i: (0, 0)),
        ],
        out_specs=[pl.BlockSpec(memory_space=pl.ANY), pl.BlockSpec((tr, LANES), lambda i: (i, 0)), smem, smem],
        out_shape=[
            jax.ShapeDtypeStruct((ne * seg_rows, d), BF16),
            jax.ShapeDtypeStruct((t, LANES), F32),
            jax.ShapeDtypeStruct((ntiles * 2 * ne,), jnp.int32),
            jax.ShapeDtypeStruct((ne,), jnp.int32),
        ],
        scratch_shapes=[pltpu.VMEM((2, cap, d), BF16), pltpu.SMEM((ne,), jnp.int32), pltpu.SemaphoreType.DMA((2,))],
        compiler_params=_params(("arbitrary",)),
        name="moe_route",
    )(x, g.reshape(1, d), w_router_pad)

    tiles_per_e = (tot * ROW_UNIT + te - 1) // te
    ends = jnp.cumsum(tiles_per_e)
    n_items = ends[-1]
    max_items = (2 * t + ne * (ROW_UNIT - 1) * ntiles) // te + ne
    w = jnp.arange(max_items, dtype=jnp.int32)
    wc = jnp.minimum(w, n_items - 1)
    eo = jnp.sum((wc[:, None] >= ends[None, :]).astype(jnp.int32), axis=1)
    rb = (eo * (seg_rows // te) + wc - (ends - tiles_per_e)[eo]).astype(jnp.int32)
    valid = (w < n_items).astype(jnp.int32)

    ys = pl.pallas_call(
        _moe_expert_kernel,
        grid_spec=pltpu.PrefetchScalarGridSpec(
            num_scalar_prefetch=3,
            grid=(max_items,),
            in_specs=[
                pl.BlockSpec((te, d), lambda w, eo, rb, va: (rb[w], 0)),
                pl.BlockSpec((1, d, ff), lambda w, eo, rb, va: (eo[w], 0, 0)),
                pl.BlockSpec((1, d, ff), lambda w, eo, rb, va: (eo[w], 0, 0)),
                pl.BlockSpec((1, ff, d), lambda w, eo, rb, va: (eo[w], 0, 0)),
            ],
            out_specs=pl.BlockSpec((te, d), lambda w, eo, rb, va: (rb[w], 0)),
        ),
        out_shape=jax.ShapeDtypeStruct((ne * seg_rows, d), BF16),
        compiler_params=_params(("arbitrary",)),
        name="moe_experts",
    )(eo, rb, valid, xs, wg, wu, wd)

    return pl.pallas_call(
        functools.partial(_moe_combine_kernel, tr=tr, cap=cap, seg_units=seg_units),
        grid_spec=pltpu.PrefetchScalarGridSpec(
            num_scalar_prefetch=1,
            grid=(ntiles,),
            in_specs=[
                pl.BlockSpec((tr, d), lambda i, tab: (i, 0)),
                pl.BlockSpec((tr, LANES), lambda i, tab: (i, 0)),
                pl.BlockSpec(memory_space=pl.ANY),
                pl.BlockSpec((1, d), lambda i, tab: (0, 0)),
            ],
            out_specs=pl.BlockSpec((tr, d), lambda i, tab: (i, 0)),
            scratch_shapes=[pltpu.VMEM((2, cap, d), BF16), pltpu.SemaphoreType.DMA((2,))],
        ),
        out_shape=jax.ShapeDtypeStruct((t, d), F32),
        compiler_params=_params(("arbitrary",)),
        name="moe_combine",
    )(tab, x, info, ys, final_g.reshape(1, d))


def _trunk(x, mem_k, mem_v, conv_buf, h0, past_k, past_v, p, lam_init):
    b, t, d = x.shape
    n = b * t
    xf = x.reshape(n, d)
    aw = DIFF_HEADS * DIFF_VD
    lru_w = p["lru_lambda"].shape[0]
    precise = past_k is not None
    tm = n if precise else 512
    tn = 1024
    assert t >= CONV_W - 1
    seq = lambda a: a.reshape(b, t, a.shape[-1])
    lam_params = (p["lam_q1"], p["lam_k1"], p["lam_q2"], p["lam_k2"])
    lru_args = (p["conv_w"], p["conv_b"], p["lru_wa"], p["lru_ba"].reshape(-1), p["lru_wx"],
                p["lru_bx"].reshape(-1), p["lru_lambda"])

    if precise:
        xb, gate, q, k, v = norm_linear_f32(xf, p["norm_mix_g"], p["w_in_f32"], tn=tn, split=True)
    else:
        xb, gate, q, k, v = norm_linear(xf, p["norm_mix_g"], p["w_in"], tm=tm, tn=tn,
                                        out_widths=[lru_w, lru_w, aw, aw, aw], out_dtypes=[F32] * 5)
    lru_out, h_last = lru_mixer(seq(xb), seq(gate), conv_buf, h0, *lru_args, tc=256, precise=precise)
    if precise:
        att = diff_attention_sample(seq(q), seq(k), seq(v), past_k, past_v, lam_params, p["subln_g"], lam_init)
    else:
        att = diff_attention_prompt(seq(q), seq(k), seq(v), lam_params, p["subln_g"], lam_init, tq=512)
    mix_in = [lru_out.reshape(n, lru_w), att.reshape(n, aw)]
    if precise:
        x1 = linear_residual_f32(xf, mix_in, p["w_out_f32"], tn=tn)
        (qx,) = norm_linear_f32(x1, p["norm_cross_g"], p["xq_w_f32"], tn=tn, split=False)
    else:
        x1 = linear_residual(xf, mix_in, p["w_out"], tm=tm, tn=tn)
        (qx,) = norm_linear(x1, p["norm_cross_g"], p["xq_w"], tm=tm, tn=tn, out_widths=[d], out_dtypes=[BF16])
    o = cross_attention_core(seq(qx), mem_k, mem_v, tq=512, precise=precise)
    if precise:
        x2 = linear_residual_f32(x1, [o.reshape(n, d)], p["xo_w_f32"], tn=tn)
    else:
        x2 = linear_residual(x1, [o.reshape(n, d)], p["xo_w"], tm=tm, tn=tn)

    experts = (p["exp_gate"], p["exp_up"], p["exp_down"])
    if precise:
        y = moe_dense_final(x2, p["norm_ffn_g"], p["router_pad_f32"], *experts, p["final_norm_g"], tm=tm,
                            precise_router=True)
    else:
        y = moe_sparse_final(x2, p["norm_ffn_g"], p["router_pad"], *experts, p["final_norm_g"])
    new_conv = seq(xb)[:, t - (CONV_W - 1):, :]
    return y.reshape(b, t, d), new_conv, h_last, k, v


def kernel(x_prompt, x_sample, cache_diff_k, cache_diff_v, cache_mem_k, cache_mem_v, state_conv, state_lru, mem_prompt, norm_mix_g, w_in, conv_w, conv_b, lru_wa, lru_ba, lru_wx, lru_bx, lru_lambda, lam_q1, lam_k1, lam_q2, lam_k2, subln_g, w_out, norm_cross_g, norm_mem_g, xq_w, xk_w, xv_w, xo_w, norm_ffn_g, router_group_w, router_expert_w, exp_gate, exp_up, exp_down, final_norm_g):
    depth = w_in.shape[0]
    assert depth == 1, "single-layer step"
    bp, tp, d = x_prompt.shape
    bs, ts, _ = x_sample.shape
    past = cache_diff_k.shape[2]
    n_mem = mem_prompt.shape[1]
    aw = DIFF_HEADS * DIFF_VD
    l = 0
    lam_init = 0.8 - 0.6 * math.exp(-0.3 * l)

    router = jnp.concatenate([router_group_w[l], router_expert_w[l]], axis=1)
    router_pad_f32 = jnp.pad(router, ((0, 0), (0, LANES - router.shape[1])))
    router_pad = router_pad_f32.astype(BF16)
    p = dict(router_pad_f32=router_pad_f32, w_in_f32=w_in[l], w_out_f32=w_out[l], xq_w_f32=xq_w[l],
             xo_w_f32=xo_w[l], **dict(norm_mix_g=norm_mix_g[l], conv_w=conv_w[l], conv_b=conv_b[l], lru_wa=lru_wa[l],
             lru_ba=lru_ba[l], lru_wx=lru_wx[l], lru_bx=lru_bx[l], lru_lambda=lru_lambda[l], lam_q1=lam_q1[l],
             lam_k1=lam_k1[l], lam_q2=lam_q2[l], lam_k2=lam_k2[l], subln_g=subln_g[l],
             norm_cross_g=norm_cross_g[l], norm_ffn_g=norm_ffn_g[l], router_pad=router_pad,
             final_norm_g=final_norm_g))
    for name, w in (("w_in", w_in), ("w_out", w_out), ("xq_w", xq_w), ("xo_w", xo_w), ("exp_gate", exp_gate),
                    ("exp_up", exp_up), ("exp_down", exp_down)):
        p[name] = w[l].astype(BF16)

    memf = mem_prompt.reshape(bp * n_mem, d)
    w_mem = jnp.concatenate([xk_w[l].astype(BF16), xv_w[l].astype(BF16)], axis=1)
    mk_p, mv_p = norm_linear(memf, norm_mem_g[l], w_mem, tm=512, tn=1024, out_widths=[d, d], out_dtypes=[F32, F32])
    mk_p = mk_p.reshape(bp, n_mem, d)
    mv_p = mv_p.reshape(bp, n_mem, d)

    zero_buf = jnp.zeros((bp, CONV_W - 1, lru_lambda.shape[1]), F32)
    zero_h = jnp.zeros((bp, lru_lambda.shape[1]), F32)
    y_p, cb_p, hl_p, k_p, v_p = _trunk(x_prompt, mk_p, mv_p, zero_buf, zero_h, None, None, p, lam_init)
    y_s, cb_s, hl_s, k_s, v_s = _trunk(x_sample, cache_mem_k[l], cache_mem_v[l], state_conv[l], state_lru[l],
                                       cache_diff_k[l], cache_diff_v[l],
                                       p, lam_init)

    hd2 = 2 * DIFF_HD
    return (y_p, y_s,
            k_p.reshape(1, bp, tp, DIFF_HEADS, hd2), v_p.reshape(1, bp, tp, DIFF_HEADS, DIFF_VD),
            mk_p.reshape(1, bp, n_mem, X_HEADS, d // X_HEADS), mv_p.reshape(1, bp, n_mem, X_HEADS, d // X_HEADS),
            cb_p[None], hl_p[None],
            k_s.reshape(1, bs, ts, DIFF_HEADS, hd2), v_s.reshape(1, bs, ts, DIFF_HEADS, DIFF_VD),
            cb_s[None], hl_s[None].astype(state_lru.dtype))
```

```python
import functools
import math

import jax
import jax.numpy as jnp
from jax import lax
from jax.experimental import pallas as pl
from jax.experimental.pallas import tpu as pltpu

F32 = jnp.float32
BF16 = jnp.bfloat16

CHUNK = 64
CONV_W = 4
LRU_C = 8.0
LRU_BLOCK = 64
DIFF_HEADS = 8
DIFF_HD = 64
DIFF_VD = 2 * DIFF_HD
X_HEADS = 4
N_GROUPS = 4
EXP_PER_GROUP = 4
N_EXPERTS = N_GROUPS * EXP_PER_GROUP
EPS = 1e-6

LANES = 128
SUBLANES = 8
MXU_DIM = 256
VMEM_LIMIT_BYTES = 56 * 1024 * 1024


def _params(semantics):
    return pltpu.CompilerParams(dimension_semantics=semantics, vmem_limit_bytes=VMEM_LIMIT_BYTES)


def _split(a):
    a = a.astype(F32)
    hi = a.astype(BF16)
    return hi, (a - hi.astype(F32)).astype(BF16)


def _dot_dims(a, b, dims, precise):
    if not precise:
        return lax.dot_general(a.astype(BF16), b.astype(BF16), dims, preferred_element_type=F32)
    if precise == "native":
        return lax.dot_general(a.astype(F32), b.astype(F32), dims, precision=lax.Precision.HIGHEST,
                               preferred_element_type=F32)
    ah, al = _split(a)
    bh, bl = _split(b)
    mm = lambda x, y: lax.dot_general(x, y, dims, preferred_element_type=F32)
    return mm(ah, bh) + (mm(ah, bl) + mm(al, bh))


def _dot(a, b, precise=False):
    return _dot_dims(a, b, (((1,), (0,)), ((), ())), precise)


def _dot_nt(a, b, precise=False):
    return _dot_dims(a, b, (((1,), (1,)), ((), ())), precise)


def _rms(x, g):
    return x * lax.rsqrt(jnp.mean(x * x, axis=-1, keepdims=True) + EPS) * g


def _sigmoid(x):
    return 1.0 / (1.0 + jnp.exp(-x))


def _gelu_tanh(x):
    c = math.sqrt(2.0 / math.pi)
    return 0.5 * x * (1.0 + jnp.tanh(c * (x + 0.044715 * (x * x * x))))


def _norm_linear_kernel(x_ref, g_ref, w_ref, *out_refs, tn):
    h = _rms(x_ref[...], g_ref[...]).astype(BF16)
    col = 0
    for o_ref in out_refs:
        for c in range(o_ref.shape[1] // tn):
            o_ref[:, c * tn:(c + 1) * tn] = jnp.dot(
                h, w_ref[:, col:col + tn], preferred_element_type=F32).astype(o_ref.dtype)
            col += tn


def norm_linear(x, g, w, *, tm, tn, out_widths, out_dtypes):
    t, k = x.shape
    n = w.shape[1]
    tm = min(tm, t)
    assert sum(out_widths) == n and all(wd % tn == 0 for wd in out_widths)
    return pl.pallas_call(
        functools.partial(_norm_linear_kernel, tn=tn),
        grid=(t // tm,),
        in_specs=[
            pl.BlockSpec((tm, k), lambda i: (i, 0)),
            pl.BlockSpec((1, k), lambda i: (0, 0)),
            pl.BlockSpec((k, n), lambda i: (0, 0)),
        ],
        out_specs=[pl.BlockSpec((tm, wd), lambda i: (i, 0)) for wd in out_widths],
        out_shape=[jax.ShapeDtypeStruct((t, wd), dt) for wd, dt in zip(out_widths, out_dtypes)],
        compiler_params=_params(("parallel",)),
        name="norm_linear",
    )(x, g.reshape(1, k), w)


def _linear_res_kernel(*refs, n_in, tn):
    res_ref = refs[0]
    a_refs = refs[1:1 + n_in]
    w_ref = refs[1 + n_in]
    out_ref = refs[2 + n_in]
    kc = a_refs[0].shape[1]
    for c in range(out_ref.shape[1] // tn):
        cols = slice(c * tn, (c + 1) * tn)
        acc = res_ref[:, cols]
        for r, a_ref in enumerate(a_refs):
            acc = acc + _dot(a_ref[...], w_ref[r * kc:(r + 1) * kc, cols])
        out_ref[:, cols] = acc


def linear_residual(res, a_list, w, *, tm, tn):
    t, n = res.shape
    tm = min(tm, t)
    n_in = len(a_list)
    kc = a_list[0].shape[1]
    in_specs = [pl.BlockSpec((tm, n), lambda i: (i, 0))]
    in_specs += [pl.BlockSpec((tm, kc), lambda i: (i, 0)) for _ in range(n_in)]
    in_specs += [pl.BlockSpec(w.shape, lambda i: (0, 0))]
    return pl.pallas_call(
        functools.partial(_linear_res_kernel, n_in=n_in, tn=tn),
        grid=(t // tm,),
        in_specs=in_specs,
        out_specs=pl.BlockSpec((tm, n), lambda i: (i, 0)),
        out_shape=jax.ShapeDtypeStruct((t, n), F32),
        compiler_params=_params(("parallel",)),
        name="linear_residual",
    )(res, *a_list, w)


def _norm_linear_f32_kernel(x_ref, g_ref, w_ref, *refs):
    out_refs, h_ref = refs[:-1], refs[-1]
    j = pl.program_id(1)

    @pl.when(j == 0)
    def _():
        h_ref[...] = _rms(x_ref[...], g_ref[...])

    if len(out_refs) == 1:
        out_refs[0][...] = _dot(h_ref[...], w_ref[...], True)
    else:
        for c, o_ref in enumerate(out_refs):
            @pl.when(j == c)
            def _(o_ref=o_ref):
                o_ref[...] = _dot(h_ref[...], w_ref[...], True)


def norm_linear_f32(x, g, w, *, tn, split):
    t, k = x.shape
    n = w.shape[1]
    nj = n // tn
    if split:
        out_shape = [jax.ShapeDtypeStruct((t, tn), F32) for _ in range(nj)]
        out_specs = [pl.BlockSpec((t, tn), lambda i, j: (i, 0)) for _ in range(nj)]
    else:
        out_shape = [jax.ShapeDtypeStruct((t, n), F32)]
        out_specs = [pl.BlockSpec((t, tn), lambda i, j: (i, j))]
    return pl.pallas_call(
        _norm_linear_f32_kernel,
        grid=(1, nj),
        in_specs=[
            pl.BlockSpec((t, k), lambda i, j: (i, 0)),
            pl.BlockSpec((1, k), lambda i, j: (0, 0)),
            pl.BlockSpec((k, tn), lambda i, j: (0, j)),
        ],
        out_specs=out_specs,
        out_shape=out_shape,
        scratch_shapes=[pltpu.VMEM((t, k), F32)],
        compiler_params=_params(("parallel", "arbitrary")),
        name="norm_linear_f32",
    )(x, g.reshape(1, k), w)


def _linear_res_f32_kernel(*refs, n_in):
    res_ref = refs[0]
    a_refs = refs[1:1 + n_in]
    w_refs = refs[1 + n_in:1 + 2 * n_in]
    out_ref = refs[1 + 2 * n_in]
    acc = res_ref[...]
    for a_ref, w_ref in zip(a_refs, w_refs):
        acc = acc + _dot(a_ref[...], w_ref[...], True)
    out_ref[...] = acc


def linear_residual_f32(res, a_list, w, *, tn):
    t, n = res.shape
    n_in = len(a_list)
    kc = a_list[0].shape[1]
    in_specs = [pl.BlockSpec((t, tn), lambda i, j: (i, j))]
    in_specs += [pl.BlockSpec((t, kc), lambda i, j: (i, 0)) for _ in range(n_in)]
    in_specs += [pl.BlockSpec((kc, tn), lambda i, j, c=c: (c, j)) for c in range(n_in)]
    return pl.pallas_call(
        functools.partial(_linear_res_f32_kernel, n_in=n_in),
        grid=(1, n // tn),
        in_specs=in_specs,
        out_specs=pl.BlockSpec((t, tn), lambda i, j: (i, j)),
        out_shape=jax.ShapeDtypeStruct((t, n), F32),
        compiler_params=_params(("parallel", "arbitrary")),
        name="linear_residual_f32",
    )(res, *a_list, *([w] * n_in))


def _lru_kernel(xb_ref, gate_ref, cbuf_ref, h0_ref, cw_ref, cb_ref, wa_ref, ba_ref, wx_ref, bx_ref, lam_ref,
                out_ref, hlast_ref, xpad, hcar, a_s, u_s, *, tc, width, precise):
    c = pl.program_id(1)
    nslab = width // MXU_DIM
    ngrp = tc // SUBLANES

    @pl.when(c == 0)
    def _():
        xpad[pl.ds(0, SUBLANES), :] = cbuf_ref[0]
        hcar[...] = h0_ref[0]

    xpad[pl.ds(SUBLANES, tc), :] = xb_ref[0]
    xc = cb_ref[...] + cw_ref[pl.ds(CONV_W - 1, 1), :] * xpad[pl.ds(SUBLANES, tc), :]
    for j in range(CONV_W - 1):
        xc = xc + cw_ref[pl.ds(j, 1), :] * xpad[pl.ds(SUBLANES - (CONV_W - 1) + j, tc), :]
    xpad[pl.ds(0, SUBLANES), :] = xpad[pl.ds(tc, SUBLANES), :]

    lam = lam_ref[...]
    softplus_neg = jnp.maximum(-lam, 0.0) + jnp.log1p(jnp.exp(-jnp.abs(lam)))
    c8 = -LRU_C * softplus_neg

    sub = lax.broadcasted_iota(jnp.int32, (ngrp, SUBLANES, MXU_DIM), 1)
    for s in range(nslab):
        cols = slice(s * MXU_DIM, (s + 1) * MXU_DIM)
        xs = xc[:, cols]
        r = _sigmoid(_dot(xs, wa_ref[s], precise) + ba_ref[:, cols])
        i = _sigmoid(_dot(xs, wx_ref[s], precise) + bx_ref[:, cols])
        a = jnp.exp(c8[:, cols] * r)
        u = jnp.sqrt(1.0 - a * a) * (i * xs)
        a3 = a.reshape(ngrp, SUBLANES, MXU_DIM)
        u3 = u.reshape(ngrp, SUBLANES, MXU_DIM)
        d = 1
        while d < SUBLANES:
            a_sh = pltpu.roll(a3, d, 1)
            u_sh = pltpu.roll(u3, d, 1)
            keep = sub >= d
            u3 = jnp.where(keep, u3 + a3 * u_sh, u3)
            a3 = jnp.where(keep, a3 * a_sh, a3)
            d *= 2
        a_s[:, cols] = a3.reshape(tc, MXU_DIM)
        u_s[:, cols] = u3.reshape(tc, MXU_DIM)

    def body(g, hin):
        rows = pl.ds(pl.multiple_of(g * SUBLANES, SUBLANES), SUBLANES)
        h = u_s[rows, :] + a_s[rows, :] * hin
        u_s[rows, :] = h
        return h[SUBLANES - 1:SUBLANES, :]

    hfin = lax.fori_loop(0, ngrp, body, hcar[...])
    hcar[...] = hfin
    out_ref[0] = (u_s[...] * _gelu_tanh(gate_ref[0])).astype(out_ref.dtype)
    hlast_ref[0] = hfin


def _block_diag(w, per):
    nb, k, _ = w.shape
    w4 = w.reshape(nb // per, per, k, k)
    eye = jnp.eye(per, dtype=w.dtype)
    return jnp.einsum("cipq,ij->cipjq", w4, eye).reshape(nb // per, per * k, per * k)


def lru_mixer(xb, gate, conv_buf, h0, conv_w, conv_b, wa, ba, wx, bx, lam, *, tc, precise):
    b, t, width = xb.shape
    tc = min(tc, t)
    per = MXU_DIM // LRU_BLOCK
    act_dt = F32 if precise else BF16
    wa_bd = _block_diag(wa, per).astype(act_dt)
    wx_bd = _block_diag(wx, per).astype(act_dt)
    nslab = wa_bd.shape[0]
    cbuf8 = jnp.concatenate([jnp.zeros((b, SUBLANES - (CONV_W - 1), width), F32), conv_buf.astype(F32)], axis=1)
    row = lambda v: v.reshape(1, width).astype(F32)
    vec_spec = pl.BlockSpec((1, width), lambda bi, ci: (0, 0))
    seq_spec = pl.BlockSpec((1, tc, width), lambda bi, ci: (bi, ci, 0))
    wspec = pl.BlockSpec((nslab, MXU_DIM, MXU_DIM), lambda bi, ci: (0, 0, 0))
    out, hlast = pl.pallas_call(
        functools.partial(_lru_kernel, tc=tc, width=width, precise=precise),
        grid=(b, t // tc),
        in_specs=[
            seq_spec, seq_spec,
            pl.BlockSpec((1, SUBLANES, width), lambda bi, ci: (bi, 0, 0)),
            pl.BlockSpec((1, 1, width), lambda bi, ci: (bi, 0, 0)),
            pl.BlockSpec((CONV_W, width), lambda bi, ci: (0, 0)),
            vec_spec, wspec, vec_spec, wspec, vec_spec, vec_spec,
        ],
        out_specs=[seq_spec, pl.BlockSpec((1, 1, width), lambda bi, ci: (bi, 0, 0))],
        out_shape=[jax.ShapeDtypeStruct((b, t, width), act_dt), jax.ShapeDtypeStruct((b, 1, width), F32)],
        scratch_shapes=[
            pltpu.VMEM((tc + SUBLANES, width), F32),
            pltpu.VMEM((1, width), F32),
            pltpu.VMEM((tc, width), F32),
            pltpu.VMEM((tc, width), F32),
        ],
        compiler_params=_params(("parallel", "arbitrary")),
        name="lru_mixer",
    )(xb, gate, cbuf8, h0.reshape(b, 1, width).astype(F32), conv_w.astype(F32), row(conv_b), wa_bd,
      row(ba), wx_bd, row(bx), row(lam))
    return out, hlast.reshape(b, width)


def _diff_lambda(lq1, lk1, lq2, lk2, lam_init):
    s1 = jnp.sum(lq1[...] * lk1[...], axis=-1, keepdims=True)
    s2 = jnp.sum(lq2[...] * lk2[...], axis=-1, keepdims=True)
    return jnp.exp(s1) - jnp.exp(s2) + lam_init


_LOG2E_PARTS = (1.4453125, -0.00262451171875, 7.063150405883789e-06)
LOG2E = sum(_LOG2E_PARTS)
N_BIAS_COLS = 2 * len(_LOG2E_PARTS)


def _attn_prompt_kernel(slopes_ref, q_ref, k_ref, v_ref, lq1, lk1, lq2, lk2, subln_ref, out_ref,
                        kaug, vt, qt, m_s, l_s, acc_s, s_s, *, tq, lam_init):
    h = pl.program_id(1)
    qi = pl.program_id(2)
    slope = slopes_ref[h]
    nblk, tk, _ = kaug.shape
    nparts = len(_LOG2E_PARTS)

    @pl.when(qi == 0)
    def _():
        pos = lax.broadcasted_iota(jnp.int32, (tk, LANES), 0)
        lane = lax.broadcasted_iota(jnp.int32, (tk, LANES), 1)
        within = (pos % CHUNK).astype(F32) * slope
        for j in range(nblk):
            rows = slice(j * tk, (j + 1) * tk)
            kaug[j, :, 0:LANES] = k_ref[0, rows, :].astype(BF16)
            coarse = ((pos + j * tk) // CHUNK * CHUNK).astype(F32) * slope
            cols = jnp.where(lane < nparts, coarse, jnp.where(lane < N_BIAS_COLS, within, 0.0))
            kaug[j, :, LANES:2 * LANES] = cols.astype(BF16)
            vt[j] = v_ref[0, rows, :].T.astype(BF16)
        r = lax.broadcasted_iota(jnp.int32, (LANES, tq), 0)
        part = jnp.where(r % nparts == 0, _LOG2E_PARTS[0],
                         jnp.where(r % nparts == 1, _LOG2E_PARTS[1], _LOG2E_PARTS[2]))
        const_rows = jnp.where(r < N_BIAS_COLS, part, 0.0).astype(BF16)
        qt[0, LANES:2 * LANES, :] = const_rows
        qt[1, LANES:2 * LANES, :] = const_rows

    qs = q_ref[0] * (LOG2E * DIFF_HD ** -0.5)
    qlane = lax.broadcasted_iota(jnp.int32, qs.shape, 1)
    qt[0, 0:LANES, :] = jnp.where(qlane < DIFF_HD, qs, 0.0).T.astype(BF16)
    qt[1, 0:LANES, :] = jnp.where(qlane >= DIFF_HD, qs, 0.0).T.astype(BF16)
    m_s[...] = jnp.full(m_s.shape, -jnp.inf, F32)
    l_s[...] = jnp.zeros(l_s.shape, F32)
    acc_s[...] = jnp.zeros(acc_s.shape, F32)

    def blocks(kis, diagonal):
        if diagonal:
            kpos = lax.broadcasted_iota(jnp.int32, (tk, tq), 0)
            qpos = lax.broadcasted_iota(jnp.int32, (tk, tq), 1)
            ahead = (kpos - qpos).astype(F32)
            fix = jnp.where(kpos > qpos, (-2.0 * LOG2E) * slope * ahead, 0.0)
            allowed = (kpos // CHUNK) <= (qpos // CHUNK)
        slab = MXU_DIM
        colmax = [[None, None] for _ in kis]
        for j, ki in enumerate(kis):
            last = diagonal and j == len(kis) - 1
            for m in range(2):
                for r in range(tk // slab):
                    rows = slice(r * slab, (r + 1) * slab)
                    s = jnp.dot(kaug[ki, rows, :], qt[m], preferred_element_type=F32)
                    if last:
                        s = jnp.where(allowed[rows], s + fix[rows], -jnp.inf)
                    s_s[j, m, rows, :] = s
                    pm = jnp.max(s, axis=0, keepdims=True)
                    colmax[j][m] = pm if r == 0 else jnp.maximum(colmax[j][m], pm)
        for j, ki in enumerate(kis):
            vblk = vt[ki]
            for m in range(2):
                m_old = m_s[m]
                m_new = jnp.maximum(m_old, colmax[j][m])
                p = jnp.exp2(s_s[j, m] - m_new)
                alpha = jnp.exp2(m_old - m_new)
                l_s[m] = alpha * l_s[m] + jnp.sum(p, axis=0, keepdims=True)
                acc_s[m] = alpha * acc_s[m] + jnp.dot(vblk, p.astype(BF16), preferred_element_type=F32)
                m_s[m] = m_new

    def pair_body(j, carry):
        blocks([2 * j, 2 * j + 1], False)
        return carry

    lax.fori_loop(0, qi // 2, pair_body, 0)

    @pl.when(qi % 2 == 1)
    def _():
        blocks([qi - 1, qi], True)

    @pl.when(qi % 2 == 0)
    def _():
        blocks([qi], True)

    lam = _diff_lambda(lq1, lk1, lq2, lk2, lam_init)
    o = acc_s[0] / l_s[0] - lam * (acc_s[1] / l_s[1])
    o = o * lax.rsqrt(jnp.mean(o * o, axis=0, keepdims=True) + EPS) * subln_ref[...] * (1.0 - lam_init)
    out_ref[0] = o.T.astype(out_ref.dtype)


def _alibi_slopes():
    return 2.0 ** (-8.0 * jnp.arange(1, DIFF_HEADS + 1, dtype=F32) / DIFF_HEADS)


def diff_attention_prompt(q, k, v, lam_params, subln_g, lam_init, *, tq):
    b, t, aw = q.shape
    tq = min(tq, t)
    hd2 = 2 * DIFF_HD
    assert hd2 == LANES and DIFF_VD == LANES and tq % CHUNK == 0
    assert t // CHUNK <= 256, "chunk index must stay exact in bf16"
    lrow = lambda p: p.reshape(1, DIFF_HD).astype(F32)
    lspec = pl.BlockSpec((1, DIFF_HD), lambda bi, hi, qi: (0, 0))
    kv_spec = pl.BlockSpec((1, t, hd2), lambda bi, hi, qi: (bi, 0, hi))
    q_spec = pl.BlockSpec((1, tq, hd2), lambda bi, hi, qi: (bi, qi, hi))
    return pl.pallas_call(
        functools.partial(_attn_prompt_kernel, tq=tq, lam_init=lam_init),
        grid=(b, DIFF_HEADS, t // tq),
        in_specs=[
            pl.BlockSpec(memory_space=pltpu.SMEM),
            q_spec, kv_spec, kv_spec, lspec, lspec, lspec, lspec,
            pl.BlockSpec((DIFF_VD, 1), lambda bi, hi, qi: (0, 0)),
        ],
        out_specs=q_spec,
        out_shape=jax.ShapeDtypeStruct((b, t, aw), BF16),
        scratch_shapes=[
            pltpu.VMEM((t // tq, tq, 2 * LANES), BF16),
            pltpu.VMEM((t // tq, DIFF_VD, tq), BF16),
            pltpu.VMEM((2, 2 * LANES, tq), BF16),
            pltpu.VMEM((2, 1, tq), F32),
            pltpu.VMEM((2, 1, tq), F32),
            pltpu.VMEM((2, DIFF_VD, tq), F32),
            pltpu.VMEM((2, 2, tq, tq), F32),
        ],
        compiler_params=_params(("parallel", "parallel", "arbitrary")),
        name="diff_attention_prompt",
    )(_alibi_slopes(), q, k, v, *[lrow(p) for p in lam_params], subln_g.reshape(DIFF_VD, 1).astype(F32))


def _attn_sample_kernel(q_ref, kp_ref, vp_ref, kn_ref, vn_ref, lq1, lk1, lq2, lk2, subln_ref, out_ref,
                        *, past, tq, lam_init):
    hd2 = 2 * DIFF_HD

    def bias_mask(nk, k_off):
        qpos = past + lax.broadcasted_iota(jnp.int32, (tq, nk), 0)
        kpos = k_off + lax.broadcasted_iota(jnp.int32, (tq, nk), 1)
        dist = jnp.abs(qpos - kpos).astype(F32)
        allowed = (kpos // CHUNK) <= (qpos // CHUNK)
        return dist, allowed

    dist_p, ok_p = bias_mask(past, 0)
    dist_n, ok_n = bias_mask(tq, past)
    lam = _diff_lambda(lq1, lk1, lq2, lk2, lam_init)
    qlane = lax.broadcasted_iota(jnp.int32, (tq, hd2), 1)
    for h in range(DIFF_HEADS):
        slope = 2.0 ** (-8.0 * (h + 1) / DIFF_HEADS)
        cols = slice(h * hd2, (h + 1) * hd2)
        q = q_ref[0, :, cols]
        kp, vp = kp_ref[0, :, h, :], vp_ref[0, :, h, :]
        kn, vn = kn_ref[0, :, cols], vn_ref[0, :, cols]
        probs = []
        for m in range(2):
            qm = jnp.where((qlane >= DIFF_HD) == (m == 1), q, 0.0)
            sp = jnp.where(ok_p, _dot_nt(qm, kp, "native") * (DIFF_HD ** -0.5) - slope * dist_p, -jnp.inf)
            sn = jnp.where(ok_n, _dot_nt(qm, kn, "native") * (DIFF_HD ** -0.5) - slope * dist_n, -jnp.inf)
            mx = jnp.maximum(jnp.max(sp, axis=-1, keepdims=True), jnp.max(sn, axis=-1, keepdims=True))
            pp = jnp.exp(sp - mx)
            pn = jnp.exp(sn - mx)
            l = jnp.sum(pp, axis=-1, keepdims=True) + jnp.sum(pn, axis=-1, keepdims=True)
            probs.append((pp / l, pn / l))
        o = (_dot(probs[0][0] - lam * probs[1][0], vp, "native")
             + _dot(probs[0][1] - lam * probs[1][1], vn, "native"))
        out_ref[0, :, cols] = (o * lax.rsqrt(jnp.mean(o * o, axis=-1, keepdims=True) + EPS) * subln_ref[...]
                               * (1.0 - lam_init)).astype(out_ref.dtype)


def diff_attention_sample(q, k_new, v_new, past_k, past_v, lam_params, subln_g, lam_init):
    b, t, aw = q.shape
    past = past_k.shape[1]
    lrow = lambda p: p.reshape(1, DIFF_HD).astype(F32)
    lspec = pl.BlockSpec((1, DIFF_HD), lambda bi: (0, 0))
    new_spec = pl.BlockSpec((1, t, aw), lambda bi: (bi, 0, 0))
    past_spec = pl.BlockSpec((1, past, DIFF_HEADS, DIFF_VD), lambda bi: (bi, 0, 0, 0))
    return pl.pallas_call(
        functools.partial(_attn_sample_kernel, past=past, tq=t, lam_init=lam_init),
        grid=(b,),
        in_specs=[
            new_spec, past_spec, past_spec, new_spec, new_spec, lspec, lspec, lspec, lspec,
            pl.BlockSpec((1, DIFF_VD), lambda bi: (0, 0)),
        ],
        out_specs=new_spec,
        out_shape=jax.ShapeDtypeStruct((b, t, aw), F32),
        compiler_params=_params(("parallel",)),
        name="diff_attention_sample",
    )(q, past_k, past_v, k_new, v_new, *[lrow(p) for p in lam_params], subln_g.reshape(1, DIFF_VD).astype(F32))


def _cross_kernel(q_ref, mk_ref, mv_ref, out_ref, *, precise):
    d = q_ref.shape[-1]
    hd = d // X_HEADS
    per_head = len(mk_ref.shape) == 4
    for hh in range(X_HEADS):
        cols = slice(hh * hd, (hh + 1) * hd)
        mk = mk_ref[0, :, hh, :] if per_head else mk_ref[0, :, cols]
        mv = mv_ref[0, :, hh, :] if per_head else mv_ref[0, :, cols]
        s = _dot_nt(q_ref[0, :, cols], mk, precise) * (hd ** -0.5)
        p = jnp.exp(s - jnp.max(s, axis=-1, keepdims=True))
        p = p / jnp.sum(p, axis=-1, keepdims=True)
        out_ref[0, :, cols] = _dot(p, mv, precise).astype(out_ref.dtype)


def cross_attention_core(q, mk, mv, *, tq, precise):
    b, t, d = q.shape
    nm = mk.shape[1]
    tq = min(tq, t)
    q_spec = pl.BlockSpec((1, tq, d), lambda bi, qi: (bi, qi, 0))
    m_spec = pl.BlockSpec((1,) + mk.shape[1:], lambda bi, qi: (bi,) + (0,) * (mk.ndim - 1))
    return pl.pallas_call(
        functools.partial(_cross_kernel, precise=precise),
        grid=(b, t // tq),
        in_specs=[q_spec, m_spec, m_spec],
        out_specs=q_spec,
        out_shape=jax.ShapeDtypeStruct((b, t, d), F32 if precise else BF16),
        compiler_params=_params(("parallel", "arbitrary")),
        name="cross_attention_core",
    )(q, mk, mv)


def _first_argmax(vals, lane, valid):
    masked = jnp.where(valid, vals, -jnp.inf)
    mx = jnp.max(masked, axis=-1, keepdims=True)
    idx = jnp.min(jnp.where(masked == mx, lane, LANES), axis=-1, keepdims=True)
    return mx, idx


def _route(logits):
    lane = lax.broadcasted_iota(jnp.int32, logits.shape, 1).astype(F32)
    is_group = lane < N_GROUPS
    gmax, gidx = _first_argmax(logits, lane, is_group)
    gsum = jnp.sum(jnp.where(is_group, jnp.exp(logits - gmax), 0.0), axis=-1, keepdims=True)
    g_top = 1.0 / gsum
    lo = N_GROUPS + gidx * EXP_PER_GROUP
    in_group = (lane >= lo) & (lane < lo + EXP_PER_GROUP)
    e1, i1 = _first_argmax(logits, lane, in_group)
    e2, i2 = _first_argmax(logits, lane, in_group & (lane != i1))
    w2 = jnp.exp(e2 - e1)
    gate1 = g_top / (1.0 + w2)
    gate2 = g_top * w2 / (1.0 + w2)
    return i1 - N_GROUPS, i2 - N_GROUPS, gate1, gate2


def _combine_weights(logits):
    lane = lax.broadcasted_iota(jnp.int32, logits.shape, 1).astype(F32)
    x1, x2, gate1, gate2 = _route(logits)
    return jnp.where(lane == x1, gate1, 0.0) + jnp.where(lane == x2, gate2, 0.0)


def _moe_dense_kernel(x_ref, g_ref, wr_ref, wg_ref, wu_ref, wd_ref, fg_ref, out_ref, h_s, comb_s, acc_s,
                      *, precise_router):
    e = pl.program_id(1)

    @pl.when(e == 0)
    def _():
        h = _rms(x_ref[...], g_ref[...])
        h_s[...] = h.astype(BF16)
        comb_s[...] = _combine_weights(_dot(h, wr_ref[...], precise_router))
        acc_s[...] = x_ref[...]

    hb = h_s[...]
    act = _dot(hb, wg_ref[0])
    act = act * _sigmoid(act) * _dot(hb, wu_ref[0])
    y = _dot(act, wd_ref[0])
    lane = lax.broadcasted_iota(jnp.int32, comb_s.shape, 1)
    ce = jnp.sum(jnp.where(lane == e, comb_s[...], 0.0), axis=-1, keepdims=True)
    acc_s[...] += ce * y

    @pl.when(e == pl.num_programs(1) - 1)
    def _():
        out_ref[...] = _rms(acc_s[...], fg_ref[...])


def moe_dense_final(x, g, w_router_pad, wg, wu, wd, final_g, *, tm, precise_router):
    t, d = x.shape
    tm = min(tm, t)
    ne, _, ff = wg.shape
    return pl.pallas_call(
        functools.partial(_moe_dense_kernel, precise_router=precise_router),
        grid=(t // tm, ne),
        in_specs=[
            pl.BlockSpec((tm, d), lambda i, e: (i, 0)),
            pl.BlockSpec((1, d), lambda i, e: (0, 0)),
            pl.BlockSpec((d, LANES), lambda i, e: (0, 0)),
            pl.BlockSpec((1, d, ff), lambda i, e: (e, 0, 0)),
            pl.BlockSpec((1, d, ff), lambda i, e: (e, 0, 0)),
            pl.BlockSpec((1, ff, d), lambda i, e: (e, 0, 0)),
            pl.BlockSpec((1, d), lambda i, e: (0, 0)),
        ],
        out_specs=pl.BlockSpec((tm, d), lambda i, e: (i, 0)),
        out_shape=jax.ShapeDtypeStruct((t, d), F32),
        scratch_shapes=[pltpu.VMEM((tm, d), BF16), pltpu.VMEM((tm, LANES), F32), pltpu.VMEM((tm, d), F32)],
        compiler_params=_params(("parallel", "arbitrary")),
        name="moe_dense_final",
    )(x, g.reshape(1, d), w_router_pad, wg, wu, wd, final_g.reshape(1, d))


ROW_UNIT = 16
ROUTE_TILE = MXU_DIM
EXPERT_TILE = 512
UNIT_BITS = (16, 8, 4, 2, 1)


def _sorted_cap(tr):
    rows = 2 * tr + N_EXPERTS * (ROW_UNIT - 1)
    return -(-rows // MXU_DIM) * MXU_DIM


def _chunk_dma(units, make_copy, wait):
    off = jnp.int32(0)
    for bit in UNIT_BITS:
        take = (units & bit) != 0

        @pl.when(take)
        def _(off=off, bit=bit):
            cp = make_copy(off, bit)
            if wait:
                cp.wait()
            else:
                cp.start()

        off = off + jnp.where(take, bit, 0)


def _rows(unit_start, units):
    return pl.ds(pl.multiple_of(unit_start * ROW_UNIT, ROW_UNIT), units * ROW_UNIT)


def _moe_route_kernel(x_ref, g_ref, wr_ref, xs_ref, info_ref, tab_ref, tot_ref, xc, run, sem, *, tr, cap, seg_units):
    i = pl.program_id(0)
    nt = pl.num_programs(0)

    @pl.when(i == 0)
    def _():
        for e in range(N_EXPERTS):
            run[e] = 0

    hb = _rms(x_ref[...], g_ref[...]).astype(BF16)
    e1, e2, g1, g2 = _route(jnp.dot(hb, wr_ref[...], preferred_element_type=F32))
    lane = lax.broadcasted_iota(jnp.int32, (tr, LANES), 1).astype(F32)
    a1 = lane == e1
    a2 = lane == e2
    assigned = jnp.where(a1 | a2, 1.0, 0.0)
    earlier = lax.broadcasted_iota(jnp.int32, (tr, tr), 1) < lax.broadcasted_iota(jnp.int32, (tr, tr), 0)
    rank = jnp.dot(jnp.where(earlier, 1.0, 0.0).astype(BF16), assigned.astype(BF16), preferred_element_type=F32)
    count = jnp.sum(assigned, axis=0, keepdims=True)
    units = jnp.floor((count + (ROW_UNIT - 1)) * (1.0 / ROW_UNIT))
    before = lax.broadcasted_iota(jnp.int32, (LANES, LANES), 0) < lax.broadcasted_iota(jnp.int32, (LANES, LANES), 1)
    units8 = jnp.broadcast_to(units, (SUBLANES, LANES)).astype(BF16)
    base = ROW_UNIT * jnp.dot(units8, jnp.where(before, 1.0, 0.0).astype(BF16), preferred_element_type=F32)[0:1]
    slot = base + rank
    slot1 = jnp.sum(jnp.where(a1, slot, 0.0), axis=1, keepdims=True)
    slot2 = jnp.sum(jnp.where(a2, slot, 0.0), axis=1, keepdims=True)
    info_ref[...] = jnp.where(lane == 0, slot1, jnp.where(lane == 1, slot2,
                              jnp.where(lane == 2, g1, jnp.where(lane == 3, g2, 0.0))))
    pos = lax.broadcasted_iota(jnp.int32, (tr, cap), 1).astype(F32)
    onehot_t = jnp.where((pos == slot1) | (pos == slot2), 1.0, 0.0).astype(BF16)
    xc[i % 2] = lax.dot_general(onehot_t, hb, (((0,), (0,)), ((), ())), preferred_element_type=F32).astype(BF16)

    def copies(tile, wait):
        buf = tile % 2
        src = jnp.int32(0)
        for e in range(N_EXPERTS):
            ne = tab_ref[tile * 2 * N_EXPERTS + N_EXPERTS + e]
            dst = e * seg_units + tab_ref[tile * 2 * N_EXPERTS + e]
            _chunk_dma(ne, lambda off, bit, src=src, dst=dst: pltpu.make_async_copy(
                xc.at[buf].at[_rows(src + off, bit)], xs_ref.at[_rows(dst + off, bit)], sem.at[buf]), wait)
            src = src + ne

    for e in range(N_EXPERTS):
        ne = units[0, e].astype(jnp.int32)
        tab_ref[i * 2 * N_EXPERTS + e] = run[e]
        tab_ref[i * 2 * N_EXPERTS + N_EXPERTS + e] = ne
        run[e] = run[e] + ne
    copies(i, False)

    @pl.when(i > 0)
    def _():
        copies(i - 1, True)

    @pl.when(i == nt - 1)
    def _():
        copies(i, True)
        fill = EXPERT_TILE // ROW_UNIT
        xc[0, pl.ds(0, EXPERT_TILE), :] = jnp.zeros((EXPERT_TILE, xc.shape[2]), BF16)
        tails = [pltpu.make_async_copy(xc.at[0].at[_rows(0, fill)],
                                       xs_ref.at[_rows(e * seg_units + run[e], fill)], sem.at[0])
                 for e in range(N_EXPERTS)]
        for cp in tails:
            cp.start()
        for cp in tails:
            cp.wait()
        for e in range(N_EXPERTS):
            tot_ref[e] = run[e]


def _moe_expert_kernel(eo_ref, rb_ref, valid_ref, xs_ref, wg_ref, wu_ref, wd_ref, ys_ref, wg_s, wu_s, wd_s):
    w = pl.program_id(0)

    @pl.when((w == 0) | (eo_ref[w] != eo_ref[jnp.maximum(w - 1, 0)]))
    def _():
        wg_s[...] = wg_ref[0].astype(BF16)
        wu_s[...] = wu_ref[0].astype(BF16)
        wd_s[...] = wd_ref[0].astype(BF16)

    @pl.when(valid_ref[w] == 1)
    def _():
        x = xs_ref[...]
        act = jnp.dot(x, wg_s[...], preferred_element_type=F32)
        act = act * _sigmoid(act) * jnp.dot(x, wu_s[...], preferred_element_type=F32)
        ys_ref[...] = jnp.dot(act.astype(BF16), wd_s[...], preferred_element_type=F32).astype(ys_ref.dtype)


def _moe_combine_kernel(tab_ref, x_ref, info_ref, ys_ref, fg_ref, out_ref, yc, sem, *, tr, cap, seg_units):
    i = pl.program_id(0)

    nt = pl.num_programs(0)

    def copies(tile, wait):
        buf = tile % 2
        dst = jnp.int32(0)
        for e in range(N_EXPERTS):
            ne = tab_ref[tile * 2 * N_EXPERTS + N_EXPERTS + e]
            src = e * seg_units + tab_ref[tile * 2 * N_EXPERTS + e]
            _chunk_dma(ne, lambda off, bit, src=src, dst=dst: pltpu.make_async_copy(
                ys_ref.at[_rows(src + off, bit)], yc.at[buf].at[_rows(dst + off, bit)], sem.at[buf]), wait)
            dst = dst + ne

    @pl.when(i == 0)
    def _():
        yc[...] = jnp.zeros(yc.shape, yc.dtype)
        copies(i, False)

    @pl.when(i + 1 < nt)
    def _():
        copies(i + 1, False)

    copies(i, True)
    info = info_ref[...]
    pos = lax.broadcasted_iota(jnp.int32, (tr, cap), 1).astype(F32)
    rows = yc[i % 2]
    y1 = jnp.dot(jnp.where(pos == info[:, 0:1], 1.0, 0.0).astype(BF16), rows, preferred_element_type=F32)
    y2 = jnp.dot(jnp.where(pos == info[:, 1:2], 1.0, 0.0).astype(BF16), rows, preferred_element_type=F32)
    out_ref[...] = _rms(x_ref[...] + info[:, 2:3] * y1 + info[:, 3:4] * y2, fg_ref[...])


def moe_sparse_final(x, g, w_router_pad, wg, wu, wd, final_g):
    t, d = x.shape
    tr, te = ROUTE_TILE, EXPERT_TILE
    assert t % tr == 0
    ntiles = t // tr
    cap = _sorted_cap(tr)
    ne, _, ff = wg.shape
    seg_rows = -(-(t + (ROW_UNIT - 1) * ntiles + te) // te) * te
    seg_units = seg_rows // ROW_UNIT
    smem = pl.BlockSpec(memory_space=pltpu.SMEM)

    xs, info, tab, tot = pl.pallas_call(
        functools.partial(_moe_route_kernel, tr=tr, cap=cap, seg_units=seg_units),
        grid=(ntiles,),
        in_specs=[
            pl.BlockSpec((tr, d), lambda i: (i, 0)),
            pl.BlockSpec((1, d), lambda i: (0, 0)),
            pl.BlockSpec((d, LANES), lambda i: (0, 0)),
        ],
        out_specs=[pl.BlockSpec(memory_space=pl.ANY), pl.BlockSpec((tr, LANES), lambda i: (i, 0)), smem, smem],
        out_shape=[
            jax.ShapeDtypeStruct((ne * seg_rows, d), BF16),
            jax.ShapeDtypeStruct((t, LANES), F32),
            jax.ShapeDtypeStruct((ntiles * 2 * ne,), jnp.int32),
            jax.ShapeDtypeStruct((ne,), jnp.int32),
        ],
        scratch_shapes=[pltpu.VMEM((2, cap, d), BF16), pltpu.SMEM((ne,), jnp.int32), pltpu.SemaphoreType.DMA((2,))],
        compiler_params=_params(("arbitrary",)),
        name="moe_route",
    )(x, g.reshape(1, d), w_router_pad)

    tiles_per_e = (tot * ROW_UNIT + te - 1) // te
    ends = jnp.cumsum(tiles_per_e)
    n_items = ends[-1]
    max_items = (2 * t + ne * (ROW_UNIT - 1) * ntiles) // te + ne
    w = jnp.arange(max_items, dtype=jnp.int32)
    wc = jnp.minimum(w, n_items - 1)
    eo = jnp.sum((wc[:, None] >= ends[None, :]).astype(jnp.int32), axis=1)
    rb = (eo * (seg_rows // te) + wc - (ends - tiles_per_e)[eo]).astype(jnp.int32)
    valid = (w < n_items).astype(jnp.int32)

    ys = pl.pallas_call(
        _moe_expert_kernel,
        grid_spec=pltpu.PrefetchScalarGridSpec(
            num_scalar_prefetch=3,
            grid=(max_items,),
            in_specs=[
                pl.BlockSpec((te, d), lambda w, eo, rb, va: (rb[w], 0)),
                pl.BlockSpec((1, d, ff), lambda w, eo, rb, va: (eo[w], 0, 0)),
                pl.BlockSpec((1, d, ff), lambda w, eo, rb, va: (eo[w], 0, 0)),
                pl.BlockSpec((1, ff, d), lambda w, eo, rb, va: (eo[w], 0, 0)),
            ],
            out_specs=pl.BlockSpec((te, d), lambda w, eo, rb, va: (rb[w], 0)),
            scratch_shapes=[pltpu.VMEM((d, ff), BF16), pltpu.VMEM((d, ff), BF16), pltpu.VMEM((ff, d), BF16)],
        ),
        out_shape=jax.ShapeDtypeStruct((ne * seg_rows, d), BF16),
        compiler_params=_params(("arbitrary",)),
        name="moe_experts",
    )(eo, rb, valid, xs, wg, wu, wd)

    return pl.pallas_call(
        functools.partial(_moe_combine_kernel, tr=tr, cap=cap, seg_units=seg_units),
        grid_spec=pltpu.PrefetchScalarGridSpec(
            num_scalar_prefetch=1,
            grid=(ntiles,),
            in_specs=[
                pl.BlockSpec((tr, d), lambda i, tab: (i, 0)),
                pl.BlockSpec((tr, LANES), lambda i, tab: (i, 0)),
                pl.BlockSpec(memory_space=pl.ANY),
                pl.BlockSpec((1, d), lambda i, tab: (0, 0)),
            ],
            out_specs=pl.BlockSpec((tr, d), lambda i, tab: (i, 0)),
            scratch_shapes=[pltpu.VMEM((2, cap, d), BF16), pltpu.SemaphoreType.DMA((2,))],
        ),
        out_shape=jax.ShapeDtypeStruct((t, d), F32),
        compiler_params=_params(("arbitrary",)),
        name="moe_combine",
    )(tab, x, info, ys, final_g.reshape(1, d))


def _trunk(x, mem_k, mem_v, conv_buf, h0, past_k, past_v, p, lam_init):
    b, t, d = x.shape
    n = b * t
    xf = x.reshape(n, d)
    aw = DIFF_HEADS * DIFF_VD
    lru_w = p["lru_lambda"].shape[0]
    precise = past_k is not None
    tm = n if precise else 512
    tn = 1024
    assert t >= CONV_W - 1
    seq = lambda a: a.reshape(b, t, a.shape[-1])
    lam_params = (p["lam_q1"], p["lam_k1"], p["lam_q2"], p["lam_k2"])
    lru_args = (p["conv_w"], p["conv_b"], p["lru_wa"], p["lru_ba"].reshape(-1), p["lru_wx"],
                p["lru_bx"].reshape(-1), p["lru_lambda"])

    if precise:
        xb, gate, q, k, v = norm_linear_f32(xf, p["norm_mix_g"], p["w_in_f32"], tn=tn, split=True)
    else:
        xb, gate, q, k, v = norm_linear(xf, p["norm_mix_g"], p["w_in"], tm=tm, tn=tn,
                                        out_widths=[lru_w, lru_w, aw, aw, aw], out_dtypes=[F32] * 5)
    lru_out, h_last = lru_mixer(seq(xb), seq(gate), conv_buf, h0, *lru_args, tc=256, precise=precise)
    if precise:
        att = diff_attention_sample(seq(q), seq(k), seq(v), past_k, past_v, lam_params, p["subln_g"], lam_init)
    else:
        att = diff_attention_prompt(seq(q), seq(k), seq(v), lam_params, p["subln_g"], lam_init, tq=512)
    mix_in = [lru_out.reshape(n, lru_w), att.reshape(n, aw)]
    if precise:
        x1 = linear_residual_f32(xf, mix_in, p["w_out_f32"], tn=tn)
        (qx,) = norm_linear_f32(x1, p["norm_cross_g"], p["xq_w_f32"], tn=tn, split=False)
    else:
        x1 = linear_residual(xf, mix_in, p["w_out"], tm=tm, tn=tn)
        (qx,) = norm_linear(x1, p["norm_cross_g"], p["xq_w"], tm=tm, tn=tn, out_widths=[d], out_dtypes=[BF16])
    o = cross_attention_core(seq(qx), mem_k, mem_v, tq=512, precise="native" if precise else False)
    if precise:
        x2 = linear_residual_f32(x1, [o.reshape(n, d)], p["xo_w_f32"], tn=tn)
    else:
        x2 = linear_residual(x1, [o.reshape(n, d)], p["xo_w"], tm=tm, tn=tn)

    experts = (p["exp_gate"], p["exp_up"], p["exp_down"])
    if precise:
        y = moe_dense_final(x2, p["norm_ffn_g"], p["router_pad_f32"], *experts, p["final_norm_g"], tm=tm,
                            precise_router=True)
    else:
        y = moe_sparse_final(x2, p["norm_ffn_g"], p["router_pad"], *experts, p["final_norm_g"])
    new_conv = seq(xb)[:, t - (CONV_W - 1):, :]
    return y.reshape(b, t, d), new_conv, h_last, k, v


def kernel(x_prompt, x_sample, cache_diff_k, cache_diff_v, cache_mem_k, cache_mem_v, state_conv, state_lru, mem_prompt, norm_mix_g, w_in, conv_w, conv_b, lru_wa, lru_ba, lru_wx, lru_bx, lru_lambda, lam_q1, lam_k1, lam_q2, lam_k2, subln_g, w_out, norm_cross_g, norm_mem_g, xq_w, xk_w, xv_w, xo_w, norm_ffn_g, router_group_w, router_expert_w, exp_gate, exp_up, exp_down, final_norm_g):
    depth = w_in.shape[0]
    assert depth == 1, "single-layer step"
    bp, tp, d = x_prompt.shape
    bs, ts, _ = x_sample.shape
    past = cache_diff_k.shape[2]
    n_mem = mem_prompt.shape[1]
    aw = DIFF_HEADS * DIFF_VD
    l = 0
    lam_init = 0.8 - 0.6 * math.exp(-0.3 * l)

    router = jnp.concatenate([router_group_w[l], router_expert_w[l]], axis=1)
    router_pad_f32 = jnp.pad(router, ((0, 0), (0, LANES - router.shape[1])))
    router_pad = router_pad_f32.astype(BF16)
    p = dict(router_pad_f32=router_pad_f32, w_in_f32=w_in[l], w_out_f32=w_out[l], xq_w_f32=xq_w[l],
             xo_w_f32=xo_w[l], **dict(norm_mix_g=norm_mix_g[l], conv_w=conv_w[l], conv_b=conv_b[l], lru_wa=lru_wa[l],
             lru_ba=lru_ba[l], lru_wx=lru_wx[l], lru_bx=lru_bx[l], lru_lambda=lru_lambda[l], lam_q1=lam_q1[l],
             lam_k1=lam_k1[l], lam_q2=lam_q2[l], lam_k2=lam_k2[l], subln_g=subln_g[l],
             norm_cross_g=norm_cross_g[l], norm_ffn_g=norm_ffn_g[l], router_pad=router_pad,
             final_norm_g=final_norm_g))
    for name, w in (("w_in", w_in), ("w_out", w_out), ("xq_w", xq_w), ("xo_w", xo_w)):
        p[name] = w[l].astype(BF16)
    p.update(exp_gate=exp_gate[l], exp_up=exp_up[l], exp_down=exp_down[l])

    memf = mem_prompt.reshape(bp * n_mem, d)
    w_mem = jnp.concatenate([xk_w[l].astype(BF16), xv_w[l].astype(BF16)], axis=1)
    mk_p, mv_p = norm_linear(memf, norm_mem_g[l], w_mem, tm=512, tn=1024, out_widths=[d, d], out_dtypes=[F32, F32])
    mk_p = mk_p.reshape(bp, n_mem, d)
    mv_p = mv_p.reshape(bp, n_mem, d)

    zero_buf = jnp.zeros((bp, CONV_W - 1, lru_lambda.shape[1]), F32)
    zero_h = jnp.zeros((bp, lru_lambda.shape[1]), F32)
    y_p, cb_p, hl_p, k_p, v_p = _trunk(x_prompt, mk_p, mv_p, zero_buf, zero_h, None, None, p, lam_init)
    y_s, cb_s, hl_s, k_s, v_s = _trunk(x_sample, cache_mem_k[l], cache_mem_v[l], state_conv[l], state_lru[l],
                                       cache_diff_k[l], cache_diff_v[l],
                                       p, lam_init)

    hd2 = 2 * DIFF_HD
    return (y_p, y_s,
            k_p.reshape(1, bp, tp, DIFF_HEADS, hd2), v_p.reshape(1, bp, tp, DIFF_HEADS, DIFF_VD),
            mk_p.reshape(1, bp, n_mem, X_HEADS, d // X_HEADS), mv_p.reshape(1, bp, n_mem, X_HEADS, d // X_HEADS),
            cb_p[None], hl_p[None],
            k_s.reshape(1, bs, ts, DIFF_HEADS, hd2), v_s.reshape(1, bs, ts, DIFF_HEADS, DIFF_VD),
            cb_s[None], hl_s[None].astype(state_lru.dtype))
```

```python
import functools
import math

import jax
import jax.numpy as jnp
from jax import lax
from jax.experimental import pallas as pl
from jax.experimental.pallas import tpu as pltpu

F32 = jnp.float32
BF16 = jnp.bfloat16

CHUNK = 64
CONV_W = 4
LRU_C = 8.0
LRU_BLOCK = 64
DIFF_HEADS = 8
DIFF_HD = 64
DIFF_VD = 2 * DIFF_HD
X_HEADS = 4
N_GROUPS = 4
EXP_PER_GROUP = 4
N_EXPERTS = N_GROUPS * EXP_PER_GROUP
EPS = 1e-6

LANES = 128
SUBLANES = 8
MXU_DIM = 256
VMEM_LIMIT_BYTES = 56 * 1024 * 1024


def _params(semantics):
    return pltpu.CompilerParams(dimension_semantics=semantics, vmem_limit_bytes=VMEM_LIMIT_BYTES)


def _split(a):
    a = a.astype(F32)
    hi = a.astype(BF16)
    return hi, (a - hi.astype(F32)).astype(BF16)


def _dot_dims(a, b, dims, precise):
    if not precise:
        return lax.dot_general(a.astype(BF16), b.astype(BF16), dims, preferred_element_type=F32)
    if precise == "native":
        return lax.dot_general(a.astype(F32), b.astype(F32), dims, precision=lax.Precision.HIGHEST,
                               preferred_element_type=F32)
    ah, al = _split(a)
    bh, bl = _split(b)
    mm = lambda x, y: lax.dot_general(x, y, dims, preferred_element_type=F32)
    return mm(ah, bh) + (mm(ah, bl) + mm(al, bh))


def _dot(a, b, precise=False):
    return _dot_dims(a, b, (((1,), (0,)), ((), ())), precise)


def _dot_nt(a, b, precise=False):
    return _dot_dims(a, b, (((1,), (1,)), ((), ())), precise)


def _rms(x, g):
    return x * lax.rsqrt(jnp.mean(x * x, axis=-1, keepdims=True) + EPS) * g


def _sigmoid(x):
    return 1.0 / (1.0 + jnp.exp(-x))


def _gelu_tanh(x):
    c = math.sqrt(2.0 / math.pi)
    return 0.5 * x * (1.0 + jnp.tanh(c * (x + 0.044715 * (x * x * x))))


def _norm_linear_kernel(x_ref, g_ref, w_ref, *out_refs, tn):
    h = _rms(x_ref[...], g_ref[...]).astype(BF16)
    col = 0
    for o_ref in out_refs:
        for c in range(o_ref.shape[1] // tn):
            o_ref[:, c * tn:(c + 1) * tn] = jnp.dot(
                h, w_ref[:, col:col + tn], preferred_element_type=F32).astype(o_ref.dtype)
            col += tn


def norm_linear(x, g, w, *, tm, tn, out_widths, out_dtypes):
    t, k = x.shape
    n = w.shape[1]
    tm = min(tm, t)
    assert sum(out_widths) == n and all(wd % tn == 0 for wd in out_widths)
    return pl.pallas_call(
        functools.partial(_norm_linear_kernel, tn=tn),
        grid=(t // tm,),
        in_specs=[
            pl.BlockSpec((tm, k), lambda i: (i, 0)),
            pl.BlockSpec((1, k), lambda i: (0, 0)),
            pl.BlockSpec((k, n), lambda i: (0, 0)),
        ],
        out_specs=[pl.BlockSpec((tm, wd), lambda i: (i, 0)) for wd in out_widths],
        out_shape=[jax.ShapeDtypeStruct((t, wd), dt) for wd, dt in zip(out_widths, out_dtypes)],
        compiler_params=_params(("parallel",)),
        name="norm_linear",
    )(x, g.reshape(1, k), w)


def _linear_res_kernel(*refs, n_in, tn):
    res_ref = refs[0]
    a_refs = refs[1:1 + n_in]
    w_ref = refs[1 + n_in]
    out_ref = refs[2 + n_in]
    kc = a_refs[0].shape[1]
    for c in range(out_ref.shape[1] // tn):
        cols = slice(c * tn, (c + 1) * tn)
        acc = res_ref[:, cols]
        for r, a_ref in enumerate(a_refs):
            acc = acc + _dot(a_ref[...], w_ref[r * kc:(r + 1) * kc, cols])
        out_ref[:, cols] = acc


def linear_residual(res, a_list, w, *, tm, tn):
    t, n = res.shape
    tm = min(tm, t)
    n_in = len(a_list)
    kc = a_list[0].shape[1]
    in_specs = [pl.BlockSpec((tm, n), lambda i: (i, 0))]
    in_specs += [pl.BlockSpec((tm, kc), lambda i: (i, 0)) for _ in range(n_in)]
    in_specs += [pl.BlockSpec(w.shape, lambda i: (0, 0))]
    return pl.pallas_call(
        functools.partial(_linear_res_kernel, n_in=n_in, tn=tn),
        grid=(t // tm,),
        in_specs=in_specs,
        out_specs=pl.BlockSpec((tm, n), lambda i: (i, 0)),
        out_shape=jax.ShapeDtypeStruct((t, n), F32),
        compiler_params=_params(("parallel",)),
        name="linear_residual",
    )(res, *a_list, w)


def _norm_linear_f32_kernel(x_ref, g_ref, w_ref, *refs):
    out_refs, h_ref = refs[:-1], refs[-1]
    j = pl.program_id(1)

    @pl.when(j == 0)
    def _():
        h_ref[...] = _rms(x_ref[...], g_ref[...])

    if len(out_refs) == 1:
        out_refs[0][...] = _dot(h_ref[...], w_ref[...], True)
    else:
        for c, o_ref in enumerate(out_refs):
            @pl.when(j == c)
            def _(o_ref=o_ref):
                o_ref[...] = _dot(h_ref[...], w_ref[...], True)


def norm_linear_f32(x, g, w, *, tn, split):
    t, k = x.shape
    n = w.shape[1]
    nj = n // tn
    if split:
        out_shape = [jax.ShapeDtypeStruct((t, tn), F32) for _ in range(nj)]
        out_specs = [pl.BlockSpec((t, tn), lambda i, j: (i, 0)) for _ in range(nj)]
    else:
        out_shape = [jax.ShapeDtypeStruct((t, n), F32)]
        out_specs = [pl.BlockSpec((t, tn), lambda i, j: (i, j))]
    return pl.pallas_call(
        _norm_linear_f32_kernel,
        grid=(1, nj),
        in_specs=[
            pl.BlockSpec((t, k), lambda i, j: (i, 0)),
            pl.BlockSpec((1, k), lambda i, j: (0, 0)),
            pl.BlockSpec((k, tn), lambda i, j: (0, j)),
        ],
        out_specs=out_specs,
        out_shape=out_shape,
        scratch_shapes=[pltpu.VMEM((t, k), F32)],
        compiler_params=_params(("parallel", "arbitrary")),
        name="norm_linear_f32",
    )(x, g.reshape(1, k), w)


def _linear_res_f32_kernel(*refs, n_in):
    res_ref = refs[0]
    a_refs = refs[1:1 + n_in]
    w_refs = refs[1 + n_in:1 + 2 * n_in]
    out_ref = refs[1 + 2 * n_in]
    acc = res_ref[...]
    for a_ref, w_ref in zip(a_refs, w_refs):
        acc = acc + _dot(a_ref[...], w_ref[...], True)
    out_ref[...] = acc


def linear_residual_f32(res, a_list, w, *, tn):
    t, n = res.shape
    n_in = len(a_list)
    kc = a_list[0].shape[1]
    in_specs = [pl.BlockSpec((t, tn), lambda i, j: (i, j))]
    in_specs += [pl.BlockSpec((t, kc), lambda i, j: (i, 0)) for _ in range(n_in)]
    in_specs += [pl.BlockSpec((kc, tn), lambda i, j, c=c: (c, j)) for c in range(n_in)]
    return pl.pallas_call(
        functools.partial(_linear_res_f32_kernel, n_in=n_in),
        grid=(1, n // tn),
        in_specs=in_specs,
        out_specs=pl.BlockSpec((t, tn), lambda i, j: (i, j)),
        out_shape=jax.ShapeDtypeStruct((t, n), F32),
        compiler_params=_params(("parallel", "arbitrary")),
        name="linear_residual_f32",
    )(res, *a_list, *([w] * n_in))


def _lru_kernel(xb_ref, gate_ref, cbuf_ref, h0_ref, cw_ref, cb_ref, wa_ref, ba_ref, wx_ref, bx_ref, lam_ref,
                out_ref, hlast_ref, xpad, hcar, a_s, u_s, *, tc, width, precise):
    c = pl.program_id(1)
    nslab = width // MXU_DIM
    ngrp = tc // SUBLANES

    @pl.when(c == 0)
    def _():
        xpad[pl.ds(0, SUBLANES), :] = cbuf_ref[0]
        hcar[...] = h0_ref[0]

    xpad[pl.ds(SUBLANES, tc), :] = xb_ref[0]
    xc = cb_ref[...] + cw_ref[pl.ds(CONV_W - 1, 1), :] * xpad[pl.ds(SUBLANES, tc), :]
    for j in range(CONV_W - 1):
        xc = xc + cw_ref[pl.ds(j, 1), :] * xpad[pl.ds(SUBLANES - (CONV_W - 1) + j, tc), :]
    xpad[pl.ds(0, SUBLANES), :] = xpad[pl.ds(tc, SUBLANES), :]

    lam = lam_ref[...]
    softplus_neg = jnp.maximum(-lam, 0.0) + jnp.log1p(jnp.exp(-jnp.abs(lam)))
    c8 = -LRU_C * softplus_neg

    sub = lax.broadcasted_iota(jnp.int32, (ngrp, SUBLANES, MXU_DIM), 1)
    for s in range(nslab):
        cols = slice(s * MXU_DIM, (s + 1) * MXU_DIM)
        xs = xc[:, cols]
        r = _sigmoid(_dot(xs, wa_ref[s], precise) + ba_ref[:, cols])
        i = _sigmoid(_dot(xs, wx_ref[s], precise) + bx_ref[:, cols])
        a = jnp.exp(c8[:, cols] * r)
        u = jnp.sqrt(1.0 - a * a) * (i * xs)
        a3 = a.reshape(ngrp, SUBLANES, MXU_DIM)
        u3 = u.reshape(ngrp, SUBLANES, MXU_DIM)
        d = 1
        while d < SUBLANES:
            a_sh = pltpu.roll(a3, d, 1)
            u_sh = pltpu.roll(u3, d, 1)
            keep = sub >= d
            u3 = jnp.where(keep, u3 + a3 * u_sh, u3)
            a3 = jnp.where(keep, a3 * a_sh, a3)
            d *= 2
        a_s[:, cols] = a3.reshape(tc, MXU_DIM)
        u_s[:, cols] = u3.reshape(tc, MXU_DIM)

    def body(g, hin):
        rows = pl.ds(pl.multiple_of(g * SUBLANES, SUBLANES), SUBLANES)
        h = u_s[rows, :] + a_s[rows, :] * hin
        u_s[rows, :] = h
        return h[SUBLANES - 1:SUBLANES, :]

    hfin = lax.fori_loop(0, ngrp, body, hcar[...])
    hcar[...] = hfin
    out_ref[0] = (u_s[...] * _gelu_tanh(gate_ref[0])).astype(out_ref.dtype)
    hlast_ref[0] = hfin


def _block_diag(w, per):
    nb, k, _ = w.shape
    w4 = w.reshape(nb // per, per, k, k)
    eye = jnp.eye(per, dtype=w.dtype)
    return jnp.einsum("cipq,ij->cipjq", w4, eye).reshape(nb // per, per * k, per * k)


def lru_mixer(xb, gate, conv_buf, h0, conv_w, conv_b, wa, ba, wx, bx, lam, *, tc, precise):
    b, t, width = xb.shape
    tc = min(tc, t)
    per = MXU_DIM // LRU_BLOCK
    act_dt = F32 if precise else BF16
    wa_bd = _block_diag(wa, per).astype(act_dt)
    wx_bd = _block_diag(wx, per).astype(act_dt)
    nslab = wa_bd.shape[0]
    cbuf8 = jnp.concatenate([jnp.zeros((b, SUBLANES - (CONV_W - 1), width), F32), conv_buf.astype(F32)], axis=1)
    row = lambda v: v.reshape(1, width).astype(F32)
    vec_spec = pl.BlockSpec((1, width), lambda bi, ci: (0, 0))
    seq_spec = pl.BlockSpec((1, tc, width), lambda bi, ci: (bi, ci, 0))
    wspec = pl.BlockSpec((nslab, MXU_DIM, MXU_DIM), lambda bi, ci: (0, 0, 0))
    out, hlast = pl.pallas_call(
        functools.partial(_lru_kernel, tc=tc, width=width, precise=precise),
        grid=(b, t // tc),
        in_specs=[
            seq_spec, seq_spec,
            pl.BlockSpec((1, SUBLANES, width), lambda bi, ci: (bi, 0, 0)),
            pl.BlockSpec((1, 1, width), lambda bi, ci: (bi, 0, 0)),
            pl.BlockSpec((CONV_W, width), lambda bi, ci: (0, 0)),
            vec_spec, wspec, vec_spec, wspec, vec_spec, vec_spec,
        ],
        out_specs=[seq_spec, pl.BlockSpec((1, 1, width), lambda bi, ci: (bi, 0, 0))],
        out_shape=[jax.ShapeDtypeStruct((b, t, width), act_dt), jax.ShapeDtypeStruct((b, 1, width), F32)],
        scratch_shapes=[
            pltpu.VMEM((tc + SUBLANES, width), F32),
            pltpu.VMEM((1, width), F32),
            pltpu.VMEM((tc, width), F32),
            pltpu.VMEM((tc, width), F32),
        ],
        compiler_params=_params(("parallel", "arbitrary")),
        name="lru_mixer",
    )(xb, gate, cbuf8, h0.reshape(b, 1, width).astype(F32), conv_w.astype(F32), row(conv_b), wa_bd,
      row(ba), wx_bd, row(bx), row(lam))
    return out, hlast.reshape(b, width)


def _diff_lambda(lq1, lk1, lq2, lk2, lam_init):
    s1 = jnp.sum(lq1[...] * lk1[...], axis=-1, keepdims=True)
    s2 = jnp.sum(lq2[...] * lk2[...], axis=-1, keepdims=True)
    return jnp.exp(s1) - jnp.exp(s2) + lam_init


_LOG2E_PARTS = (1.4453125, -0.00262451171875, 7.063150405883789e-06)
LOG2E = sum(_LOG2E_PARTS)
N_BIAS_COLS = 2 * len(_LOG2E_PARTS)


def _attn_prompt_kernel(slopes_ref, q_ref, k_ref, v_ref, lq1, lk1, lq2, lk2, subln_ref, out_ref,
                        kaug, vt, qt, m_s, l_s, acc_s, s_s, diag_s, *, tq, lam_init):
    h = pl.program_id(1)
    qi = pl.program_id(2)
    slope = slopes_ref[h]
    nblk, tk, _ = kaug.shape
    nparts = len(_LOG2E_PARTS)

    @pl.when(qi == 0)
    def _():
        pos = lax.broadcasted_iota(jnp.int32, (tk, LANES), 0)
        lane = lax.broadcasted_iota(jnp.int32, (tk, LANES), 1)
        within = (pos % CHUNK).astype(F32) * slope
        for j in range(nblk):
            rows = slice(j * tk, (j + 1) * tk)
            kaug[j, :, 0:LANES] = k_ref[0, rows, :].astype(BF16)
            coarse = ((pos + j * tk) // CHUNK * CHUNK).astype(F32) * slope
            cols = jnp.where(lane < nparts, coarse, jnp.where(lane < N_BIAS_COLS, within, 0.0))
            kaug[j, :, LANES:2 * LANES] = cols.astype(BF16)
            vt[j] = v_ref[0, rows, :].T.astype(BF16)
        r = lax.broadcasted_iota(jnp.int32, (LANES, tq), 0)
        part = jnp.where(r % nparts == 0, _LOG2E_PARTS[0],
                         jnp.where(r % nparts == 1, _LOG2E_PARTS[1], _LOG2E_PARTS[2]))
        const_rows = jnp.where(r < N_BIAS_COLS, part, 0.0).astype(BF16)
        qt[0, LANES:2 * LANES, :] = const_rows
        qt[1, LANES:2 * LANES, :] = const_rows
        kpos = lax.broadcasted_iota(jnp.int32, (tk, tq), 0)
        qpos = lax.broadcasted_iota(jnp.int32, (tk, tq), 1)
        fix = jnp.where(kpos > qpos, (-2.0 * LOG2E) * slope * (kpos - qpos).astype(F32), 0.0)
        diag_s[...] = jnp.where((kpos // CHUNK) <= (qpos // CHUNK), fix, -jnp.inf)

    qs = q_ref[0] * (LOG2E * DIFF_HD ** -0.5)
    qlane = lax.broadcasted_iota(jnp.int32, qs.shape, 1)
    qt[0, 0:LANES, :] = jnp.where(qlane < DIFF_HD, qs, 0.0).T.astype(BF16)
    qt[1, 0:LANES, :] = jnp.where(qlane >= DIFF_HD, qs, 0.0).T.astype(BF16)
    m_s[...] = jnp.full(m_s.shape, -jnp.inf, F32)
    l_s[...] = jnp.zeros(l_s.shape, F32)
    acc_s[...] = jnp.zeros(acc_s.shape, F32)

    def blocks(kis, diagonal):
        slab = MXU_DIM
        colmax = [[None, None] for _ in kis]
        for j, ki in enumerate(kis):
            last = diagonal and j == len(kis) - 1
            for m in range(2):
                for r in range(tk // slab):
                    rows = slice(r * slab, (r + 1) * slab)
                    s = jnp.dot(kaug[ki, rows, :], qt[m], preferred_element_type=F32)
                    if last:
                        s = s + diag_s[rows, :]
                    s_s[j, m, rows, :] = s
                    pm = jnp.max(s, axis=0, keepdims=True)
                    colmax[j][m] = pm if r == 0 else jnp.maximum(colmax[j][m], pm)
        for j, ki in enumerate(kis):
            vblk = vt[ki]
            for m in range(2):
                m_old = m_s[m]
                m_new = jnp.maximum(m_old, colmax[j][m])
                p = jnp.exp2(s_s[j, m] - m_new)
                alpha = jnp.exp2(m_old - m_new)
                l_s[m] = alpha * l_s[m] + jnp.sum(p, axis=0, keepdims=True)
                acc_s[m] = alpha * acc_s[m] + jnp.dot(vblk, p.astype(BF16), preferred_element_type=F32)
                m_s[m] = m_new

    def pair_body(j, carry):
        blocks([2 * j, 2 * j + 1], False)
        return carry

    lax.fori_loop(0, qi // 2, pair_body, 0)

    @pl.when(qi % 2 == 1)
    def _():
        blocks([qi - 1, qi], True)

    @pl.when(qi % 2 == 0)
    def _():
        blocks([qi], True)

    lam = _diff_lambda(lq1, lk1, lq2, lk2, lam_init)
    o = acc_s[0] / l_s[0] - lam * (acc_s[1] / l_s[1])
    o = o * lax.rsqrt(jnp.mean(o * o, axis=0, keepdims=True) + EPS) * subln_ref[...] * (1.0 - lam_init)
    out_ref[0] = o.T.astype(out_ref.dtype)


def _alibi_slopes():
    return 2.0 ** (-8.0 * jnp.arange(1, DIFF_HEADS + 1, dtype=F32) / DIFF_HEADS)


def diff_attention_prompt(q, k, v, lam_params, subln_g, lam_init, *, tq):
    b, t, aw = q.shape
    tq = min(tq, t)
    hd2 = 2 * DIFF_HD
    assert hd2 == LANES and DIFF_VD == LANES and tq % CHUNK == 0
    assert t // CHUNK <= 256, "chunk index must stay exact in bf16"
    lrow = lambda p: p.reshape(1, DIFF_HD).astype(F32)
    lspec = pl.BlockSpec((1, DIFF_HD), lambda bi, hi, qi: (0, 0))
    kv_spec = pl.BlockSpec((1, t, hd2), lambda bi, hi, qi: (bi, 0, hi))
    q_spec = pl.BlockSpec((1, tq, hd2), lambda bi, hi, qi: (bi, qi, hi))
    return pl.pallas_call(
        functools.partial(_attn_prompt_kernel, tq=tq, lam_init=lam_init),
        grid=(b, DIFF_HEADS, t // tq),
        in_specs=[
            pl.BlockSpec(memory_space=pltpu.SMEM),
            q_spec, kv_spec, kv_spec, lspec, lspec, lspec, lspec,
            pl.BlockSpec((DIFF_VD, 1), lambda bi, hi, qi: (0, 0)),
        ],
        out_specs=q_spec,
        out_shape=jax.ShapeDtypeStruct((b, t, aw), BF16),
        scratch_shapes=[
            pltpu.VMEM((t // tq, tq, 2 * LANES), BF16),
            pltpu.VMEM((t // tq, DIFF_VD, tq), BF16),
            pltpu.VMEM((2, 2 * LANES, tq), BF16),
            pltpu.VMEM((2, 1, tq), F32),
            pltpu.VMEM((2, 1, tq), F32),
            pltpu.VMEM((2, DIFF_VD, tq), F32),
            pltpu.VMEM((2, 2, tq, tq), F32),
            pltpu.VMEM((tq, tq), F32),
        ],
        compiler_params=_params(("parallel", "parallel", "arbitrary")),
        name="diff_attention_prompt",
    )(_alibi_slopes(), q, k, v, *[lrow(p) for p in lam_params], subln_g.reshape(DIFF_VD, 1).astype(F32))


def _attn_sample_kernel(q_ref, kp_ref, vp_ref, kn_ref, vn_ref, lq1, lk1, lq2, lk2, subln_ref, out_ref,
                        *, past, tq, lam_init):
    hd2 = 2 * DIFF_HD

    def bias_mask(nk, k_off):
        qpos = past + lax.broadcasted_iota(jnp.int32, (tq, nk), 0)
        kpos = k_off + lax.broadcasted_iota(jnp.int32, (tq, nk), 1)
        dist = jnp.abs(qpos - kpos).astype(F32)
        allowed = (kpos // CHUNK) <= (qpos // CHUNK)
        return dist, allowed

    dist_p, ok_p = bias_mask(past, 0)
    dist_n, ok_n = bias_mask(tq, past)
    lam = _diff_lambda(lq1, lk1, lq2, lk2, lam_init)
    qlane = lax.broadcasted_iota(jnp.int32, (tq, hd2), 1)
    for h in range(DIFF_HEADS):
        slope = 2.0 ** (-8.0 * (h + 1) / DIFF_HEADS)
        cols = slice(h * hd2, (h + 1) * hd2)
        q = q_ref[0, :, cols]
        kp, vp = kp_ref[0, :, h, :], vp_ref[0, :, h, :]
        kn, vn = kn_ref[0, :, cols], vn_ref[0, :, cols]
        probs = []
        q2 = jnp.concatenate([jnp.where(qlane < DIFF_HD, q, 0.0), jnp.where(qlane >= DIFF_HD, q, 0.0)], axis=0)
        sp2 = _dot_nt(q2, kp, "native") * (DIFF_HD ** -0.5)
        sn2 = _dot_nt(q2, kn, "native") * (DIFF_HD ** -0.5)
        for m in range(2):
            rows = slice(m * tq, (m + 1) * tq)
            sp = jnp.where(ok_p, sp2[rows] - slope * dist_p, -jnp.inf)
            sn = jnp.where(ok_n, sn2[rows] - slope * dist_n, -jnp.inf)
            mx = jnp.maximum(jnp.max(sp, axis=-1, keepdims=True), jnp.max(sn, axis=-1, keepdims=True))
            pp = jnp.exp(sp - mx)
            pn = jnp.exp(sn - mx)
            l = jnp.sum(pp, axis=-1, keepdims=True) + jnp.sum(pn, axis=-1, keepdims=True)
            probs.append((pp / l, pn / l))
        o = (_dot(probs[0][0] - lam * probs[1][0], vp, "native")
             + _dot(probs[0][1] - lam * probs[1][1], vn, "native"))
        out_ref[0, :, cols] = (o * lax.rsqrt(jnp.mean(o * o, axis=-1, keepdims=True) + EPS) * subln_ref[...]
                               * (1.0 - lam_init)).astype(out_ref.dtype)


def diff_attention_sample(q, k_new, v_new, past_k, past_v, lam_params, subln_g, lam_init):
    b, t, aw = q.shape
    past = past_k.shape[1]
    lrow = lambda p: p.reshape(1, DIFF_HD).astype(F32)
    lspec = pl.BlockSpec((1, DIFF_HD), lambda bi: (0, 0))
    new_spec = pl.BlockSpec((1, t, aw), lambda bi: (bi, 0, 0))
    past_spec = pl.BlockSpec((1, past, DIFF_HEADS, DIFF_VD), lambda bi: (bi, 0, 0, 0))
    return pl.pallas_call(
        functools.partial(_attn_sample_kernel, past=past, tq=t, lam_init=lam_init),
        grid=(b,),
        in_specs=[
            new_spec, past_spec, past_spec, new_spec, new_spec, lspec, lspec, lspec, lspec,
            pl.BlockSpec((1, DIFF_VD), lambda bi: (0, 0)),
        ],
        out_specs=new_spec,
        out_shape=jax.ShapeDtypeStruct((b, t, aw), F32),
        compiler_params=_params(("parallel",)),
        name="diff_attention_sample",
    )(q, past_k, past_v, k_new, v_new, *[lrow(p) for p in lam_params], subln_g.reshape(1, DIFF_VD).astype(F32))


def _cross_kernel(q_ref, mk_ref, mv_ref, out_ref, *, precise):
    d = q_ref.shape[-1]
    hd = d // X_HEADS
    per_head = len(mk_ref.shape) == 4
    for hh in range(X_HEADS):
        cols = slice(hh * hd, (hh + 1) * hd)
        mk = mk_ref[0, :, hh, :] if per_head else mk_ref[0, :, cols]
        mv = mv_ref[0, :, hh, :] if per_head else mv_ref[0, :, cols]
        s = _dot_nt(q_ref[0, :, cols], mk, precise) * (hd ** -0.5)
        p = jnp.exp(s - jnp.max(s, axis=-1, keepdims=True))
        p = p / jnp.sum(p, axis=-1, keepdims=True)
        out_ref[0, :, cols] = _dot(p, mv, precise).astype(out_ref.dtype)


def cross_attention_core(q, mk, mv, *, tq, precise):
    b, t, d = q.shape
    nm = mk.shape[1]
    tq = min(tq, t)
    q_spec = pl.BlockSpec((1, tq, d), lambda bi, qi: (bi, qi, 0))
    m_spec = pl.BlockSpec((1,) + mk.shape[1:], lambda bi, qi: (bi,) + (0,) * (mk.ndim - 1))
    return pl.pallas_call(
        functools.partial(_cross_kernel, precise=precise),
        grid=(b, t // tq),
        in_specs=[q_spec, m_spec, m_spec],
        out_specs=q_spec,
        out_shape=jax.ShapeDtypeStruct((b, t, d), F32 if precise else BF16),
        compiler_params=_params(("parallel", "arbitrary")),
        name="cross_attention_core",
    )(q, mk, mv)


def _first_argmax(vals, lane, valid):
    masked = jnp.where(valid, vals, -jnp.inf)
    mx = jnp.max(masked, axis=-1, keepdims=True)
    idx = jnp.min(jnp.where(masked == mx, lane, LANES), axis=-1, keepdims=True)
    return mx, idx


def _route(logits):
    lane = lax.broadcasted_iota(jnp.int32, logits.shape, 1).astype(F32)
    is_group = lane < N_GROUPS
    gmax, gidx = _first_argmax(logits, lane, is_group)
    gsum = jnp.sum(jnp.where(is_group, jnp.exp(logits - gmax), 0.0), axis=-1, keepdims=True)
    g_top = 1.0 / gsum
    lo = N_GROUPS + gidx * EXP_PER_GROUP
    in_group = (lane >= lo) & (lane < lo + EXP_PER_GROUP)
    e1, i1 = _first_argmax(logits, lane, in_group)
    e2, i2 = _first_argmax(logits, lane, in_group & (lane != i1))
    w2 = jnp.exp(e2 - e1)
    gate1 = g_top / (1.0 + w2)
    gate2 = g_top * w2 / (1.0 + w2)
    return i1 - N_GROUPS, i2 - N_GROUPS, gate1, gate2


def _combine_weights(logits):
    lane = lax.broadcasted_iota(jnp.int32, logits.shape, 1).astype(F32)
    x1, x2, gate1, gate2 = _route(logits)
    return jnp.where(lane == x1, gate1, 0.0) + jnp.where(lane == x2, gate2, 0.0)


def _moe_dense_kernel(x_ref, g_ref, wr_ref, wg_ref, wu_ref, wd_ref, fg_ref, out_ref, h_s, comb_s, acc_s,
                      *, precise_router):
    e = pl.program_id(1)

    @pl.when(e == 0)
    def _():
        h = _rms(x_ref[...], g_ref[...])
        h_s[...] = h.astype(BF16)
        comb_s[...] = _combine_weights(_dot(h, wr_ref[...], precise_router))
        acc_s[...] = x_ref[...]

    hb = h_s[...]
    act = _dot(hb, wg_ref[0])
    act = act * _sigmoid(act) * _dot(hb, wu_ref[0])
    y = _dot(act, wd_ref[0])
    lane = lax.broadcasted_iota(jnp.int32, comb_s.shape, 1)
    ce = jnp.sum(jnp.where(lane == e, comb_s[...], 0.0), axis=-1, keepdims=True)
    acc_s[...] += ce * y

    @pl.when(e == pl.num_programs(1) - 1)
    def _():
        out_ref[...] = _rms(acc_s[...], fg_ref[...])


def moe_dense_final(x, g, w_router_pad, wg, wu, wd, final_g, *, tm, precise_router):
    t, d = x.shape
    tm = min(tm, t)
    ne, _, ff = wg.shape
    return pl.pallas_call(
        functools.partial(_moe_dense_kernel, precise_router=precise_router),
        grid=(t // tm, ne),
        in_specs=[
            pl.BlockSpec((tm, d), lambda i, e: (i, 0)),
            pl.BlockSpec((1, d), lambda i, e: (0, 0)),
            pl.BlockSpec((d, LANES), lambda i, e: (0, 0)),
            pl.BlockSpec((1, d, ff), lambda i, e: (e, 0, 0)),
            pl.BlockSpec((1, d, ff), lambda i, e: (e, 0, 0)),
            pl.BlockSpec((1, ff, d), lambda i, e: (e, 0, 0)),
            pl.BlockSpec((1, d), lambda i, e: (0, 0)),
        ],
        out_specs=pl.BlockSpec((tm, d), lambda i, e: (i, 0)),
        out_shape=jax.ShapeDtypeStruct((t, d), F32),
        scratch_shapes=[pltpu.VMEM((tm, d), BF16), pltpu.VMEM((tm, LANES), F32), pltpu.VMEM((tm, d), F32)],
        compiler_params=_params(("parallel", "arbitrary")),
        name="moe_dense_final",
    )(x, g.reshape(1, d), w_router_pad, wg, wu, wd, final_g.reshape(1, d))


ROW_UNIT = 16
ROUTE_TILE = MXU_DIM
EXPERT_TILE = 512
UNIT_BITS = (16, 8, 4, 2, 1)


def _sorted_cap(tr):
    rows = 2 * tr + N_EXPERTS * (ROW_UNIT - 1)
    return -(-rows // MXU_DIM) * MXU_DIM


def _chunk_dma(units, make_copy):
    off = jnp.int32(0)
    for bit in UNIT_BITS:
        take = (units & bit) != 0

        @pl.when(take)
        def _(off=off, bit=bit):
            make_copy(off, bit).start()

        off = off + jnp.where(take, bit, 0)


def _wait_units(total_units, make_copy, max_units):
    bit = 1
    while bit <= max_units:
        @pl.when((total_units & bit) != 0)
        def _(bit=bit):
            make_copy(0, bit).wait()

        bit *= 2


def _rows(unit_start, units):
    return pl.ds(pl.multiple_of(unit_start * ROW_UNIT, ROW_UNIT), units * ROW_UNIT)


def _moe_route_kernel(x_ref, g_ref, wr_ref, xs_ref, info_ref, tab_ref, tot_ref, xc, run, sem, *, tr, cap, seg_units):
    i = pl.program_id(0)
    nt = pl.num_programs(0)

    @pl.when(i == 0)
    def _():
        for e in range(N_EXPERTS):
            run[e] = 0

    hb = _rms(x_ref[...], g_ref[...]).astype(BF16)
    e1, e2, g1, g2 = _route(jnp.dot(hb, wr_ref[...], preferred_element_type=F32))
    lane = lax.broadcasted_iota(jnp.int32, (tr, LANES), 1).astype(F32)
    a1 = lane == e1
    a2 = lane == e2
    assigned = jnp.where(a1 | a2, 1.0, 0.0)
    earlier = lax.broadcasted_iota(jnp.int32, (tr, tr), 1) < lax.broadcasted_iota(jnp.int32, (tr, tr), 0)
    rank = jnp.dot(jnp.where(earlier, 1.0, 0.0).astype(BF16), assigned.astype(BF16), preferred_element_type=F32)
    count = jnp.sum(assigned, axis=0, keepdims=True)
    units = jnp.floor((count + (ROW_UNIT - 1)) * (1.0 / ROW_UNIT))
    before = lax.broadcasted_iota(jnp.int32, (LANES, LANES), 0) < lax.broadcasted_iota(jnp.int32, (LANES, LANES), 1)
    units8 = jnp.broadcast_to(units, (SUBLANES, LANES)).astype(BF16)
    base = ROW_UNIT * jnp.dot(units8, jnp.where(before, 1.0, 0.0).astype(BF16), preferred_element_type=F32)[0:1]
    slot = base + rank
    slot1 = jnp.sum(jnp.where(a1, slot, 0.0), axis=1, keepdims=True)
    slot2 = jnp.sum(jnp.where(a2, slot, 0.0), axis=1, keepdims=True)
    info_ref[...] = jnp.where(lane == 0, slot1, jnp.where(lane == 1, slot2,
                              jnp.where(lane == 2, g1, jnp.where(lane == 3, g2, 0.0))))
    pos = lax.broadcasted_iota(jnp.int32, (tr, cap), 1).astype(F32)
    onehot_t = jnp.where((pos == slot1) | (pos == slot2), 1.0, 0.0).astype(BF16)
    xc[i % 2] = lax.dot_general(onehot_t, hb, (((0,), (0,)), ((), ())), preferred_element_type=F32).astype(BF16)

    def copies(tile, wait):
        buf = tile % 2
        src = jnp.int32(0)
        for e in range(N_EXPERTS):
            ne = tab_ref[tile * 2 * N_EXPERTS + N_EXPERTS + e]
            dst = e * seg_units + tab_ref[tile * 2 * N_EXPERTS + e]
            if not wait:
                _chunk_dma(ne, lambda off, bit, src=src, dst=dst: pltpu.make_async_copy(
                    xc.at[buf].at[_rows(src + off, bit)], xs_ref.at[_rows(dst + off, bit)], sem.at[buf]))
            src = src + ne
        if wait:
            _wait_units(src, lambda off, bit: pltpu.make_async_copy(
                xc.at[buf].at[_rows(off, bit)], xs_ref.at[_rows(off, bit)], sem.at[buf]), cap // ROW_UNIT)

    for e in range(N_EXPERTS):
        ne = units[0, e].astype(jnp.int32)
        tab_ref[i * 2 * N_EXPERTS + e] = run[e]
        tab_ref[i * 2 * N_EXPERTS + N_EXPERTS + e] = ne
        run[e] = run[e] + ne
    copies(i, False)

    @pl.when(i > 0)
    def _():
        copies(i - 1, True)

    @pl.when(i == nt - 1)
    def _():
        copies(i, True)
        fill = EXPERT_TILE // ROW_UNIT
        xc[0, pl.ds(0, EXPERT_TILE), :] = jnp.zeros((EXPERT_TILE, xc.shape[2]), BF16)
        tails = [pltpu.make_async_copy(xc.at[0].at[_rows(0, fill)],
                                       xs_ref.at[_rows(e * seg_units + run[e], fill)], sem.at[0])
                 for e in range(N_EXPERTS)]
        for cp in tails:
            cp.start()
        for cp in tails:
            cp.wait()
        for e in range(N_EXPERTS):
            tot_ref[e] = run[e]


def _moe_expert_kernel(eo_ref, rb_ref, valid_ref, xs_ref, wg_ref, wu_ref, wd_ref, ys_ref, wg_s, wu_s, wd_s):
    w = pl.program_id(0)

    @pl.when((w == 0) | (eo_ref[w] != eo_ref[jnp.maximum(w - 1, 0)]))
    def _():
        wg_s[...] = wg_ref[0].astype(BF16)
        wu_s[...] = wu_ref[0].astype(BF16)
        wd_s[...] = wd_ref[0].astype(BF16)

    @pl.when(valid_ref[w] == 1)
    def _():
        x = xs_ref[...]
        act = jnp.dot(x, wg_s[...], preferred_element_type=F32)
        act = act * _sigmoid(act) * jnp.dot(x, wu_s[...], preferred_element_type=F32)
        ys_ref[...] = jnp.dot(act.astype(BF16), wd_s[...], preferred_element_type=F32).astype(ys_ref.dtype)


def _moe_combine_kernel(tab_ref, x_ref, info_ref, ys_ref, fg_ref, out_ref, yc, sem, *, tr, cap, seg_units):
    i = pl.program_id(0)

    nt = pl.num_programs(0)

    def copies(tile, wait):
        buf = tile % 2
        dst = jnp.int32(0)
        for e in range(N_EXPERTS):
            ne = tab_ref[tile * 2 * N_EXPERTS + N_EXPERTS + e]
            src = e * seg_units + tab_ref[tile * 2 * N_EXPERTS + e]
            if not wait:
                _chunk_dma(ne, lambda off, bit, src=src, dst=dst: pltpu.make_async_copy(
                    ys_ref.at[_rows(src + off, bit)], yc.at[buf].at[_rows(dst + off, bit)], sem.at[buf]))
            dst = dst + ne
        if wait:
            _wait_units(dst, lambda off, bit: pltpu.make_async_copy(
                ys_ref.at[_rows(off, bit)], yc.at[buf].at[_rows(off, bit)], sem.at[buf]), cap // ROW_UNIT)

    @pl.when(i == 0)
    def _():
        yc[...] = jnp.zeros(yc.shape, yc.dtype)
        copies(i, False)

    @pl.when(i + 1 < nt)
    def _():
        copies(i + 1, False)

    copies(i, True)
    info = info_ref[...]
    pos = lax.broadcasted_iota(jnp.int32, (tr, cap), 1).astype(F32)
    rows = yc[i % 2]
    y1 = jnp.dot(jnp.where(pos == info[:, 0:1], 1.0, 0.0).astype(BF16), rows, preferred_element_type=F32)
    y2 = jnp.dot(jnp.where(pos == info[:, 1:2], 1.0, 0.0).astype(BF16), rows, preferred_element_type=F32)
    out_ref[...] = _rms(x_ref[...] + info[:, 2:3] * y1 + info[:, 3:4] * y2, fg_ref[...])


def moe_sparse_final(x, g, w_router_pad, wg, wu, wd, final_g):
    t, d = x.shape
    tr, te = ROUTE_TILE, EXPERT_TILE
    assert t % tr == 0
    ntiles = t // tr
    cap = _sorted_cap(tr)
    ne, _, ff = wg.shape
    seg_rows = -(-(t + (ROW_UNIT - 1) * ntiles + te) // te) * te
    seg_units = seg_rows // ROW_UNIT
    smem = pl.BlockSpec(memory_space=pltpu.SMEM)

    xs, info, tab, tot = pl.pallas_call(
        functools.partial(_moe_route_kernel, tr=tr, cap=cap, seg_units=seg_units),
        grid=(ntiles,),
        in_specs=[
            pl.BlockSpec((tr, d), lambda i: (i, 0)),
            pl.BlockSpec((1, d), lambda i: (0, 0)),
            pl.BlockSpec((d, LANES), lambda i: (0, 0)),
        ],
        out_specs=[pl.BlockSpec(memory_space=pl.ANY), pl.BlockSpec((tr, LANES), lambda i: (i, 0)), smem, smem],
        out_shape=[
            jax.ShapeDtypeStruct((ne * seg_rows, d), BF16),
            jax.ShapeDtypeStruct((t, LANES), F32),
            jax.ShapeDtypeStruct((ntiles * 2 * ne,), jnp.int32),
            jax.ShapeDtypeStruct((ne,), jnp.int32),
        ],
        scratch_shapes=[pltpu.VMEM((2, cap, d), BF16), pltpu.SMEM((ne,), jnp.int32), pltpu.SemaphoreType.DMA((2,))],
        compiler_params=_params(("arbitrary",)),
        name="moe_route",
    )(x, g.reshape(1, d), w_router_pad)

    tiles_per_e = (tot * ROW_UNIT + te - 1) // te
    ends = jnp.cumsum(tiles_per_e)
    n_items = ends[-1]
    max_items = (2 * t + ne * (ROW_UNIT - 1) * ntiles) // te + ne
    w = jnp.arange(max_items, dtype=jnp.int32)
    wc = jnp.minimum(w, n_items - 1)
    eo = jnp.sum((wc[:, None] >= ends[None, :]).astype(jnp.int32), axis=1)
    rb = (eo * (seg_rows // te) + wc - (ends - tiles_per_e)[eo]).astype(jnp.int32)
    valid = (w < n_items).astype(jnp.int32)

    ys = pl.pallas_call(
        _moe_expert_kernel,
        grid_spec=pltpu.PrefetchScalarGridSpec(
            num_scalar_prefetch=3,
            grid=(max_items,),
            in_specs=[
                pl.BlockSpec((te, d), lambda w, eo, rb, va: (rb[w], 0)),
                pl.BlockSpec((1, d, ff), lambda w, eo, rb, va: (eo[w], 0, 0)),
                pl.BlockSpec((1, d, ff), lambda w, eo, rb, va: (eo[w], 0, 0)),
                pl.BlockSpec((1, ff, d), lambda w, eo, rb, va: (eo[w], 0, 0)),
            ],
            out_specs=pl.BlockSpec((te, d), lambda w, eo, rb, va: (rb[w], 0)),
            scratch_shapes=[pltpu.VMEM((d, ff), BF16), pltpu.VMEM((d, ff), BF16), pltpu.VMEM((ff, d), BF16)],
        ),
        out_shape=jax.ShapeDtypeStruct((ne * seg_rows, d), BF16),
        compiler_params=_params(("arbitrary",)),
        name="moe_experts",
    )(eo, rb, valid, xs, wg, wu, wd)

    return pl.pallas_call(
        functools.partial(_moe_combine_kernel, tr=tr, cap=cap, seg_units=seg_units),
        grid_spec=pltpu.PrefetchScalarGridSpec(
            num_scalar_prefetch=1,
            grid=(ntiles,),
            in_specs=[
                pl.BlockSpec((tr, d), lambda i, tab: (i, 0)),
                pl.BlockSpec((tr, LANES), lambda i, tab: (i, 0)),
                pl.BlockSpec(memory_space=pl.ANY),
                pl.BlockSpec((1, d), lambda i, tab: (0, 0)),
            ],
            out_specs=pl.BlockSpec((tr, d), lambda i, tab: (i, 0)),
            scratch_shapes=[pltpu.VMEM((2, cap, d), BF16), pltpu.SemaphoreType.DMA((2,))],
        ),
        out_shape=jax.ShapeDtypeStruct((t, d), F32),
        compiler_params=_params(("arbitrary",)),
        name="moe_combine",
    )(tab, x, info, ys, final_g.reshape(1, d))


def _trunk(x, mem_k, mem_v, conv_buf, h0, past_k, past_v, p, lam_init):
    b, t, d = x.shape
    n = b * t
    xf = x.reshape(n, d)
    aw = DIFF_HEADS * DIFF_VD
    lru_w = p["lru_lambda"].shape[0]
    precise = past_k is not None
    tm = n if precise else 512
    tn = 1024
    assert t >= CONV_W - 1
    seq = lambda a: a.reshape(b, t, a.shape[-1])
    lam_params = (p["lam_q1"], p["lam_k1"], p["lam_q2"], p["lam_k2"])
    lru_args = (p["conv_w"], p["conv_b"], p["lru_wa"], p["lru_ba"].reshape(-1), p["lru_wx"],
                p["lru_bx"].reshape(-1), p["lru_lambda"])

    if precise:
        xb, gate, q, k, v = norm_linear_f32(xf, p["norm_mix_g"], p["w_in_f32"], tn=tn, split=True)
    else:
        xb, gate, q, k, v = norm_linear(xf, p["norm_mix_g"], p["w_in"], tm=tm, tn=tn,
                                        out_widths=[lru_w, lru_w, aw, aw, aw], out_dtypes=[F32] * 5)
    lru_out, h_last = lru_mixer(seq(xb), seq(gate), conv_buf, h0, *lru_args, tc=256, precise=precise)
    if precise:
        att = diff_attention_sample(seq(q), seq(k), seq(v), past_k, past_v, lam_params, p["subln_g"], lam_init)
    else:
        att = diff_attention_prompt(seq(q), seq(k), seq(v), lam_params, p["subln_g"], lam_init, tq=512)
    mix_in = [lru_out.reshape(n, lru_w), att.reshape(n, aw)]
    if precise:
        x1 = linear_residual_f32(xf, mix_in, p["w_out_f32"], tn=tn)
        (qx,) = norm_linear_f32(x1, p["norm_cross_g"], p["xq_w_f32"], tn=tn, split=False)
    else:
        x1 = linear_residual(xf, mix_in, p["w_out"], tm=tm, tn=tn)
        (qx,) = norm_linear(x1, p["norm_cross_g"], p["xq_w"], tm=tm, tn=tn, out_widths=[d], out_dtypes=[BF16])
    o = cross_attention_core(seq(qx), mem_k, mem_v, tq=512, precise="native" if precise else False)
    if precise:
        x2 = linear_residual_f32(x1, [o.reshape(n, d)], p["xo_w_f32"], tn=tn)
    else:
        x2 = linear_residual(x1, [o.reshape(n, d)], p["xo_w"], tm=tm, tn=tn)

    experts = (p["exp_gate"], p["exp_up"], p["exp_down"])
    if precise:
        y = moe_dense_final(x2, p["norm_ffn_g"], p["router_pad_f32"], *experts, p["final_norm_g"], tm=tm,
                            precise_router=True)
    else:
        y = moe_sparse_final(x2, p["norm_ffn_g"], p["router_pad"], *experts, p["final_norm_g"])
    new_conv = seq(xb)[:, t - (CONV_W - 1):, :]
    return y.reshape(b, t, d), new_conv, h_last, k, v


def kernel(x_prompt, x_sample, cache_diff_k, cache_diff_v, cache_mem_k, cache_mem_v, state_conv, state_lru, mem_prompt, norm_mix_g, w_in, conv_w, conv_b, lru_wa, lru_ba, lru_wx, lru_bx, lru_lambda, lam_q1, lam_k1, lam_q2, lam_k2, subln_g, w_out, norm_cross_g, norm_mem_g, xq_w, xk_w, xv_w, xo_w, norm_ffn_g, router_group_w, router_expert_w, exp_gate, exp_up, exp_down, final_norm_g):
    depth = w_in.shape[0]
    assert depth == 1, "single-layer step"
    bp, tp, d = x_prompt.shape
    bs, ts, _ = x_sample.shape
    past = cache_diff_k.shape[2]
    n_mem = mem_prompt.shape[1]
    aw = DIFF_HEADS * DIFF_VD
    l = 0
    lam_init = 0.8 - 0.6 * math.exp(-0.3 * l)

    router = jnp.concatenate([router_group_w[l], router_expert_w[l]], axis=1)
    router_pad_f32 = jnp.pad(router, ((0, 0), (0, LANES - router.shape[1])))
    router_pad = router_pad_f32.astype(BF16)
    p = dict(router_pad_f32=router_pad_f32, w_in_f32=w_in[l], w_out_f32=w_out[l], xq_w_f32=xq_w[l],
             xo_w_f32=xo_w[l], **dict(norm_mix_g=norm_mix_g[l], conv_w=conv_w[l], conv_b=conv_b[l], lru_wa=lru_wa[l],
             lru_ba=lru_ba[l], lru_wx=lru_wx[l], lru_bx=lru_bx[l], lru_lambda=lru_lambda[l], lam_q1=lam_q1[l],
             lam_k1=lam_k1[l], lam_q2=lam_q2[l], lam_k2=lam_k2[l], subln_g=subln_g[l],
             norm_cross_g=norm_cross_g[l], norm_ffn_g=norm_ffn_g[l], router_pad=router_pad,
             final_norm_g=final_norm_g))
    for name, w in (("w_in", w_in), ("w_out", w_out), ("xq_w", xq_w), ("xo_w", xo_w)):
        p[name] = w[l].astype(BF16)
    p.update(exp_gate=exp_gate[l], exp_up=exp_up[l], exp_down=exp_down[l])

    memf = mem_prompt.reshape(bp * n_mem, d)
    w_mem = jnp.concatenate([xk_w[l].astype(BF16), xv_w[l].astype(BF16)], axis=1)
    mk_p, mv_p = norm_linear(memf, norm_mem_g[l], w_mem, tm=512, tn=1024, out_widths=[d, d], out_dtypes=[F32, F32])
    mk_p = mk_p.reshape(bp, n_mem, d)
    mv_p = mv_p.reshape(bp, n_mem, d)

    zero_buf = jnp.zeros((bp, CONV_W - 1, lru_lambda.shape[1]), F32)
    zero_h = jnp.zeros((bp, lru_lambda.shape[1]), F32)
    y_p, cb_p, hl_p, k_p, v_p = _trunk(x_prompt, mk_p, mv_p, zero_buf, zero_h, None, None, p, lam_init)
    y_s, cb_s, hl_s, k_s, v_s = _trunk(x_sample, cache_mem_k[l], cache_mem_v[l], state_conv[l], state_lru[l],
                                       cache_diff_k[l], cache_diff_v[l],
                                       p, lam_init)

    hd2 = 2 * DIFF_HD
    return (y_p, y_s,
            k_p.reshape(1, bp, tp, DIFF_HEADS, hd2), v_p.reshape(1, bp, tp, DIFF_HEADS, DIFF_VD),
            mk_p.reshape(1, bp, n_mem, X_HEADS, d // X_HEADS), mv_p.reshape(1, bp, n_mem, X_HEADS, d // X_HEADS),
            cb_p[None], hl_p[None],
            k_s.reshape(1, bs, ts, DIFF_HEADS, hd2), v_s.reshape(1, bs, ts, DIFF_HEADS, DIFF_VD),
            cb_s[None], hl_s[None].astype(state_lru.dtype))
```

```python
import functools
import math

import jax
import jax.numpy as jnp
from jax import lax
from jax.experimental import pallas as pl
from jax.experimental.pallas import tpu as pltpu

F32 = jnp.float32
BF16 = jnp.bfloat16

CHUNK = 64
CONV_W = 4
LRU_C = 8.0
LRU_BLOCK = 64
DIFF_HEADS = 8
DIFF_HD = 64
DIFF_VD = 2 * DIFF_HD
X_HEADS = 4
N_GROUPS = 4
EXP_PER_GROUP = 4
N_EXPERTS = N_GROUPS * EXP_PER_GROUP
EPS = 1e-6

LANES = 128
SUBLANES = 8
MXU_DIM = 256
VMEM_LIMIT_BYTES = 56 * 1024 * 1024


def _params(semantics):
    return pltpu.CompilerParams(dimension_semantics=semantics, vmem_limit_bytes=VMEM_LIMIT_BYTES)


def _split(a):
    a = a.astype(F32)
    hi = a.astype(BF16)
    return hi, (a - hi.astype(F32)).astype(BF16)


def _dot_dims(a, b, dims, precise):
    if not precise:
        return lax.dot_general(a.astype(BF16), b.astype(BF16), dims, preferred_element_type=F32)
    if precise == "native":
        return lax.dot_general(a.astype(F32), b.astype(F32), dims, precision=lax.Precision.HIGHEST,
                               preferred_element_type=F32)
    ah, al = _split(a)
    bh, bl = _split(b)
    mm = lambda x, y: lax.dot_general(x, y, dims, preferred_element_type=F32)
    return mm(ah, bh) + (mm(ah, bl) + mm(al, bh))


def _dot(a, b, precise=False):
    return _dot_dims(a, b, (((1,), (0,)), ((), ())), precise)


def _dot_nt(a, b, precise=False):
    return _dot_dims(a, b, (((1,), (1,)), ((), ())), precise)


def _rms(x, g):
    return x * lax.rsqrt(jnp.mean(x * x, axis=-1, keepdims=True) + EPS) * g


def _sigmoid(x):
    return 1.0 / (1.0 + jnp.exp(-x))


def _gelu_tanh(x):
    c = math.sqrt(2.0 / math.pi)
    return 0.5 * x * (1.0 + jnp.tanh(c * (x + 0.044715 * (x * x * x))))


def _norm_linear_kernel(x_ref, g_ref, w_ref, *out_refs, tn):
    h = _rms(x_ref[...], g_ref[...]).astype(BF16)
    col = 0
    for o_ref in out_refs:
        for c in range(o_ref.shape[1] // tn):
            o_ref[:, c * tn:(c + 1) * tn] = jnp.dot(
                h, w_ref[:, col:col + tn], preferred_element_type=F32).astype(o_ref.dtype)
            col += tn


def norm_linear(x, g, w, *, tm, tn, out_widths, out_dtypes):
    t, k = x.shape
    n = w.shape[1]
    tm = min(tm, t)
    assert sum(out_widths) == n and all(wd % tn == 0 for wd in out_widths)
    return pl.pallas_call(
        functools.partial(_norm_linear_kernel, tn=tn),
        grid=(t // tm,),
        in_specs=[
            pl.BlockSpec((tm, k), lambda i: (i, 0)),
            pl.BlockSpec((1, k), lambda i: (0, 0)),
            pl.BlockSpec((k, n), lambda i: (0, 0)),
        ],
        out_specs=[pl.BlockSpec((tm, wd), lambda i: (i, 0)) for wd in out_widths],
        out_shape=[jax.ShapeDtypeStruct((t, wd), dt) for wd, dt in zip(out_widths, out_dtypes)],
        compiler_params=_params(("parallel",)),
        name="norm_linear",
    )(x, g.reshape(1, k), w)


def _linear_res_norm_linear_kernel(*refs, n_in, tn):
    res_ref = refs[0]
    a_refs = refs[1:1 + n_in]
    w1_ref, g_ref, w2_ref, x1_ref, y_ref = refs[1 + n_in:]
    kc = a_refs[0].shape[1]
    for c in range(x1_ref.shape[1] // tn):
        cols = slice(c * tn, (c + 1) * tn)
        acc = res_ref[:, cols]
        for r, a_ref in enumerate(a_refs):
            acc = acc + _dot(a_ref[...], w1_ref[r * kc:(r + 1) * kc, cols])
        x1_ref[:, cols] = acc
    h = _rms(x1_ref[...], g_ref[...]).astype(BF16)
    for c in range(y_ref.shape[1] // tn):
        cols = slice(c * tn, (c + 1) * tn)
        y_ref[:, cols] = jnp.dot(h, w2_ref[:, cols], preferred_element_type=F32).astype(y_ref.dtype)


def linear_residual_norm_linear(res, a_list, w1, g, w2, *, tm, tn):
    t, n = res.shape
    tm = min(tm, t)
    n_in = len(a_list)
    kc = a_list[0].shape[1]
    n2 = w2.shape[1]
    in_specs = [pl.BlockSpec((tm, n), lambda i: (i, 0))]
    in_specs += [pl.BlockSpec((tm, kc), lambda i: (i, 0)) for _ in range(n_in)]
    in_specs += [pl.BlockSpec(w1.shape, lambda i: (0, 0)), pl.BlockSpec((1, n), lambda i: (0, 0)),
                 pl.BlockSpec(w2.shape, lambda i: (0, 0))]
    return pl.pallas_call(
        functools.partial(_linear_res_norm_linear_kernel, n_in=n_in, tn=tn),
        grid=(t // tm,),
        in_specs=in_specs,
        out_specs=[pl.BlockSpec((tm, n), lambda i: (i, 0)), pl.BlockSpec((tm, n2), lambda i: (i, 0))],
        out_shape=[jax.ShapeDtypeStruct((t, n), F32), jax.ShapeDtypeStruct((t, n2), BF16)],
        compiler_params=_params(("parallel",)),
        name="linear_residual_norm_linear",
    )(res, *a_list, w1, g.reshape(1, n), w2)


def _norm_linear_f32_kernel(x_ref, g_ref, w_ref, *refs):
    out_refs, h_ref = refs[:-1], refs[-1]
    j = pl.program_id(1)

    @pl.when(j == 0)
    def _():
        h_ref[...] = _rms(x_ref[...], g_ref[...])

    if len(out_refs) == 1:
        out_refs[0][...] = _dot(h_ref[...], w_ref[...], True)
    else:
        for c, o_ref in enumerate(out_refs):
            @pl.when(j == c)
            def _(o_ref=o_ref):
                o_ref[...] = _dot(h_ref[...], w_ref[...], True)


def norm_linear_f32(x, g, w, *, tn, split):
    t, k = x.shape
    n = w.shape[1]
    nj = n // tn
    if split:
        out_shape = [jax.ShapeDtypeStruct((t, tn), F32) for _ in range(nj)]
        out_specs = [pl.BlockSpec((t, tn), lambda i, j: (i, 0)) for _ in range(nj)]
    else:
        out_shape = [jax.ShapeDtypeStruct((t, n), F32)]
        out_specs = [pl.BlockSpec((t, tn), lambda i, j: (i, j))]
    return pl.pallas_call(
        _norm_linear_f32_kernel,
        grid=(1, nj),
        in_specs=[
            pl.BlockSpec((t, k), lambda i, j: (i, 0)),
            pl.BlockSpec((1, k), lambda i, j: (0, 0)),
            pl.BlockSpec((k, tn), lambda i, j: (0, j)),
        ],
        out_specs=out_specs,
        out_shape=out_shape,
        scratch_shapes=[pltpu.VMEM((t, k), F32)],
        compiler_params=_params(("parallel", "arbitrary")),
        name="norm_linear_f32",
    )(x, g.reshape(1, k), w)


def _linear_res_f32_kernel(*refs, n_in):
    res_ref = refs[0]
    a_refs = refs[1:1 + n_in]
    w_refs = refs[1 + n_in:1 + 2 * n_in]
    out_ref = refs[1 + 2 * n_in]
    acc = res_ref[...]
    for a_ref, w_ref in zip(a_refs, w_refs):
        acc = acc + _dot(a_ref[...], w_ref[...], True)
    out_ref[...] = acc


def linear_residual_f32(res, a_list, w, *, tn):
    t, n = res.shape
    n_in = len(a_list)
    kc = a_list[0].shape[1]
    in_specs = [pl.BlockSpec((t, tn), lambda i, j: (i, j))]
    in_specs += [pl.BlockSpec((t, kc), lambda i, j: (i, 0)) for _ in range(n_in)]
    in_specs += [pl.BlockSpec((kc, tn), lambda i, j, c=c: (c, j)) for c in range(n_in)]
    return pl.pallas_call(
        functools.partial(_linear_res_f32_kernel, n_in=n_in),
        grid=(1, n // tn),
        in_specs=in_specs,
        out_specs=pl.BlockSpec((t, tn), lambda i, j: (i, j)),
        out_shape=jax.ShapeDtypeStruct((t, n), F32),
        compiler_params=_params(("parallel", "arbitrary")),
        name="linear_residual_f32",
    )(res, *a_list, *([w] * n_in))


def _lru_kernel(xb_ref, gate_ref, cbuf_ref, h0_ref, cw_ref, cb_ref, wa_ref, ba_ref, wx_ref, bx_ref, lam_ref,
                out_ref, hlast_ref, xpad, hcar, a_s, u_s, *, tc, width, precise):
    c = pl.program_id(1)
    nslab = width // MXU_DIM
    ngrp = tc // SUBLANES

    @pl.when(c == 0)
    def _():
        xpad[pl.ds(0, SUBLANES), :] = cbuf_ref[0]
        hcar[...] = h0_ref[0]

    xpad[pl.ds(SUBLANES, tc), :] = xb_ref[0]
    xc = cb_ref[...] + cw_ref[pl.ds(CONV_W - 1, 1), :] * xpad[pl.ds(SUBLANES, tc), :]
    for j in range(CONV_W - 1):
        xc = xc + cw_ref[pl.ds(j, 1), :] * xpad[pl.ds(SUBLANES - (CONV_W - 1) + j, tc), :]
    xpad[pl.ds(0, SUBLANES), :] = xpad[pl.ds(tc, SUBLANES), :]

    lam = lam_ref[...]
    softplus_neg = jnp.maximum(-lam, 0.0) + jnp.log1p(jnp.exp(-jnp.abs(lam)))
    c8 = -LRU_C * softplus_neg

    sub = lax.broadcasted_iota(jnp.int32, (ngrp, SUBLANES, MXU_DIM), 1)
    for s in range(nslab):
        cols = slice(s * MXU_DIM, (s + 1) * MXU_DIM)
        xs = xc[:, cols]
        r = _sigmoid(_dot(xs, wa_ref[s], precise) + ba_ref[:, cols])
        i = _sigmoid(_dot(xs, wx_ref[s], precise) + bx_ref[:, cols])
        a = jnp.exp(c8[:, cols] * r)
        u = jnp.sqrt(1.0 - a * a) * (i * xs)
        a3 = a.reshape(ngrp, SUBLANES, MXU_DIM)
        u3 = u.reshape(ngrp, SUBLANES, MXU_DIM)
        d = 1
        while d < SUBLANES:
            a_sh = pltpu.roll(a3, d, 1)
            u_sh = pltpu.roll(u3, d, 1)
            keep = sub >= d
            u3 = jnp.where(keep, u3 + a3 * u_sh, u3)
            a3 = jnp.where(keep, a3 * a_sh, a3)
            d *= 2
        a_s[:, cols] = a3.reshape(tc, MXU_DIM)
        u_s[:, cols] = u3.reshape(tc, MXU_DIM)

    def body(g, hin):
        rows = pl.ds(pl.multiple_of(g * SUBLANES, SUBLANES), SUBLANES)
        h = u_s[rows, :] + a_s[rows, :] * hin
        u_s[rows, :] = h
        return h[SUBLANES - 1:SUBLANES, :]

    hfin = lax.fori_loop(0, ngrp, body, hcar[...])
    hcar[...] = hfin
    out_ref[0] = (u_s[...] * _gelu_tanh(gate_ref[0])).astype(out_ref.dtype)
    hlast_ref[0] = hfin


def _block_diag(w, per):
    nb, k, _ = w.shape
    w4 = w.reshape(nb // per, per, k, k)
    eye = jnp.eye(per, dtype=w.dtype)
    return jnp.einsum("cipq,ij->cipjq", w4, eye).reshape(nb // per, per * k, per * k)


def lru_mixer(xb, gate, conv_buf, h0, conv_w, conv_b, wa, ba, wx, bx, lam, *, tc, precise):
    b, t, width = xb.shape
    tc = min(tc, t)
    per = MXU_DIM // LRU_BLOCK
    act_dt = F32 if precise else BF16
    wa_bd = _block_diag(wa, per).astype(act_dt)
    wx_bd = _block_diag(wx, per).astype(act_dt)
    nslab = wa_bd.shape[0]
    cbuf8 = jnp.concatenate([jnp.zeros((b, SUBLANES - (CONV_W - 1), width), F32), conv_buf.astype(F32)], axis=1)
    row = lambda v: v.reshape(1, width).astype(F32)
    vec_spec = pl.BlockSpec((1, width), lambda bi, ci: (0, 0))
    seq_spec = pl.BlockSpec((1, tc, width), lambda bi, ci: (bi, ci, 0))
    wspec = pl.BlockSpec((nslab, MXU_DIM, MXU_DIM), lambda bi, ci: (0, 0, 0))
    out, hlast = pl.pallas_call(
        functools.partial(_lru_kernel, tc=tc, width=width, precise=precise),
        grid=(b, t // tc),
        in_specs=[
            seq_spec, seq_spec,
            pl.BlockSpec((1, SUBLANES, width), lambda bi, ci: (bi, 0, 0)),
            pl.BlockSpec((1, 1, width), lambda bi, ci: (bi, 0, 0)),
            pl.BlockSpec((CONV_W, width), lambda bi, ci: (0, 0)),
            vec_spec, wspec, vec_spec, wspec, vec_spec, vec_spec,
        ],
        out_specs=[seq_spec, pl.BlockSpec((1, 1, width), lambda bi, ci: (bi, 0, 0))],
        out_shape=[jax.ShapeDtypeStruct((b, t, width), act_dt), jax.ShapeDtypeStruct((b, 1, width), F32)],
        scratch_shapes=[
            pltpu.VMEM((tc + SUBLANES, width), F32),
            pltpu.VMEM((1, width), F32),
            pltpu.VMEM((tc, width), F32),
            pltpu.VMEM((tc, width), F32),
        ],
        compiler_params=_params(("parallel", "arbitrary")),
        name="lru_mixer",
    )(xb, gate, cbuf8, h0.reshape(b, 1, width).astype(F32), conv_w.astype(F32), row(conv_b), wa_bd,
      row(ba), wx_bd, row(bx), row(lam))
    return out, hlast.reshape(b, width)


def _diff_lambda(lq1, lk1, lq2, lk2, lam_init):
    s1 = jnp.sum(lq1[...] * lk1[...], axis=-1, keepdims=True)
    s2 = jnp.sum(lq2[...] * lk2[...], axis=-1, keepdims=True)
    return jnp.exp(s1) - jnp.exp(s2) + lam_init


_LOG2E_PARTS = (1.4453125, -0.00262451171875, 7.063150405883789e-06)
LOG2E = sum(_LOG2E_PARTS)
N_BIAS_COLS = 2 * len(_LOG2E_PARTS)
ATTN_GROUP = 4


def _attn_prompt_kernel(slopes_ref, q_ref, k_ref, v_ref, lq1, lk1, lq2, lk2, subln_ref, out_ref,
                        kaug, vt, qt, m_s, l_s, acc_s, s_s, diag_s, *, tq, lam_init):
    h = pl.program_id(1)
    qi = pl.program_id(2)
    slope = slopes_ref[h]
    nblk, tk, _ = kaug.shape
    nparts = len(_LOG2E_PARTS)

    @pl.when(qi == 0)
    def _():
        pos = lax.broadcasted_iota(jnp.int32, (tk, LANES), 0)
        lane = lax.broadcasted_iota(jnp.int32, (tk, LANES), 1)
        within = (pos % CHUNK).astype(F32) * slope
        for j in range(nblk):
            rows = slice(j * tk, (j + 1) * tk)
            kaug[j, :, 0:LANES] = k_ref[0, rows, :].astype(BF16)
            coarse = ((pos + j * tk) // CHUNK * CHUNK).astype(F32) * slope
            cols = jnp.where(lane < nparts, coarse, jnp.where(lane < N_BIAS_COLS, within, 0.0))
            kaug[j, :, LANES:2 * LANES] = cols.astype(BF16)
            vt[j] = v_ref[0, rows, :].T.astype(BF16)
        r = lax.broadcasted_iota(jnp.int32, (LANES, tq), 0)
        part = jnp.where(r % nparts == 0, _LOG2E_PARTS[0],
                         jnp.where(r % nparts == 1, _LOG2E_PARTS[1], _LOG2E_PARTS[2]))
        const_rows = jnp.where(r < N_BIAS_COLS, part, 0.0).astype(BF16)
        qt[0, LANES:2 * LANES, :] = const_rows
        qt[1, LANES:2 * LANES, :] = const_rows
        kpos = lax.broadcasted_iota(jnp.int32, (tk, tq), 0)
        qpos = lax.broadcasted_iota(jnp.int32, (tk, tq), 1)
        fix = jnp.where(kpos > qpos, (-2.0 * LOG2E) * slope * (kpos - qpos).astype(F32), 0.0)
        diag_s[...] = jnp.where((kpos // CHUNK) <= (qpos // CHUNK), fix, -jnp.inf)

    qs = q_ref[0] * (LOG2E * DIFF_HD ** -0.5)
    qlane = lax.broadcasted_iota(jnp.int32, qs.shape, 1)
    qt[0, 0:LANES, :] = jnp.where(qlane < DIFF_HD, qs, 0.0).T.astype(BF16)
    qt[1, 0:LANES, :] = jnp.where(qlane >= DIFF_HD, qs, 0.0).T.astype(BF16)
    m_s[...] = jnp.full(m_s.shape, -jnp.inf, F32)
    l_s[...] = jnp.zeros(l_s.shape, F32)
    acc_s[...] = jnp.zeros(acc_s.shape, F32)

    def blocks(kis, diagonal):
        slab = MXU_DIM
        colmax = [[None, None] for _ in kis]
        for j, ki in enumerate(kis):
            last = diagonal and j == len(kis) - 1
            for m in range(2):
                for r in range(tk // slab):
                    rows = slice(r * slab, (r + 1) * slab)
                    s = jnp.dot(kaug[ki, rows, :], qt[m], preferred_element_type=F32)
                    if last:
                        s = s + diag_s[rows, :]
                    s_s[j, m, rows, :] = s
                    pm = jnp.max(s, axis=0, keepdims=True)
                    colmax[j][m] = pm if r == 0 else jnp.maximum(colmax[j][m], pm)
        for j, ki in enumerate(kis):
            vblk = vt[ki]
            for m in range(2):
                m_old = m_s[m]
                m_new = jnp.maximum(m_old, colmax[j][m])
                p = jnp.exp2(s_s[j, m] - m_new)
                alpha = jnp.exp2(m_old - m_new)
                l_s[m] = alpha * l_s[m] + jnp.sum(p, axis=0, keepdims=True)
                acc_s[m] = alpha * acc_s[m] + jnp.dot(vblk, p.astype(BF16), preferred_element_type=F32)
                m_s[m] = m_new

    group = s_s.shape[0]

    def group_body(j, carry):
        blocks([group * j + t for t in range(group)], False)
        return carry

    lax.fori_loop(0, qi // group, group_body, 0)
    for r in range(group):
        @pl.when(qi % group == r)
        def _(r=r):
            blocks([qi - r + t for t in range(r + 1)], True)

    lam = _diff_lambda(lq1, lk1, lq2, lk2, lam_init)
    o = acc_s[0] / l_s[0] - lam * (acc_s[1] / l_s[1])
    o = o * lax.rsqrt(jnp.mean(o * o, axis=0, keepdims=True) + EPS) * subln_ref[...] * (1.0 - lam_init)
    out_ref[0] = o.T.astype(out_ref.dtype)


def _alibi_slopes():
    return 2.0 ** (-8.0 * jnp.arange(1, DIFF_HEADS + 1, dtype=F32) / DIFF_HEADS)


def diff_attention_prompt(q, k, v, lam_params, subln_g, lam_init, *, tq):
    b, t, aw = q.shape
    tq = min(tq, t)
    hd2 = 2 * DIFF_HD
    assert hd2 == LANES and DIFF_VD == LANES and tq % CHUNK == 0
    assert t // CHUNK <= 256, "chunk index must stay exact in bf16"
    lrow = lambda p: p.reshape(1, DIFF_HD).astype(F32)
    lspec = pl.BlockSpec((1, DIFF_HD), lambda bi, hi, qi: (0, 0))
    kv_spec = pl.BlockSpec((1, t, hd2), lambda bi, hi, qi: (bi, 0, hi))
    q_spec = pl.BlockSpec((1, tq, hd2), lambda bi, hi, qi: (bi, qi, hi))
    return pl.pallas_call(
        functools.partial(_attn_prompt_kernel, tq=tq, lam_init=lam_init),
        grid=(b, DIFF_HEADS, t // tq),
        in_specs=[
            pl.BlockSpec(memory_space=pltpu.SMEM),
            q_spec, kv_spec, kv_spec, lspec, lspec, lspec, lspec,
            pl.BlockSpec((DIFF_VD, 1), lambda bi, hi, qi: (0, 0)),
        ],
        out_specs=q_spec,
        out_shape=jax.ShapeDtypeStruct((b, t, aw), BF16),
        scratch_shapes=[
            pltpu.VMEM((t // tq, tq, 2 * LANES), BF16),
            pltpu.VMEM((t // tq, DIFF_VD, tq), BF16),
            pltpu.VMEM((2, 2 * LANES, tq), BF16),
            pltpu.VMEM((2, 1, tq), F32),
            pltpu.VMEM((2, 1, tq), F32),
            pltpu.VMEM((2, DIFF_VD, tq), F32),
            pltpu.VMEM((ATTN_GROUP, 2, tq, tq), F32),
            pltpu.VMEM((tq, tq), F32),
        ],
        compiler_params=_params(("parallel", "parallel", "arbitrary")),
        name="diff_attention_prompt",
    )(_alibi_slopes(), q, k, v, *[lrow(p) for p in lam_params], subln_g.reshape(DIFF_VD, 1).astype(F32))


def _attn_sample_kernel(q_ref, kp_ref, vp_ref, kn_ref, vn_ref, lq1, lk1, lq2, lk2, subln_ref, out_ref,
                        *, past, tq, lam_init):
    hd2 = 2 * DIFF_HD

    def bias_mask(nk, k_off):
        qpos = past + lax.broadcasted_iota(jnp.int32, (tq, nk), 0)
        kpos = k_off + lax.broadcasted_iota(jnp.int32, (tq, nk), 1)
        dist = jnp.abs(qpos - kpos).astype(F32)
        allowed = (kpos // CHUNK) <= (qpos // CHUNK)
        return dist, allowed

    dist_p, ok_p = bias_mask(past, 0)
    dist_n, ok_n = bias_mask(tq, past)
    lam = _diff_lambda(lq1, lk1, lq2, lk2, lam_init)
    qlane = lax.broadcasted_iota(jnp.int32, (tq, hd2), 1)
    for h in range(DIFF_HEADS):
        slope = 2.0 ** (-8.0 * (h + 1) / DIFF_HEADS)
        cols = slice(h * hd2, (h + 1) * hd2)
        q = q_ref[0, :, cols]
        kp, vp = kp_ref[0, :, h, :], vp_ref[0, :, h, :]
        kn, vn = kn_ref[0, :, cols], vn_ref[0, :, cols]
        probs = []
        q2 = jnp.concatenate([jnp.where(qlane < DIFF_HD, q, 0.0), jnp.where(qlane >= DIFF_HD, q, 0.0)], axis=0)
        sp2 = _dot_nt(q2, kp, "native") * (DIFF_HD ** -0.5)
        sn2 = _dot_nt(q2, kn, "native") * (DIFF_HD ** -0.5)
        for m in range(2):
            rows = slice(m * tq, (m + 1) * tq)
            sp = jnp.where(ok_p, sp2[rows] - slope * dist_p, -jnp.inf)
            sn = jnp.where(ok_n, sn2[rows] - slope * dist_n, -jnp.inf)
            mx = jnp.maximum(jnp.max(sp, axis=-1, keepdims=True), jnp.max(sn, axis=-1, keepdims=True))
            pp = jnp.exp(sp - mx)
            pn = jnp.exp(sn - mx)
            l = jnp.sum(pp, axis=-1, keepdims=True) + jnp.sum(pn, axis=-1, keepdims=True)
            probs.append((pp / l, pn / l))
        o = (_dot(probs[0][0] - lam * probs[1][0], vp, "native")
             + _dot(probs[0][1] - lam * probs[1][1], vn, "native"))
        out_ref[0, :, cols] = (o * lax.rsqrt(jnp.mean(o * o, axis=-1, keepdims=True) + EPS) * subln_ref[...]
                               * (1.0 - lam_init)).astype(out_ref.dtype)


def diff_attention_sample(q, k_new, v_new, past_k, past_v, lam_params, subln_g, lam_init):
    b, t, aw = q.shape
    past = past_k.shape[1]
    lrow = lambda p: p.reshape(1, DIFF_HD).astype(F32)
    lspec = pl.BlockSpec((1, DIFF_HD), lambda bi: (0, 0))
    new_spec = pl.BlockSpec((1, t, aw), lambda bi: (bi, 0, 0))
    past_spec = pl.BlockSpec((1, past, DIFF_HEADS, DIFF_VD), lambda bi: (bi, 0, 0, 0))
    return pl.pallas_call(
        functools.partial(_attn_sample_kernel, past=past, tq=t, lam_init=lam_init),
        grid=(b,),
        in_specs=[
            new_spec, past_spec, past_spec, new_spec, new_spec, lspec, lspec, lspec, lspec,
            pl.BlockSpec((1, DIFF_VD), lambda bi: (0, 0)),
        ],
        out_specs=new_spec,
        out_shape=jax.ShapeDtypeStruct((b, t, aw), F32),
        compiler_params=_params(("parallel",)),
        name="diff_attention_sample",
    )(q, past_k, past_v, k_new, v_new, *[lrow(p) for p in lam_params], subln_g.reshape(1, DIFF_VD).astype(F32))


def _cross_kernel(q_ref, mk_ref, mv_ref, out_ref, *, precise):
    d = q_ref.shape[-1]
    hd = d // X_HEADS
    per_head = len(mk_ref.shape) == 4
    for hh in range(X_HEADS):
        cols = slice(hh * hd, (hh + 1) * hd)
        mk = mk_ref[0, :, hh, :] if per_head else mk_ref[0, :, cols]
        mv = mv_ref[0, :, hh, :] if per_head else mv_ref[0, :, cols]
        s = _dot_nt(q_ref[0, :, cols], mk, precise) * (hd ** -0.5)
        p = jnp.exp(s - jnp.max(s, axis=-1, keepdims=True))
        p = p / jnp.sum(p, axis=-1, keepdims=True)
        out_ref[0, :, cols] = _dot(p, mv, precise).astype(out_ref.dtype)


def cross_attention_core(q, mk, mv, *, tq, precise):
    b, t, d = q.shape
    nm = mk.shape[1]
    tq = min(tq, t)
    q_spec = pl.BlockSpec((1, tq, d), lambda bi, qi: (bi, qi, 0))
    m_spec = pl.BlockSpec((1,) + mk.shape[1:], lambda bi, qi: (bi,) + (0,) * (mk.ndim - 1))
    return pl.pallas_call(
        functools.partial(_cross_kernel, precise=precise),
        grid=(b, t // tq),
        in_specs=[q_spec, m_spec, m_spec],
        out_specs=q_spec,
        out_shape=jax.ShapeDtypeStruct((b, t, d), F32 if precise else BF16),
        compiler_params=_params(("parallel", "arbitrary")),
        name="cross_attention_core",
    )(q, mk, mv)


def _first_argmax(vals, lane, valid):
    masked = jnp.where(valid, vals, -jnp.inf)
    mx = jnp.max(masked, axis=-1, keepdims=True)
    idx = jnp.min(jnp.where(masked == mx, lane, LANES), axis=-1, keepdims=True)
    return mx, idx


def _route(logits):
    lane = lax.broadcasted_iota(jnp.int32, logits.shape, 1).astype(F32)
    is_group = lane < N_GROUPS
    gmax, gidx = _first_argmax(logits, lane, is_group)
    gsum = jnp.sum(jnp.where(is_group, jnp.exp(logits - gmax), 0.0), axis=-1, keepdims=True)
    g_top = 1.0 / gsum
    lo = N_GROUPS + gidx * EXP_PER_GROUP
    in_group = (lane >= lo) & (lane < lo + EXP_PER_GROUP)
    e1, i1 = _first_argmax(logits, lane, in_group)
    e2, i2 = _first_argmax(logits, lane, in_group & (lane != i1))
    w2 = jnp.exp(e2 - e1)
    gate1 = g_top / (1.0 + w2)
    gate2 = g_top * w2 / (1.0 + w2)
    return i1 - N_GROUPS, i2 - N_GROUPS, gate1, gate2


def _combine_weights(logits):
    lane = lax.broadcasted_iota(jnp.int32, logits.shape, 1).astype(F32)
    x1, x2, gate1, gate2 = _route(logits)
    return jnp.where(lane == x1, gate1, 0.0) + jnp.where(lane == x2, gate2, 0.0)


def _moe_dense_kernel(x_ref, g_ref, wr_ref, wg_ref, wu_ref, wd_ref, fg_ref, out_ref, h_s, comb_s, acc_s,
                      *, precise_router):
    e = pl.program_id(1)

    @pl.when(e == 0)
    def _():
        h = _rms(x_ref[...], g_ref[...])
        h_s[...] = h.astype(BF16)
        comb_s[...] = _combine_weights(_dot(h, wr_ref[...], precise_router))
        acc_s[...] = x_ref[...]

    hb = h_s[...]
    act = _dot(hb, wg_ref[0])
    act = act * _sigmoid(act) * _dot(hb, wu_ref[0])
    y = _dot(act, wd_ref[0])
    lane = lax.broadcasted_iota(jnp.int32, comb_s.shape, 1)
    ce = jnp.sum(jnp.where(lane == e, comb_s[...], 0.0), axis=-1, keepdims=True)
    acc_s[...] += ce * y

    @pl.when(e == pl.num_programs(1) - 1)
    def _():
        out_ref[...] = _rms(acc_s[...], fg_ref[...])


def moe_dense_final(x, g, w_router_pad, wg, wu, wd, final_g, *, tm, precise_router):
    t, d = x.shape
    tm = min(tm, t)
    ne, _, ff = wg.shape
    return pl.pallas_call(
        functools.partial(_moe_dense_kernel, precise_router=precise_router),
        grid=(t // tm, ne),
        in_specs=[
            pl.BlockSpec((tm, d), lambda i, e: (i, 0)),
            pl.BlockSpec((1, d), lambda i, e: (0, 0)),
            pl.BlockSpec((d, LANES), lambda i, e: (0, 0)),
            pl.BlockSpec((1, d, ff), lambda i, e: (e, 0, 0)),
            pl.BlockSpec((1, d, ff), lambda i, e: (e, 0, 0)),
            pl.BlockSpec((1, ff, d), lambda i, e: (e, 0, 0)),
            pl.BlockSpec((1, d), lambda i, e: (0, 0)),
        ],
        out_specs=pl.BlockSpec((tm, d), lambda i, e: (i, 0)),
        out_shape=jax.ShapeDtypeStruct((t, d), F32),
        scratch_shapes=[pltpu.VMEM((tm, d), BF16), pltpu.VMEM((tm, LANES), F32), pltpu.VMEM((tm, d), F32)],
        compiler_params=_params(("parallel", "arbitrary")),
        name="moe_dense_final",
    )(x, g.reshape(1, d), w_router_pad, wg, wu, wd, final_g.reshape(1, d))


ROW_UNIT = 16
ROUTE_TILE = MXU_DIM
EXPERT_TILE = 512
UNIT_BITS = (16, 8, 4, 2, 1)


def _sorted_cap(tr):
    rows = 2 * tr + N_EXPERTS * (ROW_UNIT - 1)
    return -(-rows // MXU_DIM) * MXU_DIM


def _chunk_dma(units, make_copy):
    off = jnp.int32(0)
    for bit in UNIT_BITS:
        take = (units & bit) != 0

        @pl.when(take)
        def _(off=off, bit=bit):
            make_copy(off, bit).start()

        off = off + jnp.where(take, bit, 0)


def _wait_units(total_units, make_copy, max_units):
    bit = 1
    while bit <= max_units:
        @pl.when((total_units & bit) != 0)
        def _(bit=bit):
            make_copy(0, bit).wait()

        bit *= 2


def _rows(unit_start, units):
    return pl.ds(pl.multiple_of(unit_start * ROW_UNIT, ROW_UNIT), units * ROW_UNIT)


def _moe_route_kernel(res_ref, a_ref, wo_ref, g_ref, wr_ref, x_ref, xs_ref, info_ref, tab_ref, tot_ref, xc, run, sem,
                      *, tr, cap, seg_units):
    i = pl.program_id(0)
    nt = pl.num_programs(0)

    @pl.when(i == 0)
    def _():
        for e in range(N_EXPERTS):
            run[e] = 0

    x_ref[...] = res_ref[...] + jnp.dot(a_ref[...], wo_ref[...], preferred_element_type=F32)
    hb = _rms(x_ref[...], g_ref[...]).astype(BF16)
    e1, e2, g1, g2 = _route(jnp.dot(hb, wr_ref[...], preferred_element_type=F32))
    lane = lax.broadcasted_iota(jnp.int32, (tr, LANES), 1).astype(F32)
    a1 = lane == e1
    a2 = lane == e2
    assigned = jnp.where(a1 | a2, 1.0, 0.0)
    earlier = lax.broadcasted_iota(jnp.int32, (tr, tr), 1) < lax.broadcasted_iota(jnp.int32, (tr, tr), 0)
    rank = jnp.dot(jnp.where(earlier, 1.0, 0.0).astype(BF16), assigned.astype(BF16), preferred_element_type=F32)
    count = jnp.sum(assigned, axis=0, keepdims=True)
    units = jnp.floor((count + (ROW_UNIT - 1)) * (1.0 / ROW_UNIT))
    before = lax.broadcasted_iota(jnp.int32, (LANES, LANES), 0) < lax.broadcasted_iota(jnp.int32, (LANES, LANES), 1)
    units8 = jnp.broadcast_to(units, (SUBLANES, LANES)).astype(BF16)
    base = ROW_UNIT * jnp.dot(units8, jnp.where(before, 1.0, 0.0).astype(BF16), preferred_element_type=F32)[0:1]
    slot = base + rank
    slot1 = jnp.sum(jnp.where(a1, slot, 0.0), axis=1, keepdims=True)
    slot2 = jnp.sum(jnp.where(a2, slot, 0.0), axis=1, keepdims=True)
    info_ref[...] = jnp.where(lane == 0, slot1, jnp.where(lane == 1, slot2,
                              jnp.where(lane == 2, g1, jnp.where(lane == 3, g2, 0.0))))
    pos = lax.broadcasted_iota(jnp.int32, (tr, cap), 1).astype(F32)
    onehot_t = jnp.where((pos == slot1) | (pos == slot2), 1.0, 0.0).astype(BF16)
    xc[i % 2] = lax.dot_general(onehot_t, hb, (((0,), (0,)), ((), ())), preferred_element_type=F32).astype(BF16)

    def copies(tile, wait):
        buf = tile % 2
        src = jnp.int32(0)
        for e in range(N_EXPERTS):
            ne = tab_ref[tile * 2 * N_EXPERTS + N_EXPERTS + e]
            dst = e * seg_units + tab_ref[tile * 2 * N_EXPERTS + e]
            if not wait:
                _chunk_dma(ne, lambda off, bit, src=src, dst=dst: pltpu.make_async_copy(
                    xc.at[buf].at[_rows(src + off, bit)], xs_ref.at[_rows(dst + off, bit)], sem.at[buf]))
            src = src + ne
        if wait:
            _wait_units(src, lambda off, bit: pltpu.make_async_copy(
                xc.at[buf].at[_rows(off, bit)], xs_ref.at[_rows(off, bit)], sem.at[buf]), cap // ROW_UNIT)

    for e in range(N_EXPERTS):
        ne = units[0, e].astype(jnp.int32)
        tab_ref[i * 2 * N_EXPERTS + e] = run[e]
        tab_ref[i * 2 * N_EXPERTS + N_EXPERTS + e] = ne
        run[e] = run[e] + ne
    copies(i, False)

    @pl.when(i > 0)
    def _():
        copies(i - 1, True)

    @pl.when(i == nt - 1)
    def _():
        copies(i, True)
        fill = EXPERT_TILE // ROW_UNIT
        xc[0, pl.ds(0, EXPERT_TILE), :] = jnp.zeros((EXPERT_TILE, xc.shape[2]), BF16)
        tails = [pltpu.make_async_copy(xc.at[0].at[_rows(0, fill)],
                                       xs_ref.at[_rows(e * seg_units + run[e], fill)], sem.at[0])
                 for e in range(N_EXPERTS)]
        for cp in tails:
            cp.start()
        for cp in tails:
            cp.wait()
        for e in range(N_EXPERTS):
            tot_ref[e] = run[e]


def _moe_expert_kernel(eo_ref, rb_ref, valid_ref, xs_ref, wg_ref, wu_ref, wd_ref, ys_ref, wg_s, wu_s, wd_s):
    w = pl.program_id(0)

    @pl.when((w == 0) | (eo_ref[w] != eo_ref[jnp.maximum(w - 1, 0)]))
    def _():
        wg_s[...] = wg_ref[0].astype(BF16)
        wu_s[...] = wu_ref[0].astype(BF16)
        wd_s[...] = wd_ref[0].astype(BF16)

    @pl.when(valid_ref[w] == 1)
    def _():
        x = xs_ref[...]
        act = jnp.dot(x, wg_s[...], preferred_element_type=F32)
        act = act * _sigmoid(act) * jnp.dot(x, wu_s[...], preferred_element_type=F32)
        ys_ref[...] = jnp.dot(act.astype(BF16), wd_s[...], preferred_element_type=F32).astype(ys_ref.dtype)


def _moe_combine_kernel(tab_ref, x_ref, info_ref, ys_ref, fg_ref, out_ref, yc, sem, *, tr, cap, seg_units):
    i = pl.program_id(0)

    nt = pl.num_programs(0)

    def copies(tile, wait):
        buf = tile % 2
        dst = jnp.int32(0)
        for e in range(N_EXPERTS):
            ne = tab_ref[tile * 2 * N_EXPERTS + N_EXPERTS + e]
            src = e * seg_units + tab_ref[tile * 2 * N_EXPERTS + e]
            if not wait:
                _chunk_dma(ne, lambda off, bit, src=src, dst=dst: pltpu.make_async_copy(
                    ys_ref.at[_rows(src + off, bit)], yc.at[buf].at[_rows(dst + off, bit)], sem.at[buf]))
            dst = dst + ne
        if wait:
            _wait_units(dst, lambda off, bit: pltpu.make_async_copy(
                ys_ref.at[_rows(off, bit)], yc.at[buf].at[_rows(off, bit)], sem.at[buf]), cap // ROW_UNIT)

    @pl.when(i == 0)
    def _():
        yc[...] = jnp.zeros(yc.shape, yc.dtype)
        copies(i, False)

    @pl.when(i + 1 < nt)
    def _():
        copies(i + 1, False)

    copies(i, True)
    info = info_ref[...]
    pos = lax.broadcasted_iota(jnp.int32, (tr, cap), 1).astype(F32)
    rows = yc[i % 2]
    y1 = jnp.dot(jnp.where(pos == info[:, 0:1], 1.0, 0.0).astype(BF16), rows, preferred_element_type=F32)
    y2 = jnp.dot(jnp.where(pos == info[:, 1:2], 1.0, 0.0).astype(BF16), rows, preferred_element_type=F32)
    out_ref[...] = _rms(x_ref[...] + info[:, 2:3] * y1 + info[:, 3:4] * y2, fg_ref[...])


def moe_sparse_final(res, a, wo, g, w_router_pad, wg, wu, wd, final_g):
    t, d = res.shape
    tr, te = ROUTE_TILE, EXPERT_TILE
    assert t % tr == 0
    ntiles = t // tr
    cap = _sorted_cap(tr)
    ne, _, ff = wg.shape
    seg_rows = -(-(t + (ROW_UNIT - 1) * ntiles + te) // te) * te
    seg_units = seg_rows // ROW_UNIT
    smem = pl.BlockSpec(memory_space=pltpu.SMEM)

    row_spec = pl.BlockSpec((tr, d), lambda i: (i, 0))
    x, xs, info, tab, tot = pl.pallas_call(
        functools.partial(_moe_route_kernel, tr=tr, cap=cap, seg_units=seg_units),
        grid=(ntiles,),
        in_specs=[
            row_spec, row_spec,
            pl.BlockSpec((d, d), lambda i: (0, 0)),
            pl.BlockSpec((1, d), lambda i: (0, 0)),
            pl.BlockSpec((d, LANES), lambda i: (0, 0)),
        ],
        out_specs=[row_spec, pl.BlockSpec(memory_space=pl.ANY), pl.BlockSpec((tr, LANES), lambda i: (i, 0)),
                   smem, smem],
        out_shape=[
            jax.ShapeDtypeStruct((t, d), F32),
            jax.ShapeDtypeStruct((ne * seg_rows, d), BF16),
            jax.ShapeDtypeStruct((t, LANES), F32),
            jax.ShapeDtypeStruct((ntiles * 2 * ne,), jnp.int32),
            jax.ShapeDtypeStruct((ne,), jnp.int32),
        ],
        scratch_shapes=[pltpu.VMEM((2, cap, d), BF16), pltpu.SMEM((ne,), jnp.int32), pltpu.SemaphoreType.DMA((2,))],
        compiler_params=_params(("arbitrary",)),
        name="moe_route",
    )(res, a, wo, g.reshape(1, d), w_router_pad)

    tiles_per_e = (tot * ROW_UNIT + te - 1) // te
    ends = jnp.cumsum(tiles_per_e)
    n_items = ends[-1]
    max_items = (2 * t + ne * (ROW_UNIT - 1) * ntiles) // te + ne
    w = jnp.arange(max_items, dtype=jnp.int32)
    wc = jnp.minimum(w, n_items - 1)
    eo = jnp.sum((wc[:, None] >= ends[None, :]).astype(jnp.int32), axis=1)
    rb = (eo * (seg_rows // te) + wc - (ends - tiles_per_e)[eo]).astype(jnp.int32)
    valid = (w < n_items).astype(jnp.int32)

    ys = pl.pallas_call(
        _moe_expert_kernel,
        grid_spec=pltpu.PrefetchScalarGridSpec(
            num_scalar_prefetch=3,
            grid=(max_items,),
            in_specs=[
                pl.BlockSpec((te, d), lambda w, eo, rb, va: (rb[w], 0)),
                pl.BlockSpec((1, d, ff), lambda w, eo, rb, va: (eo[w], 0, 0)),
                pl.BlockSpec((1, d, ff), lambda w, eo, rb, va: (eo[w], 0, 0)),
                pl.BlockSpec((1, ff, d), lambda w, eo, rb, va: (eo[w], 0, 0)),
            ],
            out_specs=pl.BlockSpec((te, d), lambda w, eo, rb, va: (rb[w], 0)),
            scratch_shapes=[pltpu.VMEM((d, ff), BF16), pltpu.VMEM((d, ff), BF16), pltpu.VMEM((ff, d), BF16)],
        ),
        out_shape=jax.ShapeDtypeStruct((ne * seg_rows, d), BF16),
        compiler_params=_params(("arbitrary",)),
        name="moe_experts",
    )(eo, rb, valid, xs, wg, wu, wd)

    return pl.pallas_call(
        functools.partial(_moe_combine_kernel, tr=tr, cap=cap, seg_units=seg_units),
        grid_spec=pltpu.PrefetchScalarGridSpec(
            num_scalar_prefetch=1,
            grid=(ntiles,),
            in_specs=[
                pl.BlockSpec((tr, d), lambda i, tab: (i, 0)),
                pl.BlockSpec((tr, LANES), lambda i, tab: (i, 0)),
                pl.BlockSpec(memory_space=pl.ANY),
                pl.BlockSpec((1, d), lambda i, tab: (0, 0)),
            ],
            out_specs=pl.BlockSpec((tr, d), lambda i, tab: (i, 0)),
            scratch_shapes=[pltpu.VMEM((2, cap, d), BF16), pltpu.SemaphoreType.DMA((2,))],
        ),
        out_shape=jax.ShapeDtypeStruct((t, d), F32),
        compiler_params=_params(("arbitrary",)),
        name="moe_combine",
    )(tab, x, info, ys, final_g.reshape(1, d))


def _trunk(x, mem_k, mem_v, conv_buf, h0, past_k, past_v, p, lam_init):
    b, t, d = x.shape
    n = b * t
    xf = x.reshape(n, d)
    aw = DIFF_HEADS * DIFF_VD
    lru_w = p["lru_lambda"].shape[0]
    precise = past_k is not None
    tm = n if precise else 512
    tn = 1024
    assert t >= CONV_W - 1
    seq = lambda a: a.reshape(b, t, a.shape[-1])
    lam_params = (p["lam_q1"], p["lam_k1"], p["lam_q2"], p["lam_k2"])
    lru_args = (p["conv_w"], p["conv_b"], p["lru_wa"], p["lru_ba"].reshape(-1), p["lru_wx"],
                p["lru_bx"].reshape(-1), p["lru_lambda"])

    if precise:
        xb, gate, q, k, v = norm_linear_f32(xf, p["norm_mix_g"], p["w_in_f32"], tn=tn, split=True)
    else:
        xb, gate, q, k, v = norm_linear(xf, p["norm_mix_g"], p["w_in"], tm=tm, tn=tn,
                                        out_widths=[lru_w, lru_w, aw, aw, aw], out_dtypes=[F32] * 5)
    lru_out, h_last = lru_mixer(seq(xb), seq(gate), conv_buf, h0, *lru_args, tc=256, precise=precise)
    if precise:
        att = diff_attention_sample(seq(q), seq(k), seq(v), past_k, past_v, lam_params, p["subln_g"], lam_init)
    else:
        att = diff_attention_prompt(seq(q), seq(k), seq(v), lam_params, p["subln_g"], lam_init, tq=512)
    mix_in = [lru_out.reshape(n, lru_w), att.reshape(n, aw)]
    if precise:
        x1 = linear_residual_f32(xf, mix_in, p["w_out_f32"], tn=tn)
        (qx,) = norm_linear_f32(x1, p["norm_cross_g"], p["xq_w_f32"], tn=tn, split=False)
    else:
        x1, qx = linear_residual_norm_linear(xf, mix_in, p["w_out"], p["norm_cross_g"], p["xq_w"], tm=tm, tn=tn)
    o = cross_attention_core(seq(qx), mem_k, mem_v, tq=512, precise="native" if precise else False)
    experts = (p["exp_gate"], p["exp_up"], p["exp_down"])
    if precise:
        x2 = linear_residual_f32(x1, [o.reshape(n, d)], p["xo_w_f32"], tn=tn)
        y = moe_dense_final(x2, p["norm_ffn_g"], p["router_pad_f32"], *experts, p["final_norm_g"], tm=tm,
                            precise_router=True)
    else:
        y = moe_sparse_final(x1, o.reshape(n, d), p["xo_w"], p["norm_ffn_g"], p["router_pad"], *experts,
                             p["final_norm_g"])
    new_conv = seq(xb)[:, t - (CONV_W - 1):, :]
    return y.reshape(b, t, d), new_conv, h_last, k, v


def kernel(x_prompt, x_sample, cache_diff_k, cache_diff_v, cache_mem_k, cache_mem_v, state_conv, state_lru, mem_prompt, norm_mix_g, w_in, conv_w, conv_b, lru_wa, lru_ba, lru_wx, lru_bx, lru_lambda, lam_q1, lam_k1, lam_q2, lam_k2, subln_g, w_out, norm_cross_g, norm_mem_g, xq_w, xk_w, xv_w, xo_w, norm_ffn_g, router_group_w, router_expert_w, exp_gate, exp_up, exp_down, final_norm_g):
    depth = w_in.shape[0]
    assert depth == 1, "single-layer step"
    bp, tp, d = x_prompt.shape
    bs, ts, _ = x_sample.shape
    past = cache_diff_k.shape[2]
    n_mem = mem_prompt.shape[1]
    aw = DIFF_HEADS * DIFF_VD
    l = 0
    lam_init = 0.8 - 0.6 * math.exp(-0.3 * l)

    router = jnp.concatenate([router_group_w[l], router_expert_w[l]], axis=1)
    router_pad_f32 = jnp.pad(router, ((0, 0), (0, LANES - router.shape[1])))
    router_pad = router_pad_f32.astype(BF16)
    p = dict(router_pad_f32=router_pad_f32, w_in_f32=w_in[l], w_out_f32=w_out[l], xq_w_f32=xq_w[l],
             xo_w_f32=xo_w[l], **dict(norm_mix_g=norm_mix_g[l], conv_w=conv_w[l], conv_b=conv_b[l], lru_wa=lru_wa[l],
             lru_ba=lru_ba[l], lru_wx=lru_wx[l], lru_bx=lru_bx[l], lru_lambda=lru_lambda[l], lam_q1=lam_q1[l],
             lam_k1=lam_k1[l], lam_q2=lam_q2[l], lam_k2=lam_k2[l], subln_g=subln_g[l],
             norm_cross_g=norm_cross_g[l], norm_ffn_g=norm_ffn_g[l], router_pad=router_pad,
             final_norm_g=final_norm_g))
    for name, w in (("w_in", w_in), ("w_out", w_out), ("xq_w", xq_w), ("xo_w", xo_w)):
        p[name] = w[l].astype(BF16)
    p.update(exp_gate=exp_gate[l], exp_up=exp_up[l], exp_down=exp_down[l])

    memf = mem_prompt.reshape(bp * n_mem, d)
    w_mem = jnp.concatenate([xk_w[l].astype(BF16), xv_w[l].astype(BF16)], axis=1)
    mk_p, mv_p = norm_linear(memf, norm_mem_g[l], w_mem, tm=512, tn=1024, out_widths=[d, d], out_dtypes=[F32, F32])
    mk_p = mk_p.reshape(bp, n_mem, d)
    mv_p = mv_p.reshape(bp, n_mem, d)

    zero_buf = jnp.zeros((bp, CONV_W - 1, lru_lambda.shape[1]), F32)
    zero_h = jnp.zeros((bp, lru_lambda.shape[1]), F32)
    y_p, cb_p, hl_p, k_p, v_p = _trunk(x_prompt, mk_p, mv_p, zero_buf, zero_h, None, None, p, lam_init)
    y_s, cb_s, hl_s, k_s, v_s = _trunk(x_sample, cache_mem_k[l], cache_mem_v[l], state_conv[l], state_lru[l],
                                       cache_diff_k[l], cache_diff_v[l],
                                       p, lam_init)

    hd2 = 2 * DIFF_HD
    return (y_p, y_s,
            k_p.reshape(1, bp, tp, DIFF_HEADS, hd2), v_p.reshape(1, bp, tp, DIFF_HEADS, DIFF_VD),
            mk_p.reshape(1, bp, n_mem, X_HEADS, d // X_HEADS), mv_p.reshape(1, bp, n_mem, X_HEADS, d // X_HEADS),
            cb_p[None], hl_p[None],
            k_s.reshape(1, bs, ts, DIFF_HEADS, hd2), v_s.reshape(1, bs, ts, DIFF_HEADS, DIFF_VD),
            cb_s[None], hl_s[None].astype(state_lru.dtype))
```

```python
import functools
import math

import jax
import jax.numpy as jnp
from jax import lax
from jax.experimental import pallas as pl
from jax.experimental.pallas import tpu as pltpu

F32 = jnp.float32
BF16 = jnp.bfloat16

CHUNK = 64
CONV_W = 4
LRU_C = 8.0
LRU_BLOCK = 64
DIFF_HEADS = 8
DIFF_HD = 64
DIFF_VD = 2 * DIFF_HD
X_HEADS = 4
N_GROUPS = 4
EXP_PER_GROUP = 4
N_EXPERTS = N_GROUPS * EXP_PER_GROUP
EPS = 1e-6

LANES = 128
SUBLANES = 8
MXU_DIM = 256
VMEM_LIMIT_BYTES = 56 * 1024 * 1024


def _params(semantics):
    return pltpu.CompilerParams(dimension_semantics=semantics, vmem_limit_bytes=VMEM_LIMIT_BYTES)


def _split(a):
    a = a.astype(F32)
    hi = a.astype(BF16)
    return hi, (a - hi.astype(F32)).astype(BF16)


def _dot_dims(a, b, dims, precise):
    if not precise:
        return lax.dot_general(a.astype(BF16), b.astype(BF16), dims, preferred_element_type=F32)
    if precise == "native":
        return lax.dot_general(a.astype(F32), b.astype(F32), dims, precision=lax.Precision.HIGHEST,
                               preferred_element_type=F32)
    ah, al = _split(a)
    bh, bl = _split(b)
    mm = lambda x, y: lax.dot_general(x, y, dims, preferred_element_type=F32)
    return mm(ah, bh) + (mm(ah, bl) + mm(al, bh))


def _dot(a, b, precise=False):
    return _dot_dims(a, b, (((1,), (0,)), ((), ())), precise)


def _dot_nt(a, b, precise=False):
    return _dot_dims(a, b, (((1,), (1,)), ((), ())), precise)


def _rms(x, g):
    return x * lax.rsqrt(jnp.mean(x * x, axis=-1, keepdims=True) + EPS) * g


def _sigmoid(x):
    return 1.0 / (1.0 + jnp.exp(-x))


def _gelu_tanh(x):
    c = math.sqrt(2.0 / math.pi)
    return 0.5 * x * (1.0 + jnp.tanh(c * (x + 0.044715 * (x * x * x))))


def _norm_linear_kernel(x_ref, g_ref, w_ref, *out_refs, tn):
    h = _rms(x_ref[...], g_ref[...]).astype(BF16)
    col = 0
    for o_ref in out_refs:
        for c in range(o_ref.shape[1] // tn):
            o_ref[:, c * tn:(c + 1) * tn] = jnp.dot(
                h, w_ref[:, col:col + tn], preferred_element_type=F32).astype(o_ref.dtype)
            col += tn


def norm_linear(x, g, w, *, tm, tn, out_widths, out_dtypes):
    t, k = x.shape
    n = w.shape[1]
    tm = min(tm, t)
    assert sum(out_widths) == n and all(wd % tn == 0 for wd in out_widths)
    return pl.pallas_call(
        functools.partial(_norm_linear_kernel, tn=tn),
        grid=(t // tm,),
        in_specs=[
            pl.BlockSpec((tm, k), lambda i: (i, 0)),
            pl.BlockSpec((1, k), lambda i: (0, 0)),
            pl.BlockSpec((k, n), lambda i: (0, 0)),
        ],
        out_specs=[pl.BlockSpec((tm, wd), lambda i: (i, 0)) for wd in out_widths],
        out_shape=[jax.ShapeDtypeStruct((t, wd), dt) for wd, dt in zip(out_widths, out_dtypes)],
        compiler_params=_params(("parallel",)),
        name="norm_linear",
    )(x, g.reshape(1, k), w)


def _linear_res_norm_linear_kernel(*refs, n_in, tn):
    res_ref = refs[0]
    a_refs = refs[1:1 + n_in]
    w1_ref, g_ref, w2_ref, x1_ref, y_ref = refs[1 + n_in:]
    kc = a_refs[0].shape[1]
    for c in range(x1_ref.shape[1] // tn):
        cols = slice(c * tn, (c + 1) * tn)
        acc = res_ref[:, cols]
        for r, a_ref in enumerate(a_refs):
            acc = acc + _dot(a_ref[...], w1_ref[r * kc:(r + 1) * kc, cols])
        x1_ref[:, cols] = acc
    h = _rms(x1_ref[...], g_ref[...]).astype(BF16)
    for c in range(y_ref.shape[1] // tn):
        cols = slice(c * tn, (c + 1) * tn)
        y_ref[:, cols] = jnp.dot(h, w2_ref[:, cols], preferred_element_type=F32).astype(y_ref.dtype)


def linear_residual_norm_linear(res, a_list, w1, g, w2, *, tm, tn):
    t, n = res.shape
    tm = min(tm, t)
    n_in = len(a_list)
    kc = a_list[0].shape[1]
    n2 = w2.shape[1]
    in_specs = [pl.BlockSpec((tm, n), lambda i: (i, 0))]
    in_specs += [pl.BlockSpec((tm, kc), lambda i: (i, 0)) for _ in range(n_in)]
    in_specs += [pl.BlockSpec(w1.shape, lambda i: (0, 0)), pl.BlockSpec((1, n), lambda i: (0, 0)),
                 pl.BlockSpec(w2.shape, lambda i: (0, 0))]
    return pl.pallas_call(
        functools.partial(_linear_res_norm_linear_kernel, n_in=n_in, tn=tn),
        grid=(t // tm,),
        in_specs=in_specs,
        out_specs=[pl.BlockSpec((tm, n), lambda i: (i, 0)), pl.BlockSpec((tm, n2), lambda i: (i, 0))],
        out_shape=[jax.ShapeDtypeStruct((t, n), F32), jax.ShapeDtypeStruct((t, n2), BF16)],
        compiler_params=_params(("parallel",)),
        name="linear_residual_norm_linear",
    )(res, *a_list, w1, g.reshape(1, n), w2)


def _norm_linear_f32_kernel(x_ref, g_ref, w_ref, *refs):
    out_refs, h_ref = refs[:-1], refs[-1]
    j = pl.program_id(1)

    @pl.when(j == 0)
    def _():
        h_ref[...] = _rms(x_ref[...], g_ref[...])

    if len(out_refs) == 1:
        out_refs[0][...] = _dot(h_ref[...], w_ref[...], True)
    else:
        for c, o_ref in enumerate(out_refs):
            @pl.when(j == c)
            def _(o_ref=o_ref):
                o_ref[...] = _dot(h_ref[...], w_ref[...], True)


def norm_linear_f32(x, g, w, *, tn, split):
    t, k = x.shape
    n = w.shape[1]
    nj = n // tn
    if split:
        out_shape = [jax.ShapeDtypeStruct((t, tn), F32) for _ in range(nj)]
        out_specs = [pl.BlockSpec((t, tn), lambda i, j: (i, 0)) for _ in range(nj)]
    else:
        out_shape = [jax.ShapeDtypeStruct((t, n), F32)]
        out_specs = [pl.BlockSpec((t, tn), lambda i, j: (i, j))]
    return pl.pallas_call(
        _norm_linear_f32_kernel,
        grid=(1, nj),
        in_specs=[
            pl.BlockSpec((t, k), lambda i, j: (i, 0)),
            pl.BlockSpec((1, k), lambda i, j: (0, 0)),
            pl.BlockSpec((k, tn), lambda i, j: (0, j)),
        ],
        out_specs=out_specs,
        out_shape=out_shape,
        scratch_shapes=[pltpu.VMEM((t, k), F32)],
        compiler_params=_params(("parallel", "arbitrary")),
        name="norm_linear_f32",
    )(x, g.reshape(1, k), w)


def _linear_res_f32_kernel(*refs, n_in):
    res_ref = refs[0]
    a_refs = refs[1:1 + n_in]
    w_refs = refs[1 + n_in:1 + 2 * n_in]
    out_ref = refs[1 + 2 * n_in]
    acc = res_ref[...]
    for a_ref, w_ref in zip(a_refs, w_refs):
        acc = acc + _dot(a_ref[...], w_ref[...], True)
    out_ref[...] = acc


def linear_residual_f32(res, a_list, w, *, tn):
    t, n = res.shape
    n_in = len(a_list)
    kc = a_list[0].shape[1]
    in_specs = [pl.BlockSpec((t, tn), lambda i, j: (i, j))]
    in_specs += [pl.BlockSpec((t, kc), lambda i, j: (i, 0)) for _ in range(n_in)]
    in_specs += [pl.BlockSpec((kc, tn), lambda i, j, c=c: (c, j)) for c in range(n_in)]
    return pl.pallas_call(
        functools.partial(_linear_res_f32_kernel, n_in=n_in),
        grid=(1, n // tn),
        in_specs=in_specs,
        out_specs=pl.BlockSpec((t, tn), lambda i, j: (i, j)),
        out_shape=jax.ShapeDtypeStruct((t, n), F32),
        compiler_params=_params(("parallel", "arbitrary")),
        name="linear_residual_f32",
    )(res, *a_list, *([w] * n_in))


def _lru_kernel(xb_ref, gate_ref, cbuf_ref, h0_ref, cw_ref, cb_ref, wa_ref, ba_ref, wx_ref, bx_ref, lam_ref,
                out_ref, hlast_ref, xpad, hcar, a_s, u_s, *, tc, width, precise):
    c = pl.program_id(1)
    nslab = width // MXU_DIM
    ngrp = tc // SUBLANES

    @pl.when(c == 0)
    def _():
        xpad[pl.ds(0, SUBLANES), :] = cbuf_ref[0]
        hcar[...] = h0_ref[0]

    xpad[pl.ds(SUBLANES, tc), :] = xb_ref[0].astype(F32)
    xc = cb_ref[...] + cw_ref[pl.ds(CONV_W - 1, 1), :] * xpad[pl.ds(SUBLANES, tc), :]
    for j in range(CONV_W - 1):
        xc = xc + cw_ref[pl.ds(j, 1), :] * xpad[pl.ds(SUBLANES - (CONV_W - 1) + j, tc), :]
    xpad[pl.ds(0, SUBLANES), :] = xpad[pl.ds(tc, SUBLANES), :]

    lam = lam_ref[...]
    softplus_neg = jnp.maximum(-lam, 0.0) + jnp.log1p(jnp.exp(-jnp.abs(lam)))
    c8 = -LRU_C * softplus_neg

    sub = lax.broadcasted_iota(jnp.int32, (ngrp, SUBLANES, MXU_DIM), 1)
    for s in range(nslab):
        cols = slice(s * MXU_DIM, (s + 1) * MXU_DIM)
        xs = xc[:, cols]
        r = _sigmoid(_dot(xs, wa_ref[s], precise) + ba_ref[:, cols])
        i = _sigmoid(_dot(xs, wx_ref[s], precise) + bx_ref[:, cols])
        a = jnp.exp(c8[:, cols] * r)
        u = jnp.sqrt(1.0 - a * a) * (i * xs)
        a3 = a.reshape(ngrp, SUBLANES, MXU_DIM)
        u3 = u.reshape(ngrp, SUBLANES, MXU_DIM)
        d = 1
        while d < SUBLANES:
            a_sh = pltpu.roll(a3, d, 1)
            u_sh = pltpu.roll(u3, d, 1)
            keep = sub >= d
            u3 = jnp.where(keep, u3 + a3 * u_sh, u3)
            a3 = jnp.where(keep, a3 * a_sh, a3)
            d *= 2
        a_s[:, cols] = a3.reshape(tc, MXU_DIM)
        u_s[:, cols] = u3.reshape(tc, MXU_DIM)

    def body(g, hin):
        rows = pl.ds(pl.multiple_of(g * SUBLANES, SUBLANES), SUBLANES)
        h = u_s[rows, :] + a_s[rows, :] * hin
        u_s[rows, :] = h
        return h[SUBLANES - 1:SUBLANES, :]

    hfin = lax.fori_loop(0, ngrp, body, hcar[...])
    hcar[...] = hfin
    out_ref[0] = (u_s[...] * _gelu_tanh(gate_ref[0].astype(F32))).astype(out_ref.dtype)
    hlast_ref[0] = hfin


def _block_diag(w, per):
    nb, k, _ = w.shape
    w4 = w.reshape(nb // per, per, k, k)
    eye = jnp.eye(per, dtype=w.dtype)
    return jnp.einsum("cipq,ij->cipjq", w4, eye).reshape(nb // per, per * k, per * k)


def lru_mixer(xb, gate, conv_buf, h0, conv_w, conv_b, wa, ba, wx, bx, lam, *, tc, precise):
    b, t, width = xb.shape
    tc = min(tc, t)
    per = MXU_DIM // LRU_BLOCK
    act_dt = F32 if precise else BF16
    wa_bd = _block_diag(wa, per).astype(act_dt)
    wx_bd = _block_diag(wx, per).astype(act_dt)
    nslab = wa_bd.shape[0]
    cbuf8 = jnp.concatenate([jnp.zeros((b, SUBLANES - (CONV_W - 1), width), F32), conv_buf.astype(F32)], axis=1)
    row = lambda v: v.reshape(1, width).astype(F32)
    vec_spec = pl.BlockSpec((1, width), lambda bi, ci: (0, 0))
    seq_spec = pl.BlockSpec((1, tc, width), lambda bi, ci: (bi, ci, 0))
    wspec = pl.BlockSpec((nslab, MXU_DIM, MXU_DIM), lambda bi, ci: (0, 0, 0))
    out, hlast = pl.pallas_call(
        functools.partial(_lru_kernel, tc=tc, width=width, precise=precise),
        grid=(b, t // tc),
        in_specs=[
            seq_spec, seq_spec,
            pl.BlockSpec((1, SUBLANES, width), lambda bi, ci: (bi, 0, 0)),
            pl.BlockSpec((1, 1, width), lambda bi, ci: (bi, 0, 0)),
            pl.BlockSpec((CONV_W, width), lambda bi, ci: (0, 0)),
            vec_spec, wspec, vec_spec, wspec, vec_spec, vec_spec,
        ],
        out_specs=[seq_spec, pl.BlockSpec((1, 1, width), lambda bi, ci: (bi, 0, 0))],
        out_shape=[jax.ShapeDtypeStruct((b, t, width), act_dt), jax.ShapeDtypeStruct((b, 1, width), F32)],
        scratch_shapes=[
            pltpu.VMEM((tc + SUBLANES, width), F32),
            pltpu.VMEM((1, width), F32),
            pltpu.VMEM((tc, width), F32),
            pltpu.VMEM((tc, width), F32),
        ],
        compiler_params=_params(("parallel", "arbitrary")),
        name="lru_mixer",
    )(xb, gate, cbuf8, h0.reshape(b, 1, width).astype(F32), conv_w.astype(F32), row(conv_b), wa_bd,
      row(ba), wx_bd, row(bx), row(lam))
    return out, hlast.reshape(b, width)


def _diff_lambda(lq1, lk1, lq2, lk2, lam_init):
    s1 = jnp.sum(lq1[...] * lk1[...], axis=-1, keepdims=True)
    s2 = jnp.sum(lq2[...] * lk2[...], axis=-1, keepdims=True)
    return jnp.exp(s1) - jnp.exp(s2) + lam_init


_LOG2E_PARTS = (1.4453125, -0.00262451171875, 7.063150405883789e-06)
LOG2E = sum(_LOG2E_PARTS)
N_BIAS_COLS = 2 * len(_LOG2E_PARTS)
ATTN_GROUP = 4


def _attn_prompt_kernel(slopes_ref, q_ref, k_ref, v_ref, lq1, lk1, lq2, lk2, subln_ref, out_ref,
                        kaug, vt, qt, m_s, l_s, acc_s, s_s, diag_s, *, tq, lam_init):
    h = pl.program_id(1)
    qi = pl.program_id(2)
    slope = slopes_ref[h]
    nblk, tk, _ = kaug.shape
    nparts = len(_LOG2E_PARTS)

    @pl.when(qi == 0)
    def _():
        pos = lax.broadcasted_iota(jnp.int32, (tk, LANES), 0)
        lane = lax.broadcasted_iota(jnp.int32, (tk, LANES), 1)
        within = (pos % CHUNK).astype(F32) * slope
        for j in range(nblk):
            rows = slice(j * tk, (j + 1) * tk)
            kaug[j, :, 0:LANES] = k_ref[0, rows, :].astype(BF16)
            coarse = ((pos + j * tk) // CHUNK * CHUNK).astype(F32) * slope
            cols = jnp.where(lane < nparts, coarse, jnp.where(lane < N_BIAS_COLS, within, 0.0))
            kaug[j, :, LANES:2 * LANES] = cols.astype(BF16)
            vt[j] = v_ref[0, rows, :].T.astype(BF16)
        r = lax.broadcasted_iota(jnp.int32, (LANES, tq), 0)
        part = jnp.where(r % nparts == 0, _LOG2E_PARTS[0],
                         jnp.where(r % nparts == 1, _LOG2E_PARTS[1], _LOG2E_PARTS[2]))
        const_rows = jnp.where(r < N_BIAS_COLS, part, 0.0).astype(BF16)
        qt[0, LANES:2 * LANES, :] = const_rows
        qt[1, LANES:2 * LANES, :] = const_rows
        kpos = lax.broadcasted_iota(jnp.int32, (tk, tq), 0)
        qpos = lax.broadcasted_iota(jnp.int32, (tk, tq), 1)
        fix = jnp.where(kpos > qpos, (-2.0 * LOG2E) * slope * (kpos - qpos).astype(F32), 0.0)
        diag_s[...] = jnp.where((kpos // CHUNK) <= (qpos // CHUNK), fix, -jnp.inf)

    qs = q_ref[0].astype(F32) * (LOG2E * DIFF_HD ** -0.5)
    qlane = lax.broadcasted_iota(jnp.int32, qs.shape, 1)
    qt[0, 0:LANES, :] = jnp.where(qlane < DIFF_HD, qs, 0.0).T.astype(BF16)
    qt[1, 0:LANES, :] = jnp.where(qlane >= DIFF_HD, qs, 0.0).T.astype(BF16)
    m_s[...] = jnp.full(m_s.shape, -jnp.inf, F32)
    l_s[...] = jnp.zeros(l_s.shape, F32)
    acc_s[...] = jnp.zeros(acc_s.shape, F32)

    def blocks(kis, diagonal):
        slab = MXU_DIM
        colmax = [[None, None] for _ in kis]
        for j, ki in enumerate(kis):
            last = diagonal and j == len(kis) - 1
            for m in range(2):
                for r in range(tk // slab):
                    rows = slice(r * slab, (r + 1) * slab)
                    s = jnp.dot(kaug[ki, rows, :], qt[m], preferred_element_type=F32)
                    if last:
                        s = s + diag_s[rows, :]
                    s_s[j, m, rows, :] = s
                    pm = jnp.max(s, axis=0, keepdims=True)
                    colmax[j][m] = pm if r == 0 else jnp.maximum(colmax[j][m], pm)
        for j, ki in enumerate(kis):
            vblk = vt[ki]
            for m in range(2):
                m_old = m_s[m]
                m_new = jnp.maximum(m_old, colmax[j][m])
                p = jnp.exp2(s_s[j, m] - m_new)
                alpha = jnp.exp2(m_old - m_new)
                l_s[m] = alpha * l_s[m] + jnp.sum(p, axis=0, keepdims=True)
                acc_s[m] = alpha * acc_s[m] + jnp.dot(vblk, p.astype(BF16), preferred_element_type=F32)
                m_s[m] = m_new

    group = s_s.shape[0]

    def group_body(j, carry):
        blocks([group * j + t for t in range(group)], False)
        return carry

    lax.fori_loop(0, qi // group, group_body, 0)
    for r in range(group):
        @pl.when(qi % group == r)
        def _(r=r):
            blocks([qi - r + t for t in range(r + 1)], True)

    lam = _diff_lambda(lq1, lk1, lq2, lk2, lam_init)
    o = acc_s[0] / l_s[0] - lam * (acc_s[1] / l_s[1])
    o = o * lax.rsqrt(jnp.mean(o * o, axis=0, keepdims=True) + EPS) * subln_ref[...] * (1.0 - lam_init)
    out_ref[0] = o.T.astype(out_ref.dtype)


def _alibi_slopes():
    return 2.0 ** (-8.0 * jnp.arange(1, DIFF_HEADS + 1, dtype=F32) / DIFF_HEADS)


def diff_attention_prompt(q, k, v, lam_params, subln_g, lam_init, *, tq):
    b, t, aw = q.shape
    tq = min(tq, t)
    hd2 = 2 * DIFF_HD
    assert hd2 == LANES and DIFF_VD == LANES and tq % CHUNK == 0
    assert t // CHUNK <= 256, "chunk index must stay exact in bf16"
    lrow = lambda p: p.reshape(1, DIFF_HD).astype(F32)
    lspec = pl.BlockSpec((1, DIFF_HD), lambda bi, hi, qi: (0, 0))
    kv_spec = pl.BlockSpec((1, t, hd2), lambda bi, hi, qi: (bi, 0, hi))
    q_spec = pl.BlockSpec((1, tq, hd2), lambda bi, hi, qi: (bi, qi, hi))
    return pl.pallas_call(
        functools.partial(_attn_prompt_kernel, tq=tq, lam_init=lam_init),
        grid=(b, DIFF_HEADS, t // tq),
        in_specs=[
            pl.BlockSpec(memory_space=pltpu.SMEM),
            q_spec, kv_spec, kv_spec, lspec, lspec, lspec, lspec,
            pl.BlockSpec((DIFF_VD, 1), lambda bi, hi, qi: (0, 0)),
        ],
        out_specs=q_spec,
        out_shape=jax.ShapeDtypeStruct((b, t, aw), BF16),
        scratch_shapes=[
            pltpu.VMEM((t // tq, tq, 2 * LANES), BF16),
            pltpu.VMEM((t // tq, DIFF_VD, tq), BF16),
            pltpu.VMEM((2, 2 * LANES, tq), BF16),
            pltpu.VMEM((2, 1, tq), F32),
            pltpu.VMEM((2, 1, tq), F32),
            pltpu.VMEM((2, DIFF_VD, tq), F32),
            pltpu.VMEM((ATTN_GROUP, 2, tq, tq), F32),
            pltpu.VMEM((tq, tq), F32),
        ],
        compiler_params=_params(("parallel", "parallel", "arbitrary")),
        name="diff_attention_prompt",
    )(_alibi_slopes(), q, k, v, *[lrow(p) for p in lam_params], subln_g.reshape(DIFF_VD, 1).astype(F32))


def _attn_sample_kernel(q_ref, kp_ref, vp_ref, kn_ref, vn_ref, lq1, lk1, lq2, lk2, subln_ref, out_ref,
                        *, past, tq, lam_init):
    hd2 = 2 * DIFF_HD

    def bias_mask(nk, k_off):
        qpos = past + lax.broadcasted_iota(jnp.int32, (tq, nk), 0)
        kpos = k_off + lax.broadcasted_iota(jnp.int32, (tq, nk), 1)
        dist = jnp.abs(qpos - kpos).astype(F32)
        allowed = (kpos // CHUNK) <= (qpos // CHUNK)
        return dist, allowed

    dist_p, ok_p = bias_mask(past, 0)
    dist_n, ok_n = bias_mask(tq, past)
    lam = _diff_lambda(lq1, lk1, lq2, lk2, lam_init)
    qlane = lax.broadcasted_iota(jnp.int32, (tq, hd2), 1)
    for h in range(DIFF_HEADS):
        slope = 2.0 ** (-8.0 * (h + 1) / DIFF_HEADS)
        cols = slice(h * hd2, (h + 1) * hd2)
        q = q_ref[0, :, cols]
        kp, vp = kp_ref[0, :, h, :], vp_ref[0, :, h, :]
        kn, vn = kn_ref[0, :, cols], vn_ref[0, :, cols]
        probs = []
        q2 = jnp.concatenate([jnp.where(qlane < DIFF_HD, q, 0.0), jnp.where(qlane >= DIFF_HD, q, 0.0)], axis=0)
        sp2 = _dot_nt(q2, kp, "native") * (DIFF_HD ** -0.5)
        sn2 = _dot_nt(q2, kn, "native") * (DIFF_HD ** -0.5)
        for m in range(2):
            rows = slice(m * tq, (m + 1) * tq)
            sp = jnp.where(ok_p, sp2[rows] - slope * dist_p, -jnp.inf)
            sn = jnp.where(ok_n, sn2[rows] - slope * dist_n, -jnp.inf)
            mx = jnp.maximum(jnp.max(sp, axis=-1, keepdims=True), jnp.max(sn, axis=-1, keepdims=True))
            pp = jnp.exp(sp - mx)
            pn = jnp.exp(sn - mx)
            l = jnp.sum(pp, axis=-1, keepdims=True) + jnp.sum(pn, axis=-1, keepdims=True)
            probs.append((pp / l, pn / l))
        o = (_dot(probs[0][0] - lam * probs[1][0], vp, "native")
             + _dot(probs[0][1] - lam * probs[1][1], vn, "native"))
        out_ref[0, :, cols] = (o * lax.rsqrt(jnp.mean(o * o, axis=-1, keepdims=True) + EPS) * subln_ref[...]
                               * (1.0 - lam_init)).astype(out_ref.dtype)


def diff_attention_sample(q, k_new, v_new, past_k, past_v, lam_params, subln_g, lam_init):
    b, t, aw = q.shape
    past = past_k.shape[1]
    lrow = lambda p: p.reshape(1, DIFF_HD).astype(F32)
    lspec = pl.BlockSpec((1, DIFF_HD), lambda bi: (0, 0))
    new_spec = pl.BlockSpec((1, t, aw), lambda bi: (bi, 0, 0))
    past_spec = pl.BlockSpec((1, past, DIFF_HEADS, DIFF_VD), lambda bi: (bi, 0, 0, 0))
    return pl.pallas_call(
        functools.partial(_attn_sample_kernel, past=past, tq=t, lam_init=lam_init),
        grid=(b,),
        in_specs=[
            new_spec, past_spec, past_spec, new_spec, new_spec, lspec, lspec, lspec, lspec,
            pl.BlockSpec((1, DIFF_VD), lambda bi: (0, 0)),
        ],
        out_specs=new_spec,
        out_shape=jax.ShapeDtypeStruct((b, t, aw), F32),
        compiler_params=_params(("parallel",)),
        name="diff_attention_sample",
    )(q, past_k, past_v, k_new, v_new, *[lrow(p) for p in lam_params], subln_g.reshape(1, DIFF_VD).astype(F32))


def _cross_kernel(q_ref, mk_ref, mv_ref, out_ref, *, precise):
    d = q_ref.shape[-1]
    hd = d // X_HEADS
    per_head = len(mk_ref.shape) == 4
    for hh in range(X_HEADS):
        cols = slice(hh * hd, (hh + 1) * hd)
        mk = mk_ref[0, :, hh, :] if per_head else mk_ref[0, :, cols]
        mv = mv_ref[0, :, hh, :] if per_head else mv_ref[0, :, cols]
        s = _dot_nt(q_ref[0, :, cols], mk, precise) * (hd ** -0.5)
        p = jnp.exp(s - jnp.max(s, axis=-1, keepdims=True))
        p = p / jnp.sum(p, axis=-1, keepdims=True)
        out_ref[0, :, cols] = _dot(p, mv, precise).astype(out_ref.dtype)


def cross_attention_core(q, mk, mv, *, tq, precise):
    b, t, d = q.shape
    nm = mk.shape[1]
    tq = min(tq, t)
    q_spec = pl.BlockSpec((1, tq, d), lambda bi, qi: (bi, qi, 0))
    m_spec = pl.BlockSpec((1,) + mk.shape[1:], lambda bi, qi: (bi,) + (0,) * (mk.ndim - 1))
    return pl.pallas_call(
        functools.partial(_cross_kernel, precise=precise),
        grid=(b, t // tq),
        in_specs=[q_spec, m_spec, m_spec],
        out_specs=q_spec,
        out_shape=jax.ShapeDtypeStruct((b, t, d), F32 if precise else BF16),
        compiler_params=_params(("parallel", "arbitrary")),
        name="cross_attention_core",
    )(q, mk, mv)


def _first_argmax(vals, lane, valid):
    masked = jnp.where(valid, vals, -jnp.inf)
    mx = jnp.max(masked, axis=-1, keepdims=True)
    idx = jnp.min(jnp.where(masked == mx, lane, LANES), axis=-1, keepdims=True)
    return mx, idx


def _route(logits):
    lane = lax.broadcasted_iota(jnp.int32, logits.shape, 1).astype(F32)
    is_group = lane < N_GROUPS
    gmax, gidx = _first_argmax(logits, lane, is_group)
    gsum = jnp.sum(jnp.where(is_group, jnp.exp(logits - gmax), 0.0), axis=-1, keepdims=True)
    g_top = 1.0 / gsum
    lo = N_GROUPS + gidx * EXP_PER_GROUP
    in_group = (lane >= lo) & (lane < lo + EXP_PER_GROUP)
    e1, i1 = _first_argmax(logits, lane, in_group)
    e2, i2 = _first_argmax(logits, lane, in_group & (lane != i1))
    w2 = jnp.exp(e2 - e1)
    gate1 = g_top / (1.0 + w2)
    gate2 = g_top * w2 / (1.0 + w2)
    return i1 - N_GROUPS, i2 - N_GROUPS, gate1, gate2


def _combine_weights(logits):
    lane = lax.broadcasted_iota(jnp.int32, logits.shape, 1).astype(F32)
    x1, x2, gate1, gate2 = _route(logits)
    return jnp.where(lane == x1, gate1, 0.0) + jnp.where(lane == x2, gate2, 0.0)


def _moe_dense_kernel(x_ref, g_ref, wr_ref, wg_ref, wu_ref, wd_ref, fg_ref, out_ref, h_s, comb_s, acc_s,
                      *, precise_router):
    e = pl.program_id(1)

    @pl.when(e == 0)
    def _():
        h = _rms(x_ref[...], g_ref[...])
        h_s[...] = h.astype(BF16)
        comb_s[...] = _combine_weights(_dot(h, wr_ref[...], precise_router))
        acc_s[...] = x_ref[...]

    hb = h_s[...]
    act = _dot(hb, wg_ref[0])
    act = act * _sigmoid(act) * _dot(hb, wu_ref[0])
    y = _dot(act, wd_ref[0])
    lane = lax.broadcasted_iota(jnp.int32, comb_s.shape, 1)
    ce = jnp.sum(jnp.where(lane == e, comb_s[...], 0.0), axis=-1, keepdims=True)
    acc_s[...] += ce * y

    @pl.when(e == pl.num_programs(1) - 1)
    def _():
        out_ref[...] = _rms(acc_s[...], fg_ref[...])


def moe_dense_final(x, g, w_router_pad, wg, wu, wd, final_g, *, tm, precise_router):
    t, d = x.shape
    tm = min(tm, t)
    ne, _, ff = wg.shape
    return pl.pallas_call(
        functools.partial(_moe_dense_kernel, precise_router=precise_router),
        grid=(t // tm, ne),
        in_specs=[
            pl.BlockSpec((tm, d), lambda i, e: (i, 0)),
            pl.BlockSpec((1, d), lambda i, e: (0, 0)),
            pl.BlockSpec((d, LANES), lambda i, e: (0, 0)),
            pl.BlockSpec((1, d, ff), lambda i, e: (e, 0, 0)),
            pl.BlockSpec((1, d, ff), lambda i, e: (e, 0, 0)),
            pl.BlockSpec((1, ff, d), lambda i, e: (e, 0, 0)),
            pl.BlockSpec((1, d), lambda i, e: (0, 0)),
        ],
        out_specs=pl.BlockSpec((tm, d), lambda i, e: (i, 0)),
        out_shape=jax.ShapeDtypeStruct((t, d), F32),
        scratch_shapes=[pltpu.VMEM((tm, d), BF16), pltpu.VMEM((tm, LANES), F32), pltpu.VMEM((tm, d), F32)],
        compiler_params=_params(("parallel", "arbitrary")),
        name="moe_dense_final",
    )(x, g.reshape(1, d), w_router_pad, wg, wu, wd, final_g.reshape(1, d))


ROW_UNIT = 16
ROUTE_TILE = MXU_DIM
EXPERT_TILE = 512
UNIT_BITS = (16, 8, 4, 2, 1)


def _sorted_cap(tr):
    rows = 2 * tr + N_EXPERTS * (ROW_UNIT - 1)
    return -(-rows // MXU_DIM) * MXU_DIM


def _chunk_dma(units, make_copy):
    off = jnp.int32(0)
    for bit in UNIT_BITS:
        take = (units & bit) != 0

        @pl.when(take)
        def _(off=off, bit=bit):
            make_copy(off, bit).start()

        off = off + jnp.where(take, bit, 0)


def _wait_units(total_units, make_copy, max_units):
    bit = 1
    while bit <= max_units:
        @pl.when((total_units & bit) != 0)
        def _(bit=bit):
            make_copy(0, bit).wait()

        bit *= 2


def _rows(unit_start, units):
    return pl.ds(pl.multiple_of(unit_start * ROW_UNIT, ROW_UNIT), units * ROW_UNIT)


def _moe_route_kernel(res_ref, a_ref, wo_ref, g_ref, wr_ref, x_ref, xs_ref, info_ref, tab_ref, tot_ref, xc, run, sem,
                      *, tr, cap, seg_units):
    i = pl.program_id(0)
    nt = pl.num_programs(0)

    @pl.when(i == 0)
    def _():
        for e in range(N_EXPERTS):
            run[e] = 0

    x_ref[...] = res_ref[...] + jnp.dot(a_ref[...], wo_ref[...], preferred_element_type=F32)
    hb = _rms(x_ref[...], g_ref[...]).astype(BF16)
    e1, e2, g1, g2 = _route(jnp.dot(hb, wr_ref[...], preferred_element_type=F32))
    lane = lax.broadcasted_iota(jnp.int32, (tr, LANES), 1).astype(F32)
    a1 = lane == e1
    a2 = lane == e2
    assigned = jnp.where(a1 | a2, 1.0, 0.0)
    earlier = lax.broadcasted_iota(jnp.int32, (tr, tr), 1) < lax.broadcasted_iota(jnp.int32, (tr, tr), 0)
    rank = jnp.dot(jnp.where(earlier, 1.0, 0.0).astype(BF16), assigned.astype(BF16), preferred_element_type=F32)
    count = jnp.sum(assigned, axis=0, keepdims=True)
    units = jnp.floor((count + (ROW_UNIT - 1)) * (1.0 / ROW_UNIT))
    before = lax.broadcasted_iota(jnp.int32, (LANES, LANES), 0) < lax.broadcasted_iota(jnp.int32, (LANES, LANES), 1)
    units8 = jnp.broadcast_to(units, (SUBLANES, LANES)).astype(BF16)
    base = ROW_UNIT * jnp.dot(units8, jnp.where(before, 1.0, 0.0).astype(BF16), preferred_element_type=F32)[0:1]
    slot = base + rank
    slot1 = jnp.sum(jnp.where(a1, slot, 0.0), axis=1, keepdims=True)
    slot2 = jnp.sum(jnp.where(a2, slot, 0.0), axis=1, keepdims=True)
    info_ref[...] = jnp.where(lane == 0, slot1, jnp.where(lane == 1, slot2,
                              jnp.where(lane == 2, g1, jnp.where(lane == 3, g2, 0.0))))
    pos = lax.broadcasted_iota(jnp.int32, (tr, cap), 1).astype(F32)
    onehot_t = jnp.where((pos == slot1) | (pos == slot2), 1.0, 0.0).astype(BF16)
    xc[i % 2] = lax.dot_general(onehot_t, hb, (((0,), (0,)), ((), ())), preferred_element_type=F32).astype(BF16)

    def copies(tile, wait):
        buf = tile % 2
        src = jnp.int32(0)
        for e in range(N_EXPERTS):
            ne = tab_ref[tile * 2 * N_EXPERTS + N_EXPERTS + e]
            dst = e * seg_units + tab_ref[tile * 2 * N_EXPERTS + e]
            if not wait:
                _chunk_dma(ne, lambda off, bit, src=src, dst=dst: pltpu.make_async_copy(
                    xc.at[buf].at[_rows(src + off, bit)], xs_ref.at[_rows(dst + off, bit)], sem.at[buf]))
            src = src + ne
        if wait:
            _wait_units(src, lambda off, bit: pltpu.make_async_copy(
                xc.at[buf].at[_rows(off, bit)], xs_ref.at[_rows(off, bit)], sem.at[buf]), cap // ROW_UNIT)

    for e in range(N_EXPERTS):
        ne = units[0, e].astype(jnp.int32)
        tab_ref[i * 2 * N_EXPERTS + e] = run[e]
        tab_ref[i * 2 * N_EXPERTS + N_EXPERTS + e] = ne
        run[e] = run[e] + ne
    copies(i, False)

    @pl.when(i > 0)
    def _():
        copies(i - 1, True)

    @pl.when(i == nt - 1)
    def _():
        copies(i, True)
        fill = EXPERT_TILE // ROW_UNIT
        xc[0, pl.ds(0, EXPERT_TILE), :] = jnp.zeros((EXPERT_TILE, xc.shape[2]), BF16)
        tails = [pltpu.make_async_copy(xc.at[0].at[_rows(0, fill)],
                                       xs_ref.at[_rows(e * seg_units + run[e], fill)], sem.at[0])
                 for e in range(N_EXPERTS)]
        for cp in tails:
            cp.start()
        for cp in tails:
            cp.wait()
        for e in range(N_EXPERTS):
            tot_ref[e] = run[e]


def _moe_expert_kernel(eo_ref, rb_ref, valid_ref, xs_ref, wg_ref, wu_ref, wd_ref, ys_ref, wg_s, wu_s, wd_s):
    w = pl.program_id(0)

    @pl.when((w == 0) | (eo_ref[w] != eo_ref[jnp.maximum(w - 1, 0)]))
    def _():
        wg_s[...] = wg_ref[0].astype(BF16)
        wu_s[...] = wu_ref[0].astype(BF16)
        wd_s[...] = wd_ref[0].astype(BF16)

    @pl.when(valid_ref[w] == 1)
    def _():
        x = xs_ref[...]
        act = jnp.dot(x, wg_s[...], preferred_element_type=F32)
        act = act * _sigmoid(act) * jnp.dot(x, wu_s[...], preferred_element_type=F32)
        ys_ref[...] = jnp.dot(act.astype(BF16), wd_s[...], preferred_element_type=F32).astype(ys_ref.dtype)


def _moe_combine_kernel(tab_ref, x_ref, info_ref, ys_ref, fg_ref, out_ref, yc, sem, *, tr, cap, seg_units):
    i = pl.program_id(0)

    nt = pl.num_programs(0)

    def copies(tile, wait):
        buf = tile % 2
        dst = jnp.int32(0)
        for e in range(N_EXPERTS):
            ne = tab_ref[tile * 2 * N_EXPERTS + N_EXPERTS + e]
            src = e * seg_units + tab_ref[tile * 2 * N_EXPERTS + e]
            if not wait:
                _chunk_dma(ne, lambda off, bit, src=src, dst=dst: pltpu.make_async_copy(
                    ys_ref.at[_rows(src + off, bit)], yc.at[buf].at[_rows(dst + off, bit)], sem.at[buf]))
            dst = dst + ne
        if wait:
            _wait_units(dst, lambda off, bit: pltpu.make_async_copy(
                ys_ref.at[_rows(off, bit)], yc.at[buf].at[_rows(off, bit)], sem.at[buf]), cap // ROW_UNIT)

    @pl.when(i == 0)
    def _():
        yc[...] = jnp.zeros(yc.shape, yc.dtype)
        copies(i, False)

    @pl.when(i + 1 < nt)
    def _():
        copies(i + 1, False)

    copies(i, True)
    info = info_ref[...]
    pos = lax.broadcasted_iota(jnp.int32, (tr, cap), 1).astype(F32)
    rows = yc[i % 2]
    y1 = jnp.dot(jnp.where(pos == info[:, 0:1], 1.0, 0.0).astype(BF16), rows, preferred_element_type=F32)
    y2 = jnp.dot(jnp.where(pos == info[:, 1:2], 1.0, 0.0).astype(BF16), rows, preferred_element_type=F32)
    out_ref[...] = _rms(x_ref[...] + info[:, 2:3] * y1 + info[:, 3:4] * y2, fg_ref[...])


def moe_sparse_final(res, a, wo, g, w_router_pad, wg, wu, wd, final_g):
    t, d = res.shape
    tr, te = ROUTE_TILE, EXPERT_TILE
    assert t % tr == 0
    ntiles = t // tr
    cap = _sorted_cap(tr)
    ne, _, ff = wg.shape
    seg_rows = -(-(t + (ROW_UNIT - 1) * ntiles + te) // te) * te
    seg_units = seg_rows // ROW_UNIT
    smem = pl.BlockSpec(memory_space=pltpu.SMEM)

    row_spec = pl.BlockSpec((tr, d), lambda i: (i, 0))
    x, xs, info, tab, tot = pl.pallas_call(
        functools.partial(_moe_route_kernel, tr=tr, cap=cap, seg_units=seg_units),
        grid=(ntiles,),
        in_specs=[
            row_spec, row_spec,
            pl.BlockSpec((d, d), lambda i: (0, 0)),
            pl.BlockSpec((1, d), lambda i: (0, 0)),
            pl.BlockSpec((d, LANES), lambda i: (0, 0)),
        ],
        out_specs=[row_spec, pl.BlockSpec(memory_space=pl.ANY), pl.BlockSpec((tr, LANES), lambda i: (i, 0)),
                   smem, smem],
        out_shape=[
            jax.ShapeDtypeStruct((t, d), F32),
            jax.ShapeDtypeStruct((ne * seg_rows, d), BF16),
            jax.ShapeDtypeStruct((t, LANES), F32),
            jax.ShapeDtypeStruct((ntiles * 2 * ne,), jnp.int32),
            jax.ShapeDtypeStruct((ne,), jnp.int32),
        ],
        scratch_shapes=[pltpu.VMEM((2, cap, d), BF16), pltpu.SMEM((ne,), jnp.int32), pltpu.SemaphoreType.DMA((2,))],
        compiler_params=_params(("arbitrary",)),
        name="moe_route",
    )(res, a, wo, g.reshape(1, d), w_router_pad)

    tiles_per_e = (tot * ROW_UNIT + te - 1) // te
    ends = jnp.cumsum(tiles_per_e)
    n_items = ends[-1]
    max_items = (2 * t + ne * (ROW_UNIT - 1) * ntiles) // te + ne
    w = jnp.arange(max_items, dtype=jnp.int32)
    wc = jnp.minimum(w, n_items - 1)
    eo = jnp.sum((wc[:, None] >= ends[None, :]).astype(jnp.int32), axis=1)
    rb = (eo * (seg_rows // te) + wc - (ends - tiles_per_e)[eo]).astype(jnp.int32)
    valid = (w < n_items).astype(jnp.int32)

    ys = pl.pallas_call(
        _moe_expert_kernel,
        grid_spec=pltpu.PrefetchScalarGridSpec(
            num_scalar_prefetch=3,
            grid=(max_items,),
            in_specs=[
                pl.BlockSpec((te, d), lambda w, eo, rb, va: (rb[w], 0)),
                pl.BlockSpec((1, d, ff), lambda w, eo, rb, va: (eo[w], 0, 0)),
                pl.BlockSpec((1, d, ff), lambda w, eo, rb, va: (eo[w], 0, 0)),
                pl.BlockSpec((1, ff, d), lambda w, eo, rb, va: (eo[w], 0, 0)),
            ],
            out_specs=pl.BlockSpec((te, d), lambda w, eo, rb, va: (rb[w], 0)),
            scratch_shapes=[pltpu.VMEM((d, ff), BF16), pltpu.VMEM((d, ff), BF16), pltpu.VMEM((ff, d), BF16)],
        ),
        out_shape=jax.ShapeDtypeStruct((ne * seg_rows, d), BF16),
        compiler_params=_params(("arbitrary",)),
        name="moe_experts",
    )(eo, rb, valid, xs, wg, wu, wd)

    return pl.pallas_call(
        functools.partial(_moe_combine_kernel, tr=tr, cap=cap, seg_units=seg_units),
        grid_spec=pltpu.PrefetchScalarGridSpec(
            num_scalar_prefetch=1,
            grid=(ntiles,),
            in_specs=[
                pl.BlockSpec((tr, d), lambda i, tab: (i, 0)),
                pl.BlockSpec((tr, LANES), lambda i, tab: (i, 0)),
                pl.BlockSpec(memory_space=pl.ANY),
                pl.BlockSpec((1, d), lambda i, tab: (0, 0)),
            ],
            out_specs=pl.BlockSpec((tr, d), lambda i, tab: (i, 0)),
            scratch_shapes=[pltpu.VMEM((2, cap, d), BF16), pltpu.SemaphoreType.DMA((2,))],
        ),
        out_shape=jax.ShapeDtypeStruct((t, d), F32),
        compiler_params=_params(("arbitrary",)),
        name="moe_combine",
    )(tab, x, info, ys, final_g.reshape(1, d))


def _trunk(x, mem_k, mem_v, conv_buf, h0, past_k, past_v, p, lam_init):
    b, t, d = x.shape
    n = b * t
    xf = x.reshape(n, d)
    aw = DIFF_HEADS * DIFF_VD
    lru_w = p["lru_lambda"].shape[0]
    precise = past_k is not None
    tm = n if precise else 512
    tn = 1024
    assert t >= CONV_W - 1
    seq = lambda a: a.reshape(b, t, a.shape[-1])
    lam_params = (p["lam_q1"], p["lam_k1"], p["lam_q2"], p["lam_k2"])
    lru_args = (p["conv_w"], p["conv_b"], p["lru_wa"], p["lru_ba"].reshape(-1), p["lru_wx"],
                p["lru_bx"].reshape(-1), p["lru_lambda"])

    if precise:
        xb, gate, q, k, v = norm_linear_f32(xf, p["norm_mix_g"], p["w_in_f32"], tn=tn, split=True)
    else:
        xb, gate, q, k, v = norm_linear(xf, p["norm_mix_g"], p["w_in"], tm=tm, tn=tn,
                                        out_widths=[lru_w, lru_w, aw, aw, aw],
                                        out_dtypes=[F32, BF16, BF16, F32, F32])
    lru_out, h_last = lru_mixer(seq(xb), seq(gate), conv_buf, h0, *lru_args, tc=256, precise=precise)
    if precise:
        att = diff_attention_sample(seq(q), seq(k), seq(v), past_k, past_v, lam_params, p["subln_g"], lam_init)
    else:
        att = diff_attention_prompt(seq(q), seq(k), seq(v), lam_params, p["subln_g"], lam_init, tq=512)
    mix_in = [lru_out.reshape(n, lru_w), att.reshape(n, aw)]
    if precise:
        x1 = linear_residual_f32(xf, mix_in, p["w_out_f32"], tn=tn)
        (qx,) = norm_linear_f32(x1, p["norm_cross_g"], p["xq_w_f32"], tn=tn, split=False)
    else:
        x1, qx = linear_residual_norm_linear(xf, mix_in, p["w_out"], p["norm_cross_g"], p["xq_w"], tm=tm, tn=tn)
    o = cross_attention_core(seq(qx), mem_k, mem_v, tq=512, precise="native" if precise else False)
    experts = (p["exp_gate"], p["exp_up"], p["exp_down"])
    if precise:
        x2 = linear_residual_f32(x1, [o.reshape(n, d)], p["xo_w_f32"], tn=tn)
        y = moe_dense_final(x2, p["norm_ffn_g"], p["router_pad_f32"], *experts, p["final_norm_g"], tm=tm,
                            precise_router=True)
    else:
        y = moe_sparse_final(x1, o.reshape(n, d), p["xo_w"], p["norm_ffn_g"], p["router_pad"], *experts,
                             p["final_norm_g"])
    new_conv = seq(xb)[:, t - (CONV_W - 1):, :]
    return y.reshape(b, t, d), new_conv, h_last, k, v


def kernel(x_prompt, x_sample, cache_diff_k, cache_diff_v, cache_mem_k, cache_mem_v, state_conv, state_lru, mem_prompt, norm_mix_g, w_in, conv_w, conv_b, lru_wa, lru_ba, lru_wx, lru_bx, lru_lambda, lam_q1, lam_k1, lam_q2, lam_k2, subln_g, w_out, norm_cross_g, norm_mem_g, xq_w, xk_w, xv_w, xo_w, norm_ffn_g, router_group_w, router_expert_w, exp_gate, exp_up, exp_down, final_norm_g):
    depth = w_in.shape[0]
    assert depth == 1, "single-layer step"
    bp, tp, d = x_prompt.shape
    bs, ts, _ = x_sample.shape
    past = cache_diff_k.shape[2]
    n_mem = mem_prompt.shape[1]
    aw = DIFF_HEADS * DIFF_VD
    l = 0
    lam_init = 0.8 - 0.6 * math.exp(-0.3 * l)

    router = jnp.concatenate([router_group_w[l], router_expert_w[l]], axis=1)
    router_pad_f32 = jnp.pad(router, ((0, 0), (0, LANES - router.shape[1])))
    router_pad = router_pad_f32.astype(BF16)
    p = dict(router_pad_f32=router_pad_f32, w_in_f32=w_in[l], w_out_f32=w_out[l], xq_w_f32=xq_w[l],
             xo_w_f32=xo_w[l], **dict(norm_mix_g=norm_mix_g[l], conv_w=conv_w[l], conv_b=conv_b[l], lru_wa=lru_wa[l],
             lru_ba=lru_ba[l], lru_wx=lru_wx[l], lru_bx=lru_bx[l], lru_lambda=lru_lambda[l], lam_q1=lam_q1[l],
             lam_k1=lam_k1[l], lam_q2=lam_q2[l], lam_k2=lam_k2[l], subln_g=subln_g[l],
             norm_cross_g=norm_cross_g[l], norm_ffn_g=norm_ffn_g[l], router_pad=router_pad,
             final_norm_g=final_norm_g))
    for name, w in (("w_in", w_in), ("w_out", w_out), ("xq_w", xq_w), ("xo_w", xo_w)):
        p[name] = w[l].astype(BF16)
    p.update(exp_gate=exp_gate[l], exp_up=exp_up[l], exp_down=exp_down[l])

    memf = mem_prompt.reshape(bp * n_mem, d)
    w_mem = jnp.concatenate([xk_w[l].astype(BF16), xv_w[l].astype(BF16)], axis=1)
    mk_p, mv_p = norm_linear(memf, norm_mem_g[l], w_mem, tm=512, tn=1024, out_widths=[d, d], out_dtypes=[F32, F32])
    mk_p = mk_p.reshape(bp, n_mem, d)
    mv_p = mv_p.reshape(bp, n_mem, d)

    zero_buf = jnp.zeros((bp, CONV_W - 1, lru_lambda.shape[1]), F32)
    zero_h = jnp.zeros((bp, lru_lambda.shape[1]), F32)
    y_p, cb_p, hl_p, k_p, v_p = _trunk(x_prompt, mk_p, mv_p, zero_buf, zero_h, None, None, p, lam_init)
    y_s, cb_s, hl_s, k_s, v_s = _trunk(x_sample, cache_mem_k[l], cache_mem_v[l], state_conv[l], state_lru[l],
                                       cache_diff_k[l], cache_diff_v[l],
                                       p, lam_init)

    hd2 = 2 * DIFF_HD
    return (y_p, y_s,
            k_p.reshape(1, bp, tp, DIFF_HEADS, hd2), v_p.reshape(1, bp, tp, DIFF_HEADS, DIFF_VD),
            mk_p.reshape(1, bp, n_mem, X_HEADS, d // X_HEADS), mv_p.reshape(1, bp, n_mem, X_HEADS, d // X_HEADS),
            cb_p[None], hl_p[None],
            k_s.reshape(1, bs, ts, DIFF_HEADS, hd2), v_s.reshape(1, bs, ts, DIFF_HEADS, DIFF_VD),
            cb_s[None], hl_s[None].astype(state_lru.dtype))
```

```python
import functools
import math

import jax
import jax.numpy as jnp
from jax import lax
from jax.experimental import pallas as pl
from jax.experimental.pallas import tpu as pltpu

F32 = jnp.float32
BF16 = jnp.bfloat16

CHUNK = 64
CONV_W = 4
LRU_C = 8.0
LRU_BLOCK = 64
DIFF_HEADS = 8
DIFF_HD = 64
DIFF_VD = 2 * DIFF_HD
X_HEADS = 4
N_GROUPS = 4
EXP_PER_GROUP = 4
N_EXPERTS = N_GROUPS * EXP_PER_GROUP
EPS = 1e-6

LANES = 128
SUBLANES = 8
MXU_DIM = 256
VMEM_LIMIT_BYTES = 56 * 1024 * 1024


def _params(semantics):
    return pltpu.CompilerParams(dimension_semantics=semantics, vmem_limit_bytes=VMEM_LIMIT_BYTES)


def _split(a):
    a = a.astype(F32)
    hi = a.astype(BF16)
    return hi, (a - hi.astype(F32)).astype(BF16)


def _dot_dims(a, b, dims, precise):
    if not precise:
        return lax.dot_general(a.astype(BF16), b.astype(BF16), dims, preferred_element_type=F32)
    if precise == "native":
        return lax.dot_general(a.astype(F32), b.astype(F32), dims, precision=lax.Precision.HIGHEST,
                               preferred_element_type=F32)
    ah, al = _split(a)
    bh, bl = _split(b)
    mm = lambda x, y: lax.dot_general(x, y, dims, preferred_element_type=F32)
    return mm(ah, bh) + (mm(ah, bl) + mm(al, bh))


def _dot(a, b, precise=False):
    return _dot_dims(a, b, (((1,), (0,)), ((), ())), precise)


def _dot_nt(a, b, precise=False):
    return _dot_dims(a, b, (((1,), (1,)), ((), ())), precise)


def _rms(x, g):
    return x * lax.rsqrt(jnp.mean(x * x, axis=-1, keepdims=True) + EPS) * g


def _sigmoid(x):
    return 1.0 / (1.0 + jnp.exp(-x))


def _gelu_tanh(x):
    c = math.sqrt(2.0 / math.pi)
    return 0.5 * x * (1.0 + jnp.tanh(c * (x + 0.044715 * (x * x * x))))


def _norm_linear_kernel(x_ref, g_ref, w_ref, *out_refs, tn):
    h = _rms(x_ref[...], g_ref[...]).astype(BF16)
    col = 0
    for o_ref in out_refs:
        for c in range(o_ref.shape[1] // tn):
            o_ref[:, c * tn:(c + 1) * tn] = jnp.dot(
                h, w_ref[:, col:col + tn], preferred_element_type=F32).astype(o_ref.dtype)
            col += tn


def norm_linear(x, g, w, *, tm, tn, out_widths, out_dtypes):
    t, k = x.shape
    n = w.shape[1]
    tm = min(tm, t)
    assert sum(out_widths) == n and all(wd % tn == 0 for wd in out_widths)
    return pl.pallas_call(
        functools.partial(_norm_linear_kernel, tn=tn),
        grid=(t // tm,),
        in_specs=[
            pl.BlockSpec((tm, k), lambda i: (i, 0)),
            pl.BlockSpec((1, k), lambda i: (0, 0)),
            pl.BlockSpec((k, n), lambda i: (0, 0)),
        ],
        out_specs=[pl.BlockSpec((tm, wd), lambda i: (i, 0)) for wd in out_widths],
        out_shape=[jax.ShapeDtypeStruct((t, wd), dt) for wd, dt in zip(out_widths, out_dtypes)],
        compiler_params=_params(("parallel",)),
        name="norm_linear",
    )(x, g.reshape(1, k), w)


def _linear_res_norm_linear_kernel(*refs, n_in, tn):
    res_ref = refs[0]
    a_refs = refs[1:1 + n_in]
    w1_ref, g_ref, w2_ref, x1_ref, y_ref = refs[1 + n_in:]
    kc = a_refs[0].shape[1]
    for c in range(x1_ref.shape[1] // tn):
        cols = slice(c * tn, (c + 1) * tn)
        acc = res_ref[:, cols]
        for r, a_ref in enumerate(a_refs):
            acc = acc + _dot(a_ref[...], w1_ref[r * kc:(r + 1) * kc, cols])
        x1_ref[:, cols] = acc
    h = _rms(x1_ref[...], g_ref[...]).astype(BF16)
    for c in range(y_ref.shape[1] // tn):
        cols = slice(c * tn, (c + 1) * tn)
        y_ref[:, cols] = jnp.dot(h, w2_ref[:, cols], preferred_element_type=F32).astype(y_ref.dtype)


def linear_residual_norm_linear(res, a_list, w1, g, w2, *, tm, tn):
    t, n = res.shape
    tm = min(tm, t)
    n_in = len(a_list)
    kc = a_list[0].shape[1]
    n2 = w2.shape[1]
    in_specs = [pl.BlockSpec((tm, n), lambda i: (i, 0))]
    in_specs += [pl.BlockSpec((tm, kc), lambda i: (i, 0)) for _ in range(n_in)]
    in_specs += [pl.BlockSpec(w1.shape, lambda i: (0, 0)), pl.BlockSpec((1, n), lambda i: (0, 0)),
                 pl.BlockSpec(w2.shape, lambda i: (0, 0))]
    return pl.pallas_call(
        functools.partial(_linear_res_norm_linear_kernel, n_in=n_in, tn=tn),
        grid=(t // tm,),
        in_specs=in_specs,
        out_specs=[pl.BlockSpec((tm, n), lambda i: (i, 0)), pl.BlockSpec((tm, n2), lambda i: (i, 0))],
        out_shape=[jax.ShapeDtypeStruct((t, n), F32), jax.ShapeDtypeStruct((t, n2), BF16)],
        compiler_params=_params(("parallel",)),
        name="linear_residual_norm_linear",
    )(res, *a_list, w1, g.reshape(1, n), w2)


def _norm_linear_f32_kernel(x_ref, g_ref, w_ref, *refs):
    out_refs, h_ref = refs[:-1], refs[-1]
    j = pl.program_id(1)

    @pl.when(j == 0)
    def _():
        h_ref[...] = _rms(x_ref[...], g_ref[...])

    if len(out_refs) == 1:
        out_refs[0][...] = _dot(h_ref[...], w_ref[...], True)
    else:
        for c, o_ref in enumerate(out_refs):
            @pl.when(j == c)
            def _(o_ref=o_ref):
                o_ref[...] = _dot(h_ref[...], w_ref[...], True)


def norm_linear_f32(x, g, w, *, tn, split):
    t, k = x.shape
    n = w.shape[1]
    nj = n // tn
    if split:
        out_shape = [jax.ShapeDtypeStruct((t, tn), F32) for _ in range(nj)]
        out_specs = [pl.BlockSpec((t, tn), lambda i, j: (i, 0)) for _ in range(nj)]
    else:
        out_shape = [jax.ShapeDtypeStruct((t, n), F32)]
        out_specs = [pl.BlockSpec((t, tn), lambda i, j: (i, j))]
    return pl.pallas_call(
        _norm_linear_f32_kernel,
        grid=(1, nj),
        in_specs=[
            pl.BlockSpec((t, k), lambda i, j: (i, 0)),
            pl.BlockSpec((1, k), lambda i, j: (0, 0)),
            pl.BlockSpec((k, tn), lambda i, j: (0, j)),
        ],
        out_specs=out_specs,
        out_shape=out_shape,
        scratch_shapes=[pltpu.VMEM((t, k), F32)],
        compiler_params=_params(("parallel", "arbitrary")),
        name="norm_linear_f32",
    )(x, g.reshape(1, k), w)


def _linear_res_f32_kernel(*refs, n_in):
    res_ref = refs[0]
    a_refs = refs[1:1 + n_in]
    w_refs = refs[1 + n_in:1 + 2 * n_in]
    out_ref = refs[1 + 2 * n_in]
    acc = res_ref[...]
    for a_ref, w_ref in zip(a_refs, w_refs):
        acc = acc + _dot(a_ref[...], w_ref[...], True)
    out_ref[...] = acc


def linear_residual_f32(res, a_list, w, *, tn):
    t, n = res.shape
    n_in = len(a_list)
    kc = a_list[0].shape[1]
    in_specs = [pl.BlockSpec((t, tn), lambda i, j: (i, j))]
    in_specs += [pl.BlockSpec((t, kc), lambda i, j: (i, 0)) for _ in range(n_in)]
    in_specs += [pl.BlockSpec((kc, tn), lambda i, j, c=c: (c, j)) for c in range(n_in)]
    return pl.pallas_call(
        functools.partial(_linear_res_f32_kernel, n_in=n_in),
        grid=(1, n // tn),
        in_specs=in_specs,
        out_specs=pl.BlockSpec((t, tn), lambda i, j: (i, j)),
        out_shape=jax.ShapeDtypeStruct((t, n), F32),
        compiler_params=_params(("parallel", "arbitrary")),
        name="linear_residual_f32",
    )(res, *a_list, *([w] * n_in))


def _lru_kernel(xb_ref, gate_ref, cbuf_ref, h0_ref, cw_ref, cb_ref, wa_ref, ba_ref, wx_ref, bx_ref, lam_ref,
                out_ref, hlast_ref, xpad, hcar, a_s, u_s, *, tc, width, precise):
    c = pl.program_id(1)
    nslab = width // MXU_DIM
    ngrp = tc // SUBLANES

    @pl.when(c == 0)
    def _():
        xpad[pl.ds(0, SUBLANES), :] = cbuf_ref[0]
        hcar[...] = h0_ref[0]

    xpad[pl.ds(SUBLANES, tc), :] = xb_ref[0]
    xc = cb_ref[...] + cw_ref[pl.ds(CONV_W - 1, 1), :] * xpad[pl.ds(SUBLANES, tc), :]
    for j in range(CONV_W - 1):
        xc = xc + cw_ref[pl.ds(j, 1), :] * xpad[pl.ds(SUBLANES - (CONV_W - 1) + j, tc), :]
    xpad[pl.ds(0, SUBLANES), :] = xpad[pl.ds(tc, SUBLANES), :]

    lam = lam_ref[...]
    softplus_neg = jnp.maximum(-lam, 0.0) + jnp.log1p(jnp.exp(-jnp.abs(lam)))
    c8 = -LRU_C * softplus_neg

    sub = lax.broadcasted_iota(jnp.int32, (ngrp, SUBLANES, MXU_DIM), 1)
    for s in range(nslab):
        cols = slice(s * MXU_DIM, (s + 1) * MXU_DIM)
        xs = xc[:, cols]
        r = _sigmoid(_dot(xs, wa_ref[s], precise) + ba_ref[:, cols])
        i = _sigmoid(_dot(xs, wx_ref[s], precise) + bx_ref[:, cols])
        a = jnp.exp(c8[:, cols] * r)
        u = jnp.sqrt(1.0 - a * a) * (i * xs)
        a3 = a.reshape(ngrp, SUBLANES, MXU_DIM)
        u3 = u.reshape(ngrp, SUBLANES, MXU_DIM)
        d = 1
        while d < SUBLANES:
            a_sh = pltpu.roll(a3, d, 1)
            u_sh = pltpu.roll(u3, d, 1)
            keep = sub >= d
            u3 = jnp.where(keep, u3 + a3 * u_sh, u3)
            a3 = jnp.where(keep, a3 * a_sh, a3)
            d *= 2
        a_s[:, cols] = a3.reshape(tc, MXU_DIM)
        u_s[:, cols] = u3.reshape(tc, MXU_DIM)

    def body(g, hin):
        rows = pl.ds(pl.multiple_of(g * SUBLANES, SUBLANES), SUBLANES)
        h = u_s[rows, :] + a_s[rows, :] * hin
        u_s[rows, :] = h
        return h[SUBLANES - 1:SUBLANES, :]

    hfin = lax.fori_loop(0, ngrp, body, hcar[...])
    hcar[...] = hfin
    out_ref[0] = (u_s[...] * _gelu_tanh(gate_ref[0])).astype(out_ref.dtype)
    hlast_ref[0] = hfin


def _block_diag(w, per):
    nb, k, _ = w.shape
    w4 = w.reshape(nb // per, per, k, k)
    eye = jnp.eye(per, dtype=w.dtype)
    return jnp.einsum("cipq,ij->cipjq", w4, eye).reshape(nb // per, per * k, per * k)


def lru_mixer(xb, gate, conv_buf, h0, conv_w, conv_b, wa, ba, wx, bx, lam, *, tc, precise):
    b, t, width = xb.shape
    tc = min(tc, t)
    per = MXU_DIM // LRU_BLOCK
    act_dt = F32 if precise else BF16
    wa_bd = _block_diag(wa, per).astype(act_dt)
    wx_bd = _block_diag(wx, per).astype(act_dt)
    nslab = wa_bd.shape[0]
    cbuf8 = jnp.concatenate([jnp.zeros((b, SUBLANES - (CONV_W - 1), width), F32), conv_buf.astype(F32)], axis=1)
    row = lambda v: v.reshape(1, width).astype(F32)
    vec_spec = pl.BlockSpec((1, width), lambda bi, ci: (0, 0))
    seq_spec = pl.BlockSpec((1, tc, width), lambda bi, ci: (bi, ci, 0))
    wspec = pl.BlockSpec((nslab, MXU_DIM, MXU_DIM), lambda bi, ci: (0, 0, 0))
    out, hlast = pl.pallas_call(
        functools.partial(_lru_kernel, tc=tc, width=width, precise=precise),
        grid=(b, t // tc),
        in_specs=[
            seq_spec, seq_spec,
            pl.BlockSpec((1, SUBLANES, width), lambda bi, ci: (bi, 0, 0)),
            pl.BlockSpec((1, 1, width), lambda bi, ci: (bi, 0, 0)),
            pl.BlockSpec((CONV_W, width), lambda bi, ci: (0, 0)),
            vec_spec, wspec, vec_spec, wspec, vec_spec, vec_spec,
        ],
        out_specs=[seq_spec, pl.BlockSpec((1, 1, width), lambda bi, ci: (bi, 0, 0))],
        out_shape=[jax.ShapeDtypeStruct((b, t, width), act_dt), jax.ShapeDtypeStruct((b, 1, width), F32)],
        scratch_shapes=[
            pltpu.VMEM((tc + SUBLANES, width), F32),
            pltpu.VMEM((1, width), F32),
            pltpu.VMEM((tc, width), F32),
            pltpu.VMEM((tc, width), F32),
        ],
        compiler_params=_params(("parallel", "arbitrary")),
        name="lru_mixer",
    )(xb, gate, cbuf8, h0.reshape(b, 1, width).astype(F32), conv_w.astype(F32), row(conv_b), wa_bd,
      row(ba), wx_bd, row(bx), row(lam))
    return out, hlast.reshape(b, width)


def _diff_lambda(lq1, lk1, lq2, lk2, lam_init):
    s1 = jnp.sum(lq1[...] * lk1[...], axis=-1, keepdims=True)
    s2 = jnp.sum(lq2[...] * lk2[...], axis=-1, keepdims=True)
    return jnp.exp(s1) - jnp.exp(s2) + lam_init


_LOG2E_PARTS = (1.4453125, -0.00262451171875, 7.063150405883789e-06)
LOG2E = sum(_LOG2E_PARTS)
N_BIAS_COLS = 2 * len(_LOG2E_PARTS)
ATTN_GROUP = 8


def _attn_prompt_kernel(slopes_ref, q_ref, k_ref, v_ref, lq1, lk1, lq2, lk2, subln_ref, out_ref,
                        kaug, vt, qt, m_s, l_s, acc_s, s_s, diag_s, *, tq, lam_init):
    h = pl.program_id(1)
    qi = pl.program_id(2)
    slope = slopes_ref[h]
    nblk, tk, _ = kaug.shape
    nparts = len(_LOG2E_PARTS)

    @pl.when(qi == 0)
    def _():
        pos = lax.broadcasted_iota(jnp.int32, (tk, LANES), 0)
        lane = lax.broadcasted_iota(jnp.int32, (tk, LANES), 1)
        within = (pos % CHUNK).astype(F32) * slope
        for j in range(nblk):
            rows = slice(j * tk, (j + 1) * tk)
            kaug[j, :, 0:LANES] = k_ref[0, rows, :].astype(BF16)
            coarse = ((pos + j * tk) // CHUNK * CHUNK).astype(F32) * slope
            cols = jnp.where(lane < nparts, coarse, jnp.where(lane < N_BIAS_COLS, within, 0.0))
            kaug[j, :, LANES:2 * LANES] = cols.astype(BF16)
            vt[j] = v_ref[0, rows, :].T.astype(BF16)
        r = lax.broadcasted_iota(jnp.int32, (LANES, tq), 0)
        part = jnp.where(r % nparts == 0, _LOG2E_PARTS[0],
                         jnp.where(r % nparts == 1, _LOG2E_PARTS[1], _LOG2E_PARTS[2]))
        const_rows = jnp.where(r < N_BIAS_COLS, part, 0.0).astype(BF16)
        qt[0, LANES:2 * LANES, :] = const_rows
        qt[1, LANES:2 * LANES, :] = const_rows
        kpos = lax.broadcasted_iota(jnp.int32, (tk, tq), 0)
        qpos = lax.broadcasted_iota(jnp.int32, (tk, tq), 1)
        fix = jnp.where(kpos > qpos, (-2.0 * LOG2E) * slope * (kpos - qpos).astype(F32), 0.0)
        diag_s[...] = jnp.where((kpos // CHUNK) <= (qpos // CHUNK), fix, -jnp.inf)

    qs = q_ref[0] * (LOG2E * DIFF_HD ** -0.5)
    qlane = lax.broadcasted_iota(jnp.int32, qs.shape, 1)
    qt[0, 0:LANES, :] = jnp.where(qlane < DIFF_HD, qs, 0.0).T.astype(BF16)
    qt[1, 0:LANES, :] = jnp.where(qlane >= DIFF_HD, qs, 0.0).T.astype(BF16)
    m_s[...] = jnp.full(m_s.shape, -jnp.inf, F32)
    l_s[...] = jnp.zeros(l_s.shape, F32)
    acc_s[...] = jnp.zeros(acc_s.shape, F32)

    def blocks(kis, diagonal):
        slab = MXU_DIM
        colmax = [[None, None] for _ in kis]
        for j, ki in enumerate(kis):
            last = diagonal and j == len(kis) - 1
            for m in range(2):
                for r in range(tk // slab):
                    rows = slice(r * slab, (r + 1) * slab)
                    s = jnp.dot(kaug[ki, rows, :], qt[m], preferred_element_type=F32)
                    if last:
                        s = s + diag_s[rows, :]
                    s_s[j, m, rows, :] = s
                    pm = jnp.max(s, axis=0, keepdims=True)
                    colmax[j][m] = pm if r == 0 else jnp.maximum(colmax[j][m], pm)
        for j, ki in enumerate(kis):
            vblk = vt[ki]
            for m in range(2):
                m_old = m_s[m]
                m_new = jnp.maximum(m_old, colmax[j][m])
                p = jnp.exp2(s_s[j, m] - m_new)
                alpha = jnp.exp2(m_old - m_new)
                l_s[m] = alpha * l_s[m] + jnp.sum(p, axis=0, keepdims=True)
                acc_s[m] = alpha * acc_s[m] + jnp.dot(vblk, p.astype(BF16), preferred_element_type=F32)
                m_s[m] = m_new

    group = s_s.shape[0]

    def group_body(j, carry):
        blocks([group * j + t for t in range(group)], False)
        return carry

    lax.fori_loop(0, qi // group, group_body, 0)
    for r in range(group):
        @pl.when(qi % group == r)
        def _(r=r):
            blocks([qi - r + t for t in range(r + 1)], True)

    lam = _diff_lambda(lq1, lk1, lq2, lk2, lam_init)
    o = acc_s[0] / l_s[0] - lam * (acc_s[1] / l_s[1])
    o = o * lax.rsqrt(jnp.mean(o * o, axis=0, keepdims=True) + EPS) * subln_ref[...] * (1.0 - lam_init)
    out_ref[0] = o.T.astype(out_ref.dtype)


def _alibi_slopes():
    return 2.0 ** (-8.0 * jnp.arange(1, DIFF_HEADS + 1, dtype=F32) / DIFF_HEADS)


def diff_attention_prompt(q, k, v, lam_params, subln_g, lam_init, *, tq):
    b, t, aw = q.shape
    tq = min(tq, t)
    hd2 = 2 * DIFF_HD
    assert hd2 == LANES and DIFF_VD == LANES and tq % CHUNK == 0
    assert t // CHUNK <= 256, "chunk index must stay exact in bf16"
    lrow = lambda p: p.reshape(1, DIFF_HD).astype(F32)
    lspec = pl.BlockSpec((1, DIFF_HD), lambda bi, hi, qi: (0, 0))
    kv_spec = pl.BlockSpec((1, t, hd2), lambda bi, hi, qi: (bi, 0, hi))
    q_spec = pl.BlockSpec((1, tq, hd2), lambda bi, hi, qi: (bi, qi, hi))
    return pl.pallas_call(
        functools.partial(_attn_prompt_kernel, tq=tq, lam_init=lam_init),
        grid=(b, DIFF_HEADS, t // tq),
        in_specs=[
            pl.BlockSpec(memory_space=pltpu.SMEM),
            q_spec, kv_spec, kv_spec, lspec, lspec, lspec, lspec,
            pl.BlockSpec((DIFF_VD, 1), lambda bi, hi, qi: (0, 0)),
        ],
        out_specs=q_spec,
        out_shape=jax.ShapeDtypeStruct((b, t, aw), BF16),
        scratch_shapes=[
            pltpu.VMEM((t // tq, tq, 2 * LANES), BF16),
            pltpu.VMEM((t // tq, DIFF_VD, tq), BF16),
            pltpu.VMEM((2, 2 * LANES, tq), BF16),
            pltpu.VMEM((2, 1, tq), F32),
            pltpu.VMEM((2, 1, tq), F32),
            pltpu.VMEM((2, DIFF_VD, tq), F32),
            pltpu.VMEM((ATTN_GROUP, 2, tq, tq), F32),
            pltpu.VMEM((tq, tq), F32),
        ],
        compiler_params=_params(("parallel", "parallel", "arbitrary")),
        name="diff_attention_prompt",
    )(_alibi_slopes(), q, k, v, *[lrow(p) for p in lam_params], subln_g.reshape(DIFF_VD, 1).astype(F32))


def _attn_sample_kernel(q_ref, kp_ref, vp_ref, kn_ref, vn_ref, lq1, lk1, lq2, lk2, subln_ref, out_ref,
                        *, past, tq, lam_init):
    hd2 = 2 * DIFF_HD

    def bias_mask(nk, k_off):
        qpos = past + lax.broadcasted_iota(jnp.int32, (tq, nk), 0)
        kpos = k_off + lax.broadcasted_iota(jnp.int32, (tq, nk), 1)
        dist = jnp.abs(qpos - kpos).astype(F32)
        allowed = (kpos // CHUNK) <= (qpos // CHUNK)
        return dist, allowed

    dist_p, ok_p = bias_mask(past, 0)
    dist_n, ok_n = bias_mask(tq, past)
    lam = _diff_lambda(lq1, lk1, lq2, lk2, lam_init)
    qlane = lax.broadcasted_iota(jnp.int32, (tq, hd2), 1)
    for h in range(DIFF_HEADS):
        slope = 2.0 ** (-8.0 * (h + 1) / DIFF_HEADS)
        cols = slice(h * hd2, (h + 1) * hd2)
        q = q_ref[0, :, cols]
        kp, vp = kp_ref[0, :, h, :], vp_ref[0, :, h, :]
        kn, vn = kn_ref[0, :, cols], vn_ref[0, :, cols]
        probs = []
        q2 = jnp.concatenate([jnp.where(qlane < DIFF_HD, q, 0.0), jnp.where(qlane >= DIFF_HD, q, 0.0)], axis=0)
        sp2 = _dot_nt(q2, kp, "native") * (DIFF_HD ** -0.5)
        sn2 = _dot_nt(q2, kn, "native") * (DIFF_HD ** -0.5)
        for m in range(2):
            rows = slice(m * tq, (m + 1) * tq)
            sp = jnp.where(ok_p, sp2[rows] - slope * dist_p, -jnp.inf)
            sn = jnp.where(ok_n, sn2[rows] - slope * dist_n, -jnp.inf)
            mx = jnp.maximum(jnp.max(sp, axis=-1, keepdims=True), jnp.max(sn, axis=-1, keepdims=True))
            pp = jnp.exp(sp - mx)
            pn = jnp.exp(sn - mx)
            l = jnp.sum(pp, axis=-1, keepdims=True) + jnp.sum(pn, axis=-1, keepdims=True)
            probs.append((pp / l, pn / l))
        o = (_dot(probs[0][0] - lam * probs[1][0], vp, "native")
             + _dot(probs[0][1] - lam * probs[1][1], vn, "native"))
        out_ref[0, :, cols] = (o * lax.rsqrt(jnp.mean(o * o, axis=-1, keepdims=True) + EPS) * subln_ref[...]
                               * (1.0 - lam_init)).astype(out_ref.dtype)


def diff_attention_sample(q, k_new, v_new, past_k, past_v, lam_params, subln_g, lam_init):
    b, t, aw = q.shape
    past = past_k.shape[1]
    lrow = lambda p: p.reshape(1, DIFF_HD).astype(F32)
    lspec = pl.BlockSpec((1, DIFF_HD), lambda bi: (0, 0))
    new_spec = pl.BlockSpec((1, t, aw), lambda bi: (bi, 0, 0))
    past_spec = pl.BlockSpec((1, past, DIFF_HEADS, DIFF_VD), lambda bi: (bi, 0, 0, 0))
    return pl.pallas_call(
        functools.partial(_attn_sample_kernel, past=past, tq=t, lam_init=lam_init),
        grid=(b,),
        in_specs=[
            new_spec, past_spec, past_spec, new_spec, new_spec, lspec, lspec, lspec, lspec,
            pl.BlockSpec((1, DIFF_VD), lambda bi: (0, 0)),
        ],
        out_specs=new_spec,
        out_shape=jax.ShapeDtypeStruct((b, t, aw), F32),
        compiler_params=_params(("parallel",)),
        name="diff_attention_sample",
    )(q, past_k, past_v, k_new, v_new, *[lrow(p) for p in lam_params], subln_g.reshape(1, DIFF_VD).astype(F32))


def _cross_kernel(q_ref, mk_ref, mv_ref, out_ref, *, precise):
    d = q_ref.shape[-1]
    hd = d // X_HEADS
    per_head = len(mk_ref.shape) == 4
    for hh in range(X_HEADS):
        cols = slice(hh * hd, (hh + 1) * hd)
        mk = mk_ref[0, :, hh, :] if per_head else mk_ref[0, :, cols]
        mv = mv_ref[0, :, hh, :] if per_head else mv_ref[0, :, cols]
        s = _dot_nt(q_ref[0, :, cols], mk, precise) * (hd ** -0.5)
        p = jnp.exp(s - jnp.max(s, axis=-1, keepdims=True))
        p = p / jnp.sum(p, axis=-1, keepdims=True)
        out_ref[0, :, cols] = _dot(p, mv, precise).astype(out_ref.dtype)


def cross_attention_core(q, mk, mv, *, tq, precise):
    b, t, d = q.shape
    nm = mk.shape[1]
    tq = min(tq, t)
    q_spec = pl.BlockSpec((1, tq, d), lambda bi, qi: (bi, qi, 0))
    m_spec = pl.BlockSpec((1,) + mk.shape[1:], lambda bi, qi: (bi,) + (0,) * (mk.ndim - 1))
    return pl.pallas_call(
        functools.partial(_cross_kernel, precise=precise),
        grid=(b, t // tq),
        in_specs=[q_spec, m_spec, m_spec],
        out_specs=q_spec,
        out_shape=jax.ShapeDtypeStruct((b, t, d), F32 if precise else BF16),
        compiler_params=_params(("parallel", "arbitrary")),
        name="cross_attention_core",
    )(q, mk, mv)


def _first_argmax(vals, lane, valid):
    masked = jnp.where(valid, vals, -jnp.inf)
    mx = jnp.max(masked, axis=-1, keepdims=True)
    idx = jnp.min(jnp.where(masked == mx, lane, LANES), axis=-1, keepdims=True)
    return mx, idx


def _route(logits):
    lane = lax.broadcasted_iota(jnp.int32, logits.shape, 1).astype(F32)
    is_group = lane < N_GROUPS
    gmax, gidx = _first_argmax(logits, lane, is_group)
    gsum = jnp.sum(jnp.where(is_group, jnp.exp(logits - gmax), 0.0), axis=-1, keepdims=True)
    g_top = 1.0 / gsum
    lo = N_GROUPS + gidx * EXP_PER_GROUP
    in_group = (lane >= lo) & (lane < lo + EXP_PER_GROUP)
    e1, i1 = _first_argmax(logits, lane, in_group)
    e2, i2 = _first_argmax(logits, lane, in_group & (lane != i1))
    w2 = jnp.exp(e2 - e1)
    gate1 = g_top / (1.0 + w2)
    gate2 = g_top * w2 / (1.0 + w2)
    return i1 - N_GROUPS, i2 - N_GROUPS, gate1, gate2


def _combine_weights(logits):
    lane = lax.broadcasted_iota(jnp.int32, logits.shape, 1).astype(F32)
    x1, x2, gate1, gate2 = _route(logits)
    return jnp.where(lane == x1, gate1, 0.0) + jnp.where(lane == x2, gate2, 0.0)


def _moe_dense_kernel(x_ref, g_ref, wr_ref, wg_ref, wu_ref, wd_ref, fg_ref, out_ref, h_s, comb_s, acc_s,
                      *, precise_router):
    e = pl.program_id(1)

    @pl.when(e == 0)
    def _():
        h = _rms(x_ref[...], g_ref[...])
        h_s[...] = h.astype(BF16)
        comb_s[...] = _combine_weights(_dot(h, wr_ref[...], precise_router))
        acc_s[...] = x_ref[...]

    hb = h_s[...]
    act = _dot(hb, wg_ref[0])
    act = act * _sigmoid(act) * _dot(hb, wu_ref[0])
    y = _dot(act, wd_ref[0])
    lane = lax.broadcasted_iota(jnp.int32, comb_s.shape, 1)
    ce = jnp.sum(jnp.where(lane == e, comb_s[...], 0.0), axis=-1, keepdims=True)
    acc_s[...] += ce * y

    @pl.when(e == pl.num_programs(1) - 1)
    def _():
        out_ref[...] = _rms(acc_s[...], fg_ref[...])


def moe_dense_final(x, g, w_router_pad, wg, wu, wd, final_g, *, tm, precise_router):
    t, d = x.shape
    tm = min(tm, t)
    ne, _, ff = wg.shape
    return pl.pallas_call(
        functools.partial(_moe_dense_kernel, precise_router=precise_router),
        grid=(t // tm, ne),
        in_specs=[
            pl.BlockSpec((tm, d), lambda i, e: (i, 0)),
            pl.BlockSpec((1, d), lambda i, e: (0, 0)),
            pl.BlockSpec((d, LANES), lambda i, e: (0, 0)),
            pl.BlockSpec((1, d, ff), lambda i, e: (e, 0, 0)),
            pl.BlockSpec((1, d, ff), lambda i, e: (e, 0, 0)),
            pl.BlockSpec((1, ff, d), lambda i, e: (e, 0, 0)),
            pl.BlockSpec((1, d), lambda i, e: (0, 0)),
        ],
        out_specs=pl.BlockSpec((tm, d), lambda i, e: (i, 0)),
        out_shape=jax.ShapeDtypeStruct((t, d), F32),
        scratch_shapes=[pltpu.VMEM((tm, d), BF16), pltpu.VMEM((tm, LANES), F32), pltpu.VMEM((tm, d), F32)],
        compiler_params=_params(("parallel", "arbitrary")),
        name="moe_dense_final",
    )(x, g.reshape(1, d), w_router_pad, wg, wu, wd, final_g.reshape(1, d))


ROW_UNIT = 16
ROUTE_TILE = MXU_DIM
EXPERT_TILE = 512
UNIT_BITS = (16, 8, 4, 2, 1)


def _sorted_cap(tr):
    rows = 2 * tr + N_EXPERTS * (ROW_UNIT - 1)
    return -(-rows // MXU_DIM) * MXU_DIM


def _chunk_dma(units, make_copy):
    off = jnp.int32(0)
    for bit in UNIT_BITS:
        take = (units & bit) != 0

        @pl.when(take)
        def _(off=off, bit=bit):
            make_copy(off, bit).start()

        off = off + jnp.where(take, bit, 0)


def _wait_units(total_units, make_copy, max_units):
    bit = 1
    while bit <= max_units:
        @pl.when((total_units & bit) != 0)
        def _(bit=bit):
            make_copy(0, bit).wait()

        bit *= 2


def _rows(unit_start, units):
    return pl.ds(pl.multiple_of(unit_start * ROW_UNIT, ROW_UNIT), units * ROW_UNIT)


def _moe_route_kernel(res_ref, a_ref, wo_ref, g_ref, wr_ref, x_ref, xs_ref, info_ref, tab_ref, tot_ref, xc, run, sem,
                      *, tr, cap, seg_units):
    i = pl.program_id(0)
    nt = pl.num_programs(0)

    @pl.when(i == 0)
    def _():
        for e in range(N_EXPERTS):
            run[e] = 0

    x_ref[...] = res_ref[...] + jnp.dot(a_ref[...], wo_ref[...], preferred_element_type=F32)
    hb = _rms(x_ref[...], g_ref[...]).astype(BF16)
    e1, e2, g1, g2 = _route(jnp.dot(hb, wr_ref[...], preferred_element_type=F32))
    lane = lax.broadcasted_iota(jnp.int32, (tr, LANES), 1).astype(F32)
    a1 = lane == e1
    a2 = lane == e2
    assigned = jnp.where(a1 | a2, 1.0, 0.0)
    earlier = lax.broadcasted_iota(jnp.int32, (tr, tr), 1) < lax.broadcasted_iota(jnp.int32, (tr, tr), 0)
    rank = jnp.dot(jnp.where(earlier, 1.0, 0.0).astype(BF16), assigned.astype(BF16), preferred_element_type=F32)
    count = jnp.sum(assigned, axis=0, keepdims=True)
    units = jnp.floor((count + (ROW_UNIT - 1)) * (1.0 / ROW_UNIT))
    before = lax.broadcasted_iota(jnp.int32, (LANES, LANES), 0) < lax.broadcasted_iota(jnp.int32, (LANES, LANES), 1)
    units8 = jnp.broadcast_to(units, (SUBLANES, LANES)).astype(BF16)
    base = ROW_UNIT * jnp.dot(units8, jnp.where(before, 1.0, 0.0).astype(BF16), preferred_element_type=F32)[0:1]
    slot = base + rank
    slot1 = jnp.sum(jnp.where(a1, slot, 0.0), axis=1, keepdims=True)
    slot2 = jnp.sum(jnp.where(a2, slot, 0.0), axis=1, keepdims=True)
    info_ref[...] = jnp.where(lane == 0, slot1, jnp.where(lane == 1, slot2,
                              jnp.where(lane == 2, g1, jnp.where(lane == 3, g2, 0.0))))
    pos = lax.broadcasted_iota(jnp.int32, (tr, cap), 1).astype(F32)
    onehot_t = jnp.where((pos == slot1) | (pos == slot2), 1.0, 0.0).astype(BF16)
    xc[i % 2] = lax.dot_general(onehot_t, hb, (((0,), (0,)), ((), ())), preferred_element_type=F32).astype(BF16)

    def copies(tile, wait):
        buf = tile % 2
        src = jnp.int32(0)
        for e in range(N_EXPERTS):
            ne = tab_ref[tile * 2 * N_EXPERTS + N_EXPERTS + e]
            dst = e * seg_units + tab_ref[tile * 2 * N_EXPERTS + e]
            if not wait:
                _chunk_dma(ne, lambda off, bit, src=src, dst=dst: pltpu.make_async_copy(
                    xc.at[buf].at[_rows(src + off, bit)], xs_ref.at[_rows(dst + off, bit)], sem.at[buf]))
            src = src + ne
        if wait:
            _wait_units(src, lambda off, bit: pltpu.make_async_copy(
                xc.at[buf].at[_rows(off, bit)], xs_ref.at[_rows(off, bit)], sem.at[buf]), cap // ROW_UNIT)

    for e in range(N_EXPERTS):
        ne = units[0, e].astype(jnp.int32)
        tab_ref[i * 2 * N_EXPERTS + e] = run[e]
        tab_ref[i * 2 * N_EXPERTS + N_EXPERTS + e] = ne
        run[e] = run[e] + ne
    copies(i, False)

    @pl.when(i > 0)
    def _():
        copies(i - 1, True)

    @pl.when(i == nt - 1)
    def _():
        copies(i, True)
        fill = EXPERT_TILE // ROW_UNIT
        xc[0, pl.ds(0, EXPERT_TILE), :] = jnp.zeros((EXPERT_TILE, xc.shape[2]), BF16)
        tails = [pltpu.make_async_copy(xc.at[0].at[_rows(0, fill)],
                                       xs_ref.at[_rows(e * seg_units + run[e], fill)], sem.at[0])
                 for e in range(N_EXPERTS)]
        for cp in tails:
            cp.start()
        for cp in tails:
            cp.wait()
        for e in range(N_EXPERTS):
            tot_ref[e] = run[e]


def _moe_expert_kernel(eo_ref, rb_ref, valid_ref, xs_ref, wg_ref, wu_ref, wd_ref, ys_ref, wg_s, wu_s, wd_s):
    w = pl.program_id(0)

    @pl.when((w == 0) | (eo_ref[w] != eo_ref[jnp.maximum(w - 1, 0)]))
    def _():
        wg_s[...] = wg_ref[0].astype(BF16)
        wu_s[...] = wu_ref[0].astype(BF16)
        wd_s[...] = wd_ref[0].astype(BF16)

    @pl.when(valid_ref[w] == 1)
    def _():
        x = xs_ref[...]
        act = jnp.dot(x, wg_s[...], preferred_element_type=F32)
        act = act * _sigmoid(act) * jnp.dot(x, wu_s[...], preferred_element_type=F32)
        ys_ref[...] = jnp.dot(act.astype(BF16), wd_s[...], preferred_element_type=F32).astype(ys_ref.dtype)


def _moe_combine_kernel(tab_ref, x_ref, info_ref, ys_ref, fg_ref, out_ref, yc, sem, *, tr, cap, seg_units):
    i = pl.program_id(0)

    nt = pl.num_programs(0)

    def copies(tile, wait):
        buf = tile % 2
        dst = jnp.int32(0)
        for e in range(N_EXPERTS):
            ne = tab_ref[tile * 2 * N_EXPERTS + N_EXPERTS + e]
            src = e * seg_units + tab_ref[tile * 2 * N_EXPERTS + e]
            if not wait:
                _chunk_dma(ne, lambda off, bit, src=src, dst=dst: pltpu.make_async_copy(
                    ys_ref.at[_rows(src + off, bit)], yc.at[buf].at[_rows(dst + off, bit)], sem.at[buf]))
            dst = dst + ne
        if wait:
            _wait_units(dst, lambda off, bit: pltpu.make_async_copy(
                ys_ref.at[_rows(off, bit)], yc.at[buf].at[_rows(off, bit)], sem.at[buf]), cap // ROW_UNIT)

    @pl.when(i == 0)
    def _():
        yc[...] = jnp.zeros(yc.shape, yc.dtype)
        copies(i, False)

    @pl.when(i + 1 < nt)
    def _():
        copies(i + 1, False)

    copies(i, True)
    info = info_ref[...]
    pos = lax.broadcasted_iota(jnp.int32, (tr, cap), 1).astype(F32)
    rows = yc[i % 2]
    y1 = jnp.dot(jnp.where(pos == info[:, 0:1], 1.0, 0.0).astype(BF16), rows, preferred_element_type=F32)
    y2 = jnp.dot(jnp.where(pos == info[:, 1:2], 1.0, 0.0).astype(BF16), rows, preferred_element_type=F32)
    out_ref[...] = _rms(x_ref[...] + info[:, 2:3] * y1 + info[:, 3:4] * y2, fg_ref[...])


def moe_sparse_final(res, a, wo, g, w_router_pad, wg, wu, wd, final_g):
    t, d = res.shape
    tr, te = ROUTE_TILE, EXPERT_TILE
    assert t % tr == 0
    ntiles = t // tr
    cap = _sorted_cap(tr)
    ne, _, ff = wg.shape
    seg_rows = -(-(t + (ROW_UNIT - 1) * ntiles + te) // te) * te
    seg_units = seg_rows // ROW_UNIT
    smem = pl.BlockSpec(memory_space=pltpu.SMEM)

    row_spec = pl.BlockSpec((tr, d), lambda i: (i, 0))
    x, xs, info, tab, tot = pl.pallas_call(
        functools.partial(_moe_route_kernel, tr=tr, cap=cap, seg_units=seg_units),
        grid=(ntiles,),
        in_specs=[
            row_spec, row_spec,
            pl.BlockSpec((d, d), lambda i: (0, 0)),
            pl.BlockSpec((1, d), lambda i: (0, 0)),
            pl.BlockSpec((d, LANES), lambda i: (0, 0)),
        ],
        out_specs=[row_spec, pl.BlockSpec(memory_space=pl.ANY), pl.BlockSpec((tr, LANES), lambda i: (i, 0)),
                   smem, smem],
        out_shape=[
            jax.ShapeDtypeStruct((t, d), F32),
            jax.ShapeDtypeStruct((ne * seg_rows, d), BF16),
            jax.ShapeDtypeStruct((t, LANES), F32),
            jax.ShapeDtypeStruct((ntiles * 2 * ne,), jnp.int32),
            jax.ShapeDtypeStruct((ne,), jnp.int32),
        ],
        scratch_shapes=[pltpu.VMEM((2, cap, d), BF16), pltpu.SMEM((ne,), jnp.int32), pltpu.SemaphoreType.DMA((2,))],
        compiler_params=_params(("arbitrary",)),
        name="moe_route",
    )(res, a, wo, g.reshape(1, d), w_router_pad)

    tiles_per_e = (tot * ROW_UNIT + te - 1) // te
    ends = jnp.cumsum(tiles_per_e)
    n_items = ends[-1]
    max_items = (2 * t + ne * (ROW_UNIT - 1) * ntiles) // te + ne
    w = jnp.arange(max_items, dtype=jnp.int32)
    wc = jnp.minimum(w, n_items - 1)
    eo = jnp.sum((wc[:, None] >= ends[None, :]).astype(jnp.int32), axis=1)
    rb = (eo * (seg_rows // te) + wc - (ends - tiles_per_e)[eo]).astype(jnp.int32)
    valid = (w < n_items).astype(jnp.int32)

    ys = pl.pallas_call(
        _moe_expert_kernel,
        grid_spec=pltpu.PrefetchScalarGridSpec(
            num_scalar_prefetch=3,
            grid=(max_items,),
            in_specs=[
                pl.BlockSpec((te, d), lambda w, eo, rb, va: (rb[w], 0)),
                pl.BlockSpec((1, d, ff), lambda w, eo, rb, va: (eo[w], 0, 0)),
                pl.BlockSpec((1, d, ff), lambda w, eo, rb, va: (eo[w], 0, 0)),
                pl.BlockSpec((1, ff, d), lambda w, eo, rb, va: (eo[w], 0, 0)),
            ],
            out_specs=pl.BlockSpec((te, d), lambda w, eo, rb, va: (rb[w], 0)),
            scratch_shapes=[pltpu.VMEM((d, ff), BF16), pltpu.VMEM((d, ff), BF16), pltpu.VMEM((ff, d), BF16)],
        ),
        out_shape=jax.ShapeDtypeStruct((ne * seg_rows, d), BF16),
        compiler_params=_params(("arbitrary",)),
        name="moe_experts",
    )(eo, rb, valid, xs, wg, wu, wd)

    return pl.pallas_call(
        functools.partial(_moe_combine_kernel, tr=tr, cap=cap, seg_units=seg_units),
        grid_spec=pltpu.PrefetchScalarGridSpec(
            num_scalar_prefetch=1,
            grid=(ntiles,),
            in_specs=[
                pl.BlockSpec((tr, d), lambda i, tab: (i, 0)),
                pl.BlockSpec((tr, LANES), lambda i, tab: (i, 0)),
                pl.BlockSpec(memory_space=pl.ANY),
                pl.BlockSpec((1, d), lambda i, tab: (0, 0)),
            ],
            out_specs=pl.BlockSpec((tr, d), lambda i, tab: (i, 0)),
            scratch_shapes=[pltpu.VMEM((2, cap, d), BF16), pltpu.SemaphoreType.DMA((2,))],
        ),
        out_shape=jax.ShapeDtypeStruct((t, d), F32),
        compiler_params=_params(("arbitrary",)),
        name="moe_combine",
    )(tab, x, info, ys, final_g.reshape(1, d))


def _trunk(x, mem_k, mem_v, conv_buf, h0, past_k, past_v, p, lam_init):
    b, t, d = x.shape
    n = b * t
    xf = x.reshape(n, d)
    aw = DIFF_HEADS * DIFF_VD
    lru_w = p["lru_lambda"].shape[0]
    precise = past_k is not None
    tm = n if precise else 512
    tn = 1024
    assert t >= CONV_W - 1
    seq = lambda a: a.reshape(b, t, a.shape[-1])
    lam_params = (p["lam_q1"], p["lam_k1"], p["lam_q2"], p["lam_k2"])
    lru_args = (p["conv_w"], p["conv_b"], p["lru_wa"], p["lru_ba"].reshape(-1), p["lru_wx"],
                p["lru_bx"].reshape(-1), p["lru_lambda"])

    if precise:
        xb, gate, q, k, v = norm_linear_f32(xf, p["norm_mix_g"], p["w_in_f32"], tn=tn, split=True)
    else:
        xb, gate, q, k, v = norm_linear(xf, p["norm_mix_g"], p["w_in"], tm=tm, tn=tn,
                                        out_widths=[lru_w, lru_w, aw, aw, aw], out_dtypes=[F32] * 5)
    lru_out, h_last = lru_mixer(seq(xb), seq(gate), conv_buf, h0, *lru_args, tc=256, precise=precise)
    if precise:
        att = diff_attention_sample(seq(q), seq(k), seq(v), past_k, past_v, lam_params, p["subln_g"], lam_init)
    else:
        att = diff_attention_prompt(seq(q), seq(k), seq(v), lam_params, p["subln_g"], lam_init, tq=512)
    mix_in = [lru_out.reshape(n, lru_w), att.reshape(n, aw)]
    if precise:
        x1 = linear_residual_f32(xf, mix_in, p["w_out_f32"], tn=tn)
        (qx,) = norm_linear_f32(x1, p["norm_cross_g"], p["xq_w_f32"], tn=tn, split=False)
    else:
        x1, qx = linear_residual_norm_linear(xf, mix_in, p["w_out"], p["norm_cross_g"], p["xq_w"], tm=tm, tn=tn)
    o = cross_attention_core(seq(qx), mem_k, mem_v, tq=512, precise="native" if precise else False)
    experts = (p["exp_gate"], p["exp_up"], p["exp_down"])
    if precise:
        x2 = linear_residual_f32(x1, [o.reshape(n, d)], p["xo_w_f32"], tn=tn)
        y = moe_dense_final(x2, p["norm_ffn_g"], p["router_pad_f32"], *experts, p["final_norm_g"], tm=tm,
                            precise_router=True)
    else:
        y = moe_sparse_final(x1, o.reshape(n, d), p["xo_w"], p["norm_ffn_g"], p["router_pad"], *experts,
                             p["final_norm_g"])
    new_conv = seq(xb)[:, t - (CONV_W - 1):, :]
    return y.reshape(b, t, d), new_conv, h_last, k, v


def kernel(x_prompt, x_sample, cache_diff_k, cache_diff_v, cache_mem_k, cache_mem_v, state_conv, state_lru, mem_prompt, norm_mix_g, w_in, conv_w, conv_b, lru_wa, lru_ba, lru_wx, lru_bx, lru_lambda, lam_q1, lam_k1, lam_q2, lam_k2, subln_g, w_out, norm_cross_g, norm_mem_g, xq_w, xk_w, xv_w, xo_w, norm_ffn_g, router_group_w, router_expert_w, exp_gate, exp_up, exp_down, final_norm_g):
    depth = w_in.shape[0]
    assert depth == 1, "single-layer step"
    bp, tp, d = x_prompt.shape
    bs, ts, _ = x_sample.shape
    past = cache_diff_k.shape[2]
    n_mem = mem_prompt.shape[1]
    aw = DIFF_HEADS * DIFF_VD
    l = 0
    lam_init = 0.8 - 0.6 * math.exp(-0.3 * l)

    router = jnp.concatenate([router_group_w[l], router_expert_w[l]], axis=1)
    router_pad_f32 = jnp.pad(router, ((0, 0), (0, LANES - router.shape[1])))
    router_pad = router_pad_f32.astype(BF16)
    p = dict(router_pad_f32=router_pad_f32, w_in_f32=w_in[l], w_out_f32=w_out[l], xq_w_f32=xq_w[l],
             xo_w_f32=xo_w[l], **dict(norm_mix_g=norm_mix_g[l], conv_w=conv_w[l], conv_b=conv_b[l], lru_wa=lru_wa[l],
             lru_ba=lru_ba[l], lru_wx=lru_wx[l], lru_bx=lru_bx[l], lru_lambda=lru_lambda[l], lam_q1=lam_q1[l],
             lam_k1=lam_k1[l], lam_q2=lam_q2[l], lam_k2=lam_k2[l], subln_g=subln_g[l],
             norm_cross_g=norm_cross_g[l], norm_ffn_g=norm_ffn_g[l], router_pad=router_pad,
             final_norm_g=final_norm_g))
    for name, w in (("w_in", w_in), ("w_out", w_out), ("xq_w", xq_w), ("xo_w", xo_w)):
        p[name] = w[l].astype(BF16)
    p.update(exp_gate=exp_gate[l], exp_up=exp_up[l], exp_down=exp_down[l])

    memf = mem_prompt.reshape(bp * n_mem, d)
    w_mem = jnp.concatenate([xk_w[l].astype(BF16), xv_w[l].astype(BF16)], axis=1)
    mk_p, mv_p = norm_linear(memf, norm_mem_g[l], w_mem, tm=512, tn=1024, out_widths=[d, d], out_dtypes=[F32, F32])
    mk_p = mk_p.reshape(bp, n_mem, d)
    mv_p = mv_p.reshape(bp, n_mem, d)

    zero_buf = jnp.zeros((bp, CONV_W - 1, lru_lambda.shape[1]), F32)
    zero_h = jnp.zeros((bp, lru_lambda.shape[1]), F32)
    y_p, cb_p, hl_p, k_p, v_p = _trunk(x_prompt, mk_p, mv_p, zero_buf, zero_h, None, None, p, lam_init)
    y_s, cb_s, hl_s, k_s, v_s = _trunk(x_sample, cache_mem_k[l], cache_mem_v[l], state_conv[l], state_lru[l],
                                       cache_diff_k[l], cache_diff_v[l],
                                       p, lam_init)

    hd2 = 2 * DIFF_HD
    return (y_p, y_s,
            k_p.reshape(1, bp, tp, DIFF_HEADS, hd2), v_p.reshape(1, bp, tp, DIFF_HEADS, DIFF_VD),
            mk_p.reshape(1, bp, n_mem, X_HEADS, d // X_HEADS), mv_p.reshape(1, bp, n_mem, X_HEADS, d // X_HEADS),
            cb_p[None], hl_p[None],
            k_s.reshape(1, bs, ts, DIFF_HEADS, hd2), v_s.reshape(1, bs, ts, DIFF_HEADS, DIFF_VD),
            cb_s[None], hl_s[None].astype(state_lru.dtype))
```
